```python
import math
import jax
import jax.numpy as jnp
from jax import lax
import numpy as np

D_MODEL = 1024
BATCH = 8
SEQ = 4096
DEPTH = 1

NORM_EPS = 1e-5
SSM_WIDTH = 1024
SSM_GROUP_CH = 16
SSM_GROUPS = SSM_WIDTH // SSM_GROUP_CH
SSM_STATE = 64
DT_MIN = 1e-3
DT_MAX = 1e-1
HEAD_DIM = 64
N_Q_HEADS = 16
N_KV_HEADS = 4
Q_PER_KV = N_Q_HEADS // N_KV_HEADS
ATTN_WIDTH = N_Q_HEADS * HEAD_DIM
KV_WIDTH = N_KV_HEADS * HEAD_DIM
WINDOW = 128
BLOCK = 128
IN_WIDTH = SSM_WIDTH + ATTN_WIDTH + 2 * KV_WIDTH + 2 * D_MODEL
N_EXPERTS = 32
TOP_K = 4
D_FF = 1024
SWIGLU_LIMIT = 7.0
SWIGLU_ALPHA = 1.702
MOE_BLOCK = 512

kernel_name = 'hybrid_s5_swa_moe_block'


def rmsnorm(x, g):
    xf = x.astype(jnp.float32)
    y = xf * lax.rsqrt(jnp.mean(xf * xf, axis=-1, keepdims=True) + NORM_EPS)
    return (y * g.astype(jnp.float32)).astype(x.dtype)


def alibi_slopes(n_heads):
    h = jnp.arange(1, n_heads + 1, dtype=jnp.float32)
    return jnp.exp2(-8.0 * h / n_heads)


def s5_discretise(a_re, a_im, log_dt, b_re, b_im):
    a_re = a_re.astype(jnp.float32)
    a_im = a_im.astype(jnp.float32)
    b_re = b_re.astype(jnp.float32)
    b_im = b_im.astype(jnp.float32)
    dt = jnp.exp(log_dt.astype(jnp.float32))[:, None]
    mag = jnp.exp(a_re * dt)
    abar_re = mag * jnp.cos(a_im * dt)
    abar_im = mag * jnp.sin(a_im * dt)
    den = a_re * a_re + a_im * a_im
    q_re = ((abar_re - 1.0) * a_re + abar_im * a_im) / den
    q_im = (abar_im * a_re - (abar_re - 1.0) * a_im) / den
    bbar_re = q_re[..., None] * b_re - q_im[..., None] * b_im
    bbar_im = q_re[..., None] * b_im + q_im[..., None] * b_re
    return abar_re, abar_im, bbar_re, bbar_im


def _ssm_combine(earlier, later):
    ar_i, ai_i, br_i, bi_i = earlier
    ar_j, ai_j, br_j, bi_j = later
    ar = ar_j * ar_i - ai_j * ai_i
    ai = ar_j * ai_i + ai_j * ar_i
    br = ar_j * br_i - ai_j * bi_i + br_j
    bi = ar_j * bi_i + ai_j * br_i + bi_j
    return ar, ai, br, bi


def s5_branch(u, a_re, a_im, log_dt, b_re, b_im, c_re, c_im, d_skip, w_glu):
    abar_re, abar_im, bbar_re, bbar_im = s5_discretise(a_re, a_im, log_dt, b_re, b_im)
    c_re = c_re.astype(jnp.float32)
    c_im = c_im.astype(jnp.float32)
    d_skip = d_skip.astype(jnp.float32)
    seq_len = u.shape[1]
    state_shape = (seq_len, SSM_GROUPS, SSM_STATE)

    def one_sequence(u_seq):
        ug = u_seq.astype(jnp.float32).reshape(seq_len, SSM_GROUPS, SSM_GROUP_CH)
        bu_re = jnp.einsum('lgh,gph->lgp', ug, bbar_re)
        bu_im = jnp.einsum('lgh,gph->lgp', ug, bbar_im)
        _, _, s_re, s_im = lax.associative_scan(
            _ssm_combine,
            (jnp.broadcast_to(abar_re, state_shape), jnp.broadcast_to(abar_im, state_shape), bu_re, bu_im),
            axis=0)
        y = (jnp.einsum('lgp,ghp->lgh', s_re, c_re)
             - jnp.einsum('lgp,ghp->lgh', s_im, c_im)
             + d_skip * ug)
        return y.reshape(seq_len, SSM_WIDTH)

    y = jax.nn.gelu(lax.map(one_sequence, u)).astype(u.dtype)
    val, gate = jnp.split(y @ w_glu, 2, axis=-1)
    return val * jax.nn.sigmoid(gate)


def swa_attention(q, k, v, q_norm_g, k_norm_g, sinks):
    bsz, seq_len = q.shape[0], q.shape[1]
    n_blk = seq_len // BLOCK
    q = rmsnorm(q.reshape(bsz, seq_len, N_Q_HEADS, HEAD_DIM), q_norm_g)
    k = rmsnorm(k.reshape(bsz, seq_len, N_KV_HEADS, HEAD_DIM), k_norm_g)
    v = v.reshape(bsz, seq_len, N_KV_HEADS, HEAD_DIM)
    qb = q.reshape(bsz, n_blk, BLOCK, N_KV_HEADS, Q_PER_KV, HEAD_DIM)

    def with_previous_block(t):
        t = t.reshape(bsz, n_blk, BLOCK, N_KV_HEADS, HEAD_DIM)
        prev = jnp.concatenate([jnp.zeros_like(t[:, :1]), t[:, :-1]], axis=1)
        return jnp.concatenate([prev, t], axis=2)

    kw = with_previous_block(k)
    vw = with_previous_block(v)
    scale = 1.0 / math.sqrt(HEAD_DIM)
    scores = jnp.einsum('bnqkgd,bnskd->bnkgqs', qb, kw,
                        preferred_element_type=jnp.float32) * scale
    qi = jnp.arange(BLOCK)[:, None]
    kj = jnp.arange(2 * BLOCK)[None, :]
    dist = qi - kj + BLOCK
    blk = jnp.arange(n_blk)[:, None, None]
    valid = (dist >= 0) & (dist < WINDOW) & ((blk > 0) | (kj >= BLOCK))
    slopes = alibi_slopes(N_Q_HEADS).reshape(N_KV_HEADS, Q_PER_KV)
    bias = -slopes[:, :, None, None] * dist.astype(jnp.float32)
    scores = jnp.where(valid[None, :, None, None], scores + bias, -jnp.inf)
    sink = sinks.astype(jnp.float32).reshape(N_KV_HEADS, Q_PER_KV)[None, None, :, :, None, None]
    m = jnp.maximum(jnp.max(scores, axis=-1, keepdims=True), sink)
    p = jnp.exp(scores - m)
    probs = (p / (jnp.sum(p, axis=-1, keepdims=True) + jnp.exp(sink - m))).astype(v.dtype)
    out = jnp.einsum('bnkgqs,bnskd->bnqkgd', probs, vw)
    return out.reshape(bsz, seq_len, ATTN_WIDTH)


def moe_ffn(h, w_router, b_router, w_gate_up, b_gate_up, w_down, b_down):
    bsz, seq_len, d = h.shape
    n_tok = bsz * seq_len
    n_assign = n_tok * TOP_K
    xf = h.reshape(n_tok, d)
    logits = (xf @ w_router).astype(jnp.float32) + b_router.astype(jnp.float32)
    top_v, top_i = lax.top_k(logits, TOP_K)
    gate_w = jax.nn.softmax(top_v, axis=-1)
    flat_e = top_i.reshape(-1)
    order = jnp.argsort(flat_e)
    e_sorted = flat_e[order]
    tok_sorted = order // TOP_K
    w_sorted = gate_w.reshape(-1)[order]
    counts = jnp.bincount(flat_e, length=N_EXPERTS)
    padded = (counts + MOE_BLOCK - 1) // MOE_BLOCK * MOE_BLOCK
    starts = jnp.cumsum(counts) - counts
    pad_ends = jnp.cumsum(padded)
    pad_starts = pad_ends - padded
    dest = pad_starts[e_sorted] + (jnp.arange(n_assign) - starts[e_sorted])
    n_blocks = -(-n_assign // MOE_BLOCK) + N_EXPERTS
    n_rows = n_blocks * MOE_BLOCK
    row_tok = jnp.zeros((n_rows,), jnp.int32).at[dest].set(tok_sorted.astype(jnp.int32))
    row_w = jnp.zeros((n_rows,), jnp.float32).at[dest].set(w_sorted)
    block_e = jnp.minimum(
        jnp.searchsorted(pad_ends, jnp.arange(n_blocks) * MOE_BLOCK, side='right'), N_EXPERTS - 1)
    xs = xf[row_tok].reshape(n_blocks, MOE_BLOCK, d)

    def expert_block(args):
        xb, e = args
        gu = xb @ w_gate_up[e] + b_gate_up[e]
        gate, up = jnp.split(gu, 2, axis=-1)
        gate = jnp.minimum(gate, SWIGLU_LIMIT)
        up = jnp.clip(up, -SWIGLU_LIMIT, SWIGLU_LIMIT)
        act = (up + 1.0) * gate * jax.nn.sigmoid(SWIGLU_ALPHA * gate)
        return act @ w_down[e] + b_down[e]

    ys = lax.map(expert_block, (xs, block_e)).reshape(n_rows, d)
    out = jnp.zeros((n_tok, d), ys.dtype).at[row_tok].add(ys * row_w[:, None].astype(ys.dtype))
    return out.reshape(bsz, seq_len, d).astype(h.dtype)


def hybrid_layer(x, norm1_g, w_in, ssm_a_re, ssm_a_im, ssm_log_dt, ssm_b_re, ssm_b_im,
                 ssm_c_re, ssm_c_im, ssm_d, w_glu, w_ssm_proj, q_norm_g, k_norm_g, attn_sinks,
                 w_attn_proj, w_out, norm2_g, w_router, b_router, w_gate_up, b_gate_up,
                 w_down, b_down):
    h = rmsnorm(x, norm1_g)
    proj = h @ w_in
    cuts = [SSM_WIDTH, SSM_WIDTH + ATTN_WIDTH, SSM_WIDTH + ATTN_WIDTH + KV_WIDTH,
            SSM_WIDTH + ATTN_WIDTH + 2 * KV_WIDTH, SSM_WIDTH + ATTN_WIDTH + 2 * KV_WIDTH + D_MODEL]
    u_ssm, q, k, v, g_ssm, g_attn = jnp.split(proj, cuts, axis=-1)
    y_ssm = s5_branch(u_ssm, ssm_a_re, ssm_a_im, ssm_log_dt, ssm_b_re, ssm_b_im,
                      ssm_c_re, ssm_c_im, ssm_d, w_glu) @ w_ssm_proj
    y_attn = swa_attention(q, k, v, q_norm_g, k_norm_g, attn_sinks) @ w_attn_proj
    mixed = jax.nn.sigmoid(g_ssm) * y_ssm + jax.nn.sigmoid(g_attn) * y_attn
    x = x + mixed @ w_out
    x = x + moe_ffn(rmsnorm(x, norm2_g), w_router, b_router, w_gate_up, b_gate_up, w_down, b_down)
    return x


def setup_inputs(seed: int = 0) -> dict:
    key = jax.random.key(seed)
    ks = jax.random.split(key, 25)
    f32 = jnp.float32

    def nrm(k, shape, scale):
        return scale * jax.random.normal(k, shape, f32)

    n_l = DEPTH
    g, p, hc = SSM_GROUPS, SSM_STATE, SSM_GROUP_CH
    return {
        'x': nrm(ks[0], (BATCH, SEQ, D_MODEL), 1.0),
        'norm1_g': 1.0 + nrm(ks[1], (n_l, D_MODEL), 0.02),
        'w_in': nrm(ks[2], (n_l, D_MODEL, IN_WIDTH), D_MODEL ** -0.5),
        'ssm_a_re': -0.5 + nrm(ks[3], (n_l, g, p), 0.01),
        'ssm_a_im': math.pi * jnp.arange(p, dtype=f32) + nrm(ks[4], (n_l, g, p), 0.01),
        'ssm_log_dt': jax.random.uniform(ks[5], (n_l, g), f32, math.log(DT_MIN), math.log(DT_MAX)),
        'ssm_b_re': nrm(ks[6], (n_l, g, p, hc), (2 * hc) ** -0.5),
        'ssm_b_im': nrm(ks[7], (n_l, g, p, hc), (2 * hc) ** -0.5),
        'ssm_c_re': nrm(ks[8], (n_l, g, hc, p), p ** -0.5),
        'ssm_c_im': nrm(ks[9], (n_l, g, hc, p), p ** -0.5),
        'ssm_d': nrm(ks[10], (n_l, g, hc), 1.0),
        'w_glu': nrm(ks[11], (n_l, SSM_WIDTH, 2 * SSM_WIDTH), SSM_WIDTH ** -0.5),
        'w_ssm_proj': nrm(ks[12], (n_l, SSM_WIDTH, D_MODEL), SSM_WIDTH ** -0.5),
        'q_norm_g': 1.0 + nrm(ks[13], (n_l, HEAD_DIM), 0.02),
        'k_norm_g': 1.0 + nrm(ks[14], (n_l, HEAD_DIM), 0.02),
        'attn_sinks': nrm(ks[15], (n_l, N_Q_HEADS), 1.0),
        'w_attn_proj': nrm(ks[16], (n_l, ATTN_WIDTH, D_MODEL), ATTN_WIDTH ** -0.5),
        'w_out': nrm(ks[17], (n_l, D_MODEL, D_MODEL), D_MODEL ** -0.5),
        'norm2_g': 1.0 + nrm(ks[18], (n_l, D_MODEL), 0.02),
        'w_router': nrm(ks[19], (n_l, D_MODEL, N_EXPERTS), D_MODEL ** -0.5),
        'b_router': nrm(ks[20], (n_l, N_EXPERTS), 0.01),
        'w_gate_up': nrm(ks[21], (n_l, N_EXPERTS, D_MODEL, 2 * D_FF), D_MODEL ** -0.5),
        'b_gate_up': nrm(ks[22], (n_l, N_EXPERTS, 2 * D_FF), 0.01),
        'w_down': nrm(ks[23], (n_l, N_EXPERTS, D_FF, D_MODEL), D_FF ** -0.5),
        'b_down': nrm(ks[24], (n_l, N_EXPERTS, D_MODEL), 0.01),
    }


def reference(x, norm1_g, w_in, ssm_a_re, ssm_a_im, ssm_log_dt, ssm_b_re, ssm_b_im,
              ssm_c_re, ssm_c_im, ssm_d, w_glu, w_ssm_proj, q_norm_g, k_norm_g, attn_sinks,
              w_attn_proj, w_out, norm2_g, w_router, b_router, w_gate_up, b_gate_up,
              w_down, b_down):
    for layer in range(DEPTH):
        x = hybrid_layer(
            x, norm1_g[layer], w_in[layer], ssm_a_re[layer], ssm_a_im[layer], ssm_log_dt[layer],
            ssm_b_re[layer], ssm_b_im[layer], ssm_c_re[layer], ssm_c_im[layer], ssm_d[layer],
            w_glu[layer], w_ssm_proj[layer], q_norm_g[layer], k_norm_g[layer], attn_sinks[layer],
            w_attn_proj[layer], w_out[layer], norm2_g[layer], w_router[layer], b_router[layer],
            w_gate_up[layer], b_gate_up[layer], w_down[layer], b_down[layer])
    return x
```

```python
import functools

import jax
import jax.numpy as jnp
from jax import lax
from jax.experimental import pallas as pl
from jax.experimental.pallas import tpu as pltpu

D_MODEL = 1024
NORM_EPS = 1e-5
SSM_WIDTH = 1024
SSM_GROUP_CH = 16
SSM_GROUPS = SSM_WIDTH // SSM_GROUP_CH
SSM_STATE = 64
HEAD_DIM = 64
N_Q_HEADS = 16
N_KV_HEADS = 4
Q_PER_KV = N_Q_HEADS // N_KV_HEADS
ATTN_WIDTH = N_Q_HEADS * HEAD_DIM
KV_WIDTH = N_KV_HEADS * HEAD_DIM
WINDOW = 128
BLOCK = 128
N_EXPERTS = 32
TOP_K = 4
D_FF = 1024
SWIGLU_LIMIT = 7.0
SWIGLU_ALPHA = 1.702
MOE_BLOCK = 512

LANES = 128
SUBLANES = 8
VMEM_LIMIT = 56 * 1024 * 1024

S5_GROUP_BLOCK = 8
S5_CH = S5_GROUP_BLOCK * SSM_GROUP_CH
S5_ST = S5_GROUP_BLOCK * SSM_STATE
S5_CHUNK = 256
S5_PITCH = S5_CHUNK + SUBLANES

TM_IN = 256
TM_MIX = 512
TT_ROUTE = 512
TT_MOVE = 256

F32 = jnp.float32
BF16 = jnp.bfloat16


def _cparams(sem):
    return pltpu.CompilerParams(dimension_semantics=sem, vmem_limit_bytes=VMEM_LIMIT)


def _rms(x, g):
    return x * lax.rsqrt(jnp.mean(x * x, axis=-1, keepdims=True) + NORM_EPS) * g


def _in_proj_body(x_ref, g_ref, w_ref, u_ref, q_ref, k_ref, v_ref, gs_ref, ga_ref):
    h = _rms(x_ref[...], g_ref[...]).astype(BF16)
    lo = 0
    for ref in (u_ref, q_ref, k_ref, v_ref, gs_ref, ga_ref):
        hi = lo + ref.shape[-1]
        ref[...] = jnp.dot(h, w_ref[:, lo:hi], preferred_element_type=F32)
        lo = hi


def _in_proj(x2, norm_g, w_in_bf):
    t = x2.shape[0]
    widths = (SSM_WIDTH, ATTN_WIDTH, KV_WIDTH, KV_WIDTH, D_MODEL, D_MODEL)
    in_width = sum(widths)
    return pl.pallas_call(
        _in_proj_body,
        name="in_proj",
        grid=(t // TM_IN,),
        in_specs=[
            pl.BlockSpec((TM_IN, D_MODEL), lambda i: (i, 0)),
            pl.BlockSpec((1, D_MODEL), lambda i: (0, 0)),
            pl.BlockSpec((D_MODEL, in_width), lambda i: (0, 0)),
        ],
        out_specs=[pl.BlockSpec((TM_IN, w), lambda i: (i, 0)) for w in widths],
        out_shape=[jax.ShapeDtypeStruct((t, w), F32) for w in widths],
        compiler_params=_cparams(("parallel",)),
    )(x2, norm_g, w_in_bf)


def _s5_prep_body(are_ref, aim_ref, ldt_ref, bre_ref, bim_ref,
                  abr_ref, abi_ref, bbr_ref, bbi_ref):
    a_re = are_ref[...]
    a_im = aim_ref[...]
    dt = jnp.exp(ldt_ref[...])
    mag = jnp.exp(a_re * dt)
    abar_re = mag * jnp.cos(a_im * dt)
    abar_im = mag * jnp.sin(a_im * dt)
    den = a_re * a_re + a_im * a_im
    q_re = ((abar_re - 1.0) * a_re + abar_im * a_im) / den
    q_im = (abar_im * a_re - (abar_re - 1.0) * a_im) / den
    abr_ref[...] = abar_re
    abi_ref[...] = abar_im
    for h in range(SSM_GROUP_CH):
        bbr_ref[h] = q_re * bre_ref[h] - q_im * bim_ref[h]
        bbi_ref[h] = q_re * bim_ref[h] + q_im * bre_ref[h]


def _s5_prep(a_re, a_im, log_dt, b_re, b_im):
    g, p, hc = SSM_GROUPS, SSM_STATE, SSM_GROUP_CH
    bt_re = jnp.transpose(b_re, (2, 0, 1))
    bt_im = jnp.transpose(b_im, (2, 0, 1))
    return pl.pallas_call(
        _s5_prep_body,
        name="s5_prep",
        out_shape=[
            jax.ShapeDtypeStruct((g, p), F32),
            jax.ShapeDtypeStruct((g, p), F32),
            jax.ShapeDtypeStruct((hc, g, p), F32),
            jax.ShapeDtypeStruct((hc, g, p), F32),
        ],
    )(a_re, a_im, log_dt.reshape(g, 1), bt_re, bt_im)


def _s5_matrices(abar_re, abar_im, bbt_re, bbt_im, c_re, c_im):
    nb, gb, hc, p = SSM_GROUPS // S5_GROUP_BLOCK, S5_GROUP_BLOCK, SSM_GROUP_CH, SSM_STATE
    eye = jnp.eye(gb, dtype=F32)

    def b_block(bbt):
        t = jnp.transpose(bbt, (1, 0, 2)).reshape(nb, gb, hc, p)
        return jnp.einsum('jghp,gk->jghkp', t, eye).reshape(nb, gb * hc, gb * p)

    def c_block(c):
        t = c.reshape(nb, gb, hc, p)
        return jnp.einsum('jghp,gk->jgpkh', t, eye).reshape(nb, gb * p, gb * hc)

    b_mat = jnp.concatenate([b_block(bbt_re), b_block(bbt_im)], axis=2).astype(BF16)
    c_mat = jnp.concatenate([c_block(c_re), c_block(-c_im)], axis=1).astype(BF16)
    ab = jnp.concatenate([abar_re.reshape(nb, 1, gb * p), abar_im.reshape(nb, 1, gb * p)], axis=2)
    ab = jnp.broadcast_to(ab, (nb, SUBLANES, 2 * gb * p))
    return b_mat, c_mat, ab


def _s5_body(u_ref, b_ref, c_ref, ab_ref, d_ref, o_ref, ubuf, utm, sbuf, state, ytm):
    n_b = u_ref.shape[0]
    chunk = u_ref.shape[1]

    @pl.when(pl.program_id(1) == 0)
    def _():
        state[...] = jnp.zeros_like(state)

    for b in range(n_b):
        ubuf[b * S5_PITCH:b * S5_PITCH + chunk, :] = u_ref[b]

    def to_time_major(l8, carry):
        for i in range(SUBLANES):
            l = l8 * SUBLANES + i
            row = pl.multiple_of(l * n_b, SUBLANES)
            utm[pl.ds(row, n_b), :] = ubuf[pl.ds(l, n_b, stride=S5_PITCH), :]
        return carry

    lax.fori_loop(0, chunk // SUBLANES, to_time_major, 0)

    sbuf[...] = jnp.dot(utm[...].astype(BF16), b_ref[...], preferred_element_type=F32)

    a_re = ab_ref[:, :S5_ST]
    a_im = ab_ref[:, S5_ST:]

    def scan(l8, carry):
        s_re, s_im = carry
        for i in range(SUBLANES):
            row = pl.multiple_of((l8 * SUBLANES + i) * n_b, SUBLANES)
            n_re = a_re * s_re - a_im * s_im + sbuf[pl.ds(row, n_b), :S5_ST]
            n_im = a_re * s_im + a_im * s_re + sbuf[pl.ds(row, n_b), S5_ST:]
            sbuf[pl.ds(row, n_b), :S5_ST] = n_re
            sbuf[pl.ds(row, n_b), S5_ST:] = n_im
            s_re, s_im = n_re, n_im
        return s_re, s_im

    s_re, s_im = lax.fori_loop(0, chunk // SUBLANES, scan, (state[:, :S5_ST], state[:, S5_ST:]))
    state[:, :S5_ST] = s_re
    state[:, S5_ST:] = s_im

    y = jnp.dot(sbuf[...].astype(BF16), c_ref[...], preferred_element_type=F32)
    y = y + d_ref[...] * utm[...]
    ytm[...] = jax.nn.gelu(y)
    for b in range(n_b):
        o_ref[b] = ytm[pl.ds(b, chunk, stride=n_b), :].astype(o_ref.dtype)


def _s5(u3, b_mat, c_mat, ab, d_flat):
    n_b, seq, _ = u3.shape
    assert n_b == SUBLANES, "the scan keeps one sequence per sublane"
    chunk = min(S5_CHUNK, seq)
    nb = SSM_GROUPS // S5_GROUP_BLOCK
    return pl.pallas_call(
        _s5_body,
        name="s5_scan",
        grid=(nb, seq // chunk),
        in_specs=[
            pl.BlockSpec((n_b, chunk, S5_CH), lambda j, c: (0, c, j)),
            pl.BlockSpec((None, S5_CH, 2 * S5_ST), lambda j, c: (j, 0, 0)),
            pl.BlockSpec((None, 2 * S5_ST, S5_CH), lambda j, c: (j, 0, 0)),
            pl.BlockSpec((None, SUBLANES, 2 * S5_ST), lambda j, c: (j, 0, 0)),
            pl.BlockSpec((1, S5_CH), lambda j, c: (0, j)),
        ],
        out_specs=pl.BlockSpec((n_b, chunk, S5_CH), lambda j, c: (0, c, j)),
        out_shape=jax.ShapeDtypeStruct((n_b, seq, SSM_WIDTH), BF16),
        scratch_shapes=[
            pltpu.VMEM((n_b * S5_PITCH, S5_CH), F32),
            pltpu.VMEM((chunk * n_b, S5_CH), F32),
            pltpu.VMEM((chunk * n_b, 2 * S5_ST), F32),
            pltpu.VMEM((n_b, 2 * S5_ST), F32),
            pltpu.VMEM((chunk * n_b, S5_CH), F32),
        ],
        compiler_params=_cparams(("parallel", "arbitrary")),
    )(u3, b_mat, c_mat, ab, d_flat)


def _attn_body(sinks_ref, q_ref, kp_ref, kc_ref, vp_ref, vc_ref, bias_ref, qg_ref, kg_ref, o_ref):
    qg = qg_ref[...] * (HEAD_DIM ** -0.5)
    kg = kg_ref[...]
    kw = jnp.concatenate([kp_ref[...], kc_ref[...]], axis=0)
    vw = jnp.concatenate([vp_ref[...], vc_ref[...]], axis=0).astype(BF16)
    outs = []
    for kv in range(N_KV_HEADS):
        k_n = _rms(kw[:, kv * HEAD_DIM:(kv + 1) * HEAD_DIM], kg).astype(BF16)
        qs = []
        for g in range(Q_PER_KV):
            n = kv * Q_PER_KV + g
            qs.append(_rms(q_ref[:, n * HEAD_DIM:(n + 1) * HEAD_DIM], qg).astype(BF16))
        q_st = jnp.concatenate(qs, axis=0)
        s = lax.dot_general(q_st, k_n, (((1,), (1,)), ((), ())), preferred_element_type=F32)
        s = s + bias_ref[kv]
        ps = []
        for g in range(Q_PER_KV):
            sg = s[g * BLOCK:(g + 1) * BLOCK]
            sink = sinks_ref[kv * Q_PER_KV + g]
            m = jnp.maximum(jnp.max(sg, axis=-1, keepdims=True), sink)
            p = jnp.exp(sg - m)
            den = jnp.sum(p, axis=-1, keepdims=True) + jnp.exp(sink - m)
            ps.append((p / den).astype(BF16))
        p_st = jnp.concatenate(ps, axis=0)
        o = jnp.dot(p_st, vw[:, kv * HEAD_DIM:(kv + 1) * HEAD_DIM], preferred_element_type=F32)
        for g in range(Q_PER_KV):
            outs.append(o[g * BLOCK:(g + 1) * BLOCK])
    o_ref[...] = jnp.concatenate(outs, axis=1).astype(o_ref.dtype)


def _attn_bias():
    h = jnp.arange(1, N_Q_HEADS + 1, dtype=F32)
    slopes = jnp.exp2(-8.0 * h / N_Q_HEADS)
    qi = jnp.arange(BLOCK)[:, None]
    kj = jnp.arange(2 * BLOCK)[None, :]
    dist = qi - kj + BLOCK
    band = (dist >= 0) & (dist < WINDOW)
    bias = -slopes[:, None, None] * dist.astype(F32)
    later = jnp.where(band[None], bias, -jnp.inf)
    first = jnp.where((band & (kj >= BLOCK))[None], bias, -jnp.inf)
    both = jnp.stack([first, later])
    return both.reshape(2, N_KV_HEADS, Q_PER_KV * BLOCK, 2 * BLOCK)


def _attention(q3, k3, v3, q_norm_g, k_norm_g, sinks):
    n_b, seq, _ = q3.shape
    n_blk = seq // BLOCK
    prev = lambda b, i: (b, jnp.maximum(i - 1, 0), 0)
    cur = lambda b, i: (b, i, 0)
    return pl.pallas_call(
        _attn_body,
        name="swa_attention",
        grid=(n_b, n_blk),
        in_specs=[
            pl.BlockSpec(memory_space=pltpu.SMEM),
            pl.BlockSpec((None, BLOCK, ATTN_WIDTH), cur),
            pl.BlockSpec((None, BLOCK, KV_WIDTH), prev),
            pl.BlockSpec((None, BLOCK, KV_WIDTH), cur),
            pl.BlockSpec((None, BLOCK, KV_WIDTH), prev),
            pl.BlockSpec((None, BLOCK, KV_WIDTH), cur),
            pl.BlockSpec((None, N_KV_HEADS, Q_PER_KV * BLOCK, 2 * BLOCK),
                         lambda b, i: (jnp.minimum(i, 1), 0, 0, 0)),
            pl.BlockSpec((1, HEAD_DIM), lambda b, i: (0, 0)),
            pl.BlockSpec((1, HEAD_DIM), lambda b, i: (0, 0)),
        ],
        out_specs=pl.BlockSpec((None, BLOCK, ATTN_WIDTH), cur),
        out_shape=jax.ShapeDtypeStruct((n_b, seq, ATTN_WIDTH), BF16),
        compiler_params=_cparams(("parallel", "arbitrary")),
    )(sinks, q3, k3, k3, v3, v3, _attn_bias(), q_norm_g.reshape(1, HEAD_DIM),
      k_norm_g.reshape(1, HEAD_DIM))


def _mix_body(y_ref, a_ref, gs_ref, ga_ref, x_ref, wglu_ref, wsp_ref, wap_ref, wo_ref,
              g2_ref, wr_ref, br_ref, x1_ref, lg_ref):
    z = jnp.dot(y_ref[...], wglu_ref[...], preferred_element_type=F32)
    glu = z[:, :SSM_WIDTH] * jax.nn.sigmoid(z[:, SSM_WIDTH:])
    y_ssm = jnp.dot(glu.astype(BF16), wsp_ref[...], preferred_element_type=F32)
    y_attn = jnp.dot(a_ref[...], wap_ref[...], preferred_element_type=F32)
    mixed = jax.nn.sigmoid(gs_ref[...]) * y_ssm + jax.nn.sigmoid(ga_ref[...]) * y_attn
    x1 = x_ref[...] + jnp.dot(mixed.astype(BF16), wo_ref[...], preferred_element_type=F32)
    x1_ref[...] = x1
    h2 = _rms(x1, g2_ref[...]).astype(BF16)
    lg_ref[...] = jnp.dot(h2, wr_ref[...], preferred_element_type=F32) + br_ref[...]


def _mix(y2, a2, gs, ga, x2, w_glu, w_sp, w_ap, w_o, norm2_g, w_r_pad, b_r_pad):
    t = x2.shape[0]
    tm = min(TM_MIX, t)
    tok = lambda w: pl.BlockSpec((tm, w), lambda i: (i, 0))
    full = lambda a: pl.BlockSpec(a.shape, lambda i: (0, 0))
    return pl.pallas_call(
        _mix_body,
        name="mix",
        grid=(t // tm,),
        in_specs=[tok(SSM_WIDTH), tok(ATTN_WIDTH), tok(D_MODEL), tok(D_MODEL), tok(D_MODEL),
                  full(w_glu), full(w_sp), full(w_ap), full(w_o), full(norm2_g),
                  full(w_r_pad), full(b_r_pad)],
        out_specs=[tok(D_MODEL), tok(LANES)],
        out_shape=[jax.ShapeDtypeStruct((t, D_MODEL), F32), jax.ShapeDtypeStruct((t, LANES), F32)],
        compiler_params=_cparams(("parallel",)),
    )(y2, a2, gs, ga, x2, w_glu, w_sp, w_ap, w_o, norm2_g, w_r_pad, b_r_pad)


def _route_body(lg_ref, tri_ref, slab_ref, cnt_ref, base):
    @pl.when(pl.program_id(0) == 0)
    def _():
        base[...] = jnp.zeros_like(base)

    tt = lg_ref.shape[0]
    lane = lax.broadcasted_iota(jnp.int32, (tt, LANES), 1)
    l = jnp.where(lane < N_EXPERTS, lg_ref[...], -jnp.inf)
    tops, hots = [], []
    for _ in range(TOP_K):
        m = jnp.max(l, axis=-1, keepdims=True)
        idx = jnp.min(jnp.where(l == m, lane, LANES), axis=-1, keepdims=True)
        hot = lane == idx
        tops.append((m, idx))
        hots.append(hot)
        l = jnp.where(hot, -jnp.inf, l)
    es = [jnp.exp(m - tops[0][0]) for m, _ in tops]
    den = es[0] + es[1] + es[2] + es[3]
    member = jnp.zeros((tt, LANES), F32)
    for hot in hots:
        member = member + jnp.where(hot, 1.0, 0.0)
    before = jnp.dot(tri_ref[...], member.astype(BF16), preferred_element_type=F32) + base[...]
    slab = jnp.zeros((tt, LANES), F32)
    for k in range(TOP_K):
        pos = jnp.sum(jnp.where(hots[k], before, 0.0), axis=-1, keepdims=True)
        slab = jnp.where(lane == k, tops[k][1].astype(F32), slab)
        slab = jnp.where(lane == TOP_K + k, es[k] / den, slab)
        slab = jnp.where(lane == 2 * TOP_K + k, pos, slab)
    slab_ref[...] = slab
    base[...] = base[...] + jnp.sum(member, axis=0, keepdims=True)
    cnt_ref[...] = base[...]


def _route(logits):
    t = logits.shape[0]
    tt = min(TT_ROUTE, t)
    tri = jnp.tril(jnp.ones((tt, tt), F32), -1).astype(BF16)
    return pl.pallas_call(
        _route_body,
        name="route",
        grid=(t // tt,),
        in_specs=[pl.BlockSpec((tt, LANES), lambda i: (i, 0)),
                  pl.BlockSpec((tt, tt), lambda i: (0, 0))],
        out_specs=[pl.BlockSpec((tt, LANES), lambda i: (i, 0)),
                   pl.BlockSpec((1, LANES), lambda i: (0, 0))],
        out_shape=[jax.ShapeDtypeStruct((t, LANES), F32), jax.ShapeDtypeStruct((1, LANES), F32)],
        scratch_shapes=[pltpu.VMEM((1, LANES), F32)],
        compiler_params=_cparams(("arbitrary",)),
    )(logits, tri)


def _row_copies(n_tok, issue_one):
    def body(t8, carry):
        for i in range(SUBLANES):
            issue_one(t8 * SUBLANES + i)
        return carry
    lax.fori_loop(0, n_tok // SUBLANES, body, 0)


def _dispatch_body(dest_ref, x_ref, xs_in_ref, xs_ref, sem):
    del xs_in_ref
    tt = x_ref.shape[0]

    def issue(t):
        for k in range(TOP_K):
            d = dest_ref[0, 0, t * TOP_K + k]
            pltpu.make_async_copy(x_ref.at[pl.ds(t, 1)], xs_ref.at[pl.ds(d, 1)], sem).start()

    _row_copies(tt, issue)
    for _ in range(TOP_K):
        pltpu.make_async_copy(x_ref, xs_ref.at[pl.ds(0, tt)], sem).wait()


def _dispatch(x1, dest3, n_rows):
    t = x1.shape[0]
    tt = dest3.shape[2] // TOP_K
    xs0 = jnp.zeros((n_rows, D_MODEL), F32)
    return pl.pallas_call(
        _dispatch_body,
        name="dispatch",
        grid=(t // tt,),
        in_specs=[pl.BlockSpec((1, 1, tt * TOP_K), lambda i: (i, 0, 0), memory_space=pltpu.SMEM),
                  pl.BlockSpec((tt, D_MODEL), lambda i: (i, 0)),
                  pl.BlockSpec(memory_space=pl.ANY)],
        out_specs=pl.BlockSpec(memory_space=pl.ANY),
        out_shape=jax.ShapeDtypeStruct((n_rows, D_MODEL), F32),
        scratch_shapes=[pltpu.SemaphoreType.DMA(())],
        input_output_aliases={2: 0},
        compiler_params=_cparams(("arbitrary",)),
    )(dest3, x1, xs0)


def _ffn_body(be_ref, nu_ref, xs_ref, g_ref, wgu_ref, bgu_ref, wd_ref, bd_ref, ys_ref):
    del be_ref

    @pl.when(pl.program_id(0) < nu_ref[0])
    def _():
        h = _rms(xs_ref[...], g_ref[...]).astype(BF16)
        gu = jnp.dot(h, wgu_ref[...], preferred_element_type=F32) + bgu_ref[...]
        gate = jnp.minimum(gu[:, :D_FF], SWIGLU_LIMIT)
        up = jnp.clip(gu[:, D_FF:], -SWIGLU_LIMIT, SWIGLU_LIMIT)
        act = (up + 1.0) * gate * jax.nn.sigmoid(SWIGLU_ALPHA * gate)
        ys_ref[...] = jnp.dot(act.astype(BF16), wd_ref[...], preferred_element_type=F32) + bd_ref[...]

    @pl.when(pl.program_id(0) >= nu_ref[0])
    def _():
        ys_ref[...] = jnp.zeros_like(ys_ref)


def _moe_ffn(block_e, n_used, xs, norm2_g, w_gu, b_gu, w_d, b_d):
    n_rows = xs.shape[0]
    n_blocks = n_rows // MOE_BLOCK
    row_blk = lambda i, be, nu: (jnp.minimum(i, nu[0] - 1), 0)
    by_e = lambda i, be, nu: (be[i], 0, 0)
    grid_spec = pltpu.PrefetchScalarGridSpec(
        num_scalar_prefetch=2,
        grid=(n_blocks,),
        in_specs=[
            pl.BlockSpec((MOE_BLOCK, D_MODEL), row_blk),
            pl.BlockSpec((1, D_MODEL), lambda i, be, nu: (0, 0)),
            pl.BlockSpec((None, D_MODEL, 2 * D_FF), by_e),
            pl.BlockSpec((None, 1, 2 * D_FF), by_e),
            pl.BlockSpec((None, D_FF, D_MODEL), by_e),
            pl.BlockSpec((None, 1, D_MODEL), by_e),
        ],
        out_specs=pl.BlockSpec((MOE_BLOCK, D_MODEL), lambda i, be, nu: (i, 0)),
    )
    return pl.pallas_call(
        _ffn_body,
        name="moe_ffn",
        grid_spec=grid_spec,
        out_shape=jax.ShapeDtypeStruct((n_rows, D_MODEL), F32),
        compiler_params=_cparams(("arbitrary",)),
    )(block_e, n_used, xs, norm2_g, w_gu, b_gu, w_d, b_d)


def _combine_body(dest_ref, x_ref, slab_ref, ys_ref, o_ref, buf, sem):
    tt = x_ref.shape[0]

    def issue(t):
        for k in range(TOP_K):
            d = dest_ref[0, 0, t * TOP_K + k]
            pltpu.make_async_copy(ys_ref.at[pl.ds(d, 1)], buf.at[k, pl.ds(t, 1)], sem).start()

    _row_copies(tt, issue)
    for k in range(TOP_K):
        pltpu.make_async_copy(ys_ref.at[pl.ds(0, tt)], buf.at[k], sem).wait()
    acc = x_ref[...]
    for k in range(TOP_K):
        acc = acc + slab_ref[:, TOP_K + k:TOP_K + k + 1] * buf[k]
    o_ref[...] = acc


def _combine(x1, slab, dest3, ys):
    t = x1.shape[0]
    tt = dest3.shape[2] // TOP_K
    return pl.pallas_call(
        _combine_body,
        name="combine",
        grid=(t // tt,),
        in_specs=[pl.BlockSpec((1, 1, tt * TOP_K), lambda i: (i, 0, 0), memory_space=pltpu.SMEM),
                  pl.BlockSpec((tt, D_MODEL), lambda i: (i, 0)),
                  pl.BlockSpec((tt, LANES), lambda i: (i, 0)),
                  pl.BlockSpec(memory_space=pl.ANY)],
        out_specs=pl.BlockSpec((tt, D_MODEL), lambda i: (i, 0)),
        out_shape=jax.ShapeDtypeStruct((t, D_MODEL), F32),
        scratch_shapes=[pltpu.VMEM((TOP_K, tt, D_MODEL), F32), pltpu.SemaphoreType.DMA(())],
        compiler_params=_cparams(("arbitrary",)),
    )(dest3, x1, slab, ys)


def _moe(x1, logits, norm2_g, w_gate_up, b_gate_up, w_down, b_down):
    t = x1.shape[0]
    n_assign = t * TOP_K
    n_blocks = -(-n_assign // MOE_BLOCK) + N_EXPERTS
    n_rows = n_blocks * MOE_BLOCK

    slab, cnt = _route(logits)
    ids = slab[:, :TOP_K].astype(jnp.int32)
    pos = slab[:, 2 * TOP_K:3 * TOP_K].astype(jnp.int32)
    counts = cnt[0, :N_EXPERTS].astype(jnp.int32)
    padded = (counts + MOE_BLOCK - 1) // MOE_BLOCK * MOE_BLOCK
    pad_ends = jnp.cumsum(padded)
    pad_starts = pad_ends - padded
    dest = pad_starts[ids] + pos
    n_used = (pad_ends[-1] // MOE_BLOCK).astype(jnp.int32)
    blk = jnp.minimum(jnp.arange(n_blocks, dtype=jnp.int32), n_used - 1)
    block_e = jnp.minimum(jnp.searchsorted(pad_ends, blk * MOE_BLOCK, side='right'),
                          N_EXPERTS - 1).astype(jnp.int32)

    tt = min(TT_MOVE, t)
    dest3 = dest.reshape(t // tt, 1, tt * TOP_K)
    xs = _dispatch(x1, dest3, n_rows)
    ys = _moe_ffn(block_e, n_used.reshape(1), xs, norm2_g,
                  w_gate_up.astype(BF16), b_gate_up.reshape(N_EXPERTS, 1, 2 * D_FF),
                  w_down.astype(BF16), b_down.reshape(N_EXPERTS, 1, D_MODEL))
    return _combine(x1, slab, dest3, ys)


def _layer(x, norm1_g, w_in, ssm_a_re, ssm_a_im, ssm_log_dt, ssm_b_re, ssm_b_im, ssm_c_re,
           ssm_c_im, ssm_d, w_glu, w_ssm_proj, q_norm_g, k_norm_g, attn_sinks, w_attn_proj,
           w_out, norm2_g, w_router, b_router, w_gate_up, b_gate_up, w_down, b_down):
    n_b, seq, d = x.shape
    t = n_b * seq
    x2 = x.reshape(t, d)
    u, q, k, v, gs, ga = _in_proj(x2, norm1_g.reshape(1, d), w_in.astype(BF16))

    abar_re, abar_im, bbt_re, bbt_im = _s5_prep(ssm_a_re, ssm_a_im, ssm_log_dt, ssm_b_re, ssm_b_im)
    b_mat, c_mat, ab = _s5_matrices(abar_re, abar_im, bbt_re, bbt_im, ssm_c_re, ssm_c_im)
    y = _s5(u.reshape(n_b, seq, SSM_WIDTH), b_mat, c_mat, ab, ssm_d.reshape(1, SSM_WIDTH))

    a = _attention(q.reshape(n_b, seq, ATTN_WIDTH), k.reshape(n_b, seq, KV_WIDTH),
                   v.reshape(n_b, seq, KV_WIDTH), q_norm_g, k_norm_g, attn_sinks)

    w_r_pad = jnp.zeros((d, LANES), BF16).at[:, :N_EXPERTS].set(w_router.astype(BF16))
    b_r_pad = jnp.zeros((1, LANES), F32).at[0, :N_EXPERTS].set(b_router)
    norm2 = norm2_g.reshape(1, d)
    x1, logits = _mix(y.reshape(t, SSM_WIDTH), a.reshape(t, ATTN_WIDTH), gs, ga, x2,
                      w_glu.astype(BF16), w_ssm_proj.astype(BF16), w_attn_proj.astype(BF16),
                      w_out.astype(BF16), norm2, w_r_pad, b_r_pad)
    out = _moe(x1, logits, norm2, w_gate_up, b_gate_up, w_down, b_down)
    return out.reshape(n_b, seq, d)


def kernel(x, norm1_g, w_in, ssm_a_re, ssm_a_im, ssm_log_dt, ssm_b_re, ssm_b_im, ssm_c_re, ssm_c_im, ssm_d, w_glu, w_ssm_proj, q_norm_g, k_norm_g, attn_sinks, w_attn_proj, w_out, norm2_g, w_router, b_router, w_gate_up, b_gate_up, w_down, b_down):
    for layer in range(norm1_g.shape[0]):
        x = _layer(
            x, norm1_g[layer], w_in[layer], ssm_a_re[layer], ssm_a_im[layer], ssm_log_dt[layer],
            ssm_b_re[layer], ssm_b_im[layer], ssm_c_re[layer], ssm_c_im[layer], ssm_d[layer],
            w_glu[layer], w_ssm_proj[layer], q_norm_g[layer], k_norm_g[layer], attn_sinks[layer],
            w_attn_proj[layer], w_out[layer], norm2_g[layer], w_router[layer], b_router[layer],
            w_gate_up[layer], b_gate_up[layer], w_down[layer], b_down[layer])
    return x
```

```python
import functools

import jax
import jax.numpy as jnp
from jax import lax
from jax.experimental import pallas as pl
from jax.experimental.pallas import tpu as pltpu

D_MODEL = 1024
NORM_EPS = 1e-5
SSM_WIDTH = 1024
SSM_GROUP_CH = 16
SSM_GROUPS = SSM_WIDTH // SSM_GROUP_CH
SSM_STATE = 64
HEAD_DIM = 64
N_Q_HEADS = 16
N_KV_HEADS = 4
Q_PER_KV = N_Q_HEADS // N_KV_HEADS
ATTN_WIDTH = N_Q_HEADS * HEAD_DIM
KV_WIDTH = N_KV_HEADS * HEAD_DIM
WINDOW = 128
BLOCK = 128
N_EXPERTS = 32
TOP_K = 4
D_FF = 1024
SWIGLU_LIMIT = 7.0
SWIGLU_ALPHA = 1.702
MOE_BLOCK = 512

LANES = 128
SUBLANES = 8
VMEM_LIMIT = 56 * 1024 * 1024

S5_GROUP_BLOCK = 8
S5_CH = S5_GROUP_BLOCK * SSM_GROUP_CH
S5_ST = S5_GROUP_BLOCK * SSM_STATE
S5_CHUNK = 256
S5_PITCH = S5_CHUNK + SUBLANES

TM_IN = 256
TM_MIX = 512
TT_ROUTE = 512
TT_MOVE = 256

F32 = jnp.float32
BF16 = jnp.bfloat16


def _cparams(sem):
    return pltpu.CompilerParams(dimension_semantics=sem, vmem_limit_bytes=VMEM_LIMIT)


def _rms(x, g):
    return x * lax.rsqrt(jnp.mean(x * x, axis=-1, keepdims=True) + NORM_EPS) * g


def _in_proj_body(x_ref, g_ref, w_ref, u_ref, q_ref, k_ref, v_ref, gs_ref, ga_ref):
    h = _rms(x_ref[...], g_ref[...]).astype(BF16)
    lo = 0
    for ref in (u_ref, q_ref, k_ref, v_ref, gs_ref, ga_ref):
        hi = lo + ref.shape[-1]
        ref[...] = jnp.dot(h, w_ref[:, lo:hi], preferred_element_type=F32)
        lo = hi


def _in_proj(x2, norm_g, w_in_bf):
    t = x2.shape[0]
    widths = (SSM_WIDTH, ATTN_WIDTH, KV_WIDTH, KV_WIDTH, D_MODEL, D_MODEL)
    in_width = sum(widths)
    return pl.pallas_call(
        _in_proj_body,
        name="in_proj",
        grid=(t // TM_IN,),
        in_specs=[
            pl.BlockSpec((TM_IN, D_MODEL), lambda i: (i, 0)),
            pl.BlockSpec((1, D_MODEL), lambda i: (0, 0)),
            pl.BlockSpec((D_MODEL, in_width), lambda i: (0, 0)),
        ],
        out_specs=[pl.BlockSpec((TM_IN, w), lambda i: (i, 0)) for w in widths],
        out_shape=[jax.ShapeDtypeStruct((t, w), F32) for w in widths],
        compiler_params=_cparams(("parallel",)),
    )(x2, norm_g, w_in_bf)


def _s5_prep_body(are_ref, aim_ref, ldt_ref, bre_ref, bim_ref,
                  abr_ref, abi_ref, bbr_ref, bbi_ref):
    a_re = are_ref[...]
    a_im = aim_ref[...]
    dt = jnp.exp(ldt_ref[...])
    mag = jnp.exp(a_re * dt)
    abar_re = mag * jnp.cos(a_im * dt)
    abar_im = mag * jnp.sin(a_im * dt)
    den = a_re * a_re + a_im * a_im
    q_re = ((abar_re - 1.0) * a_re + abar_im * a_im) / den
    q_im = (abar_im * a_re - (abar_re - 1.0) * a_im) / den
    abr_ref[...] = abar_re
    abi_ref[...] = abar_im
    for h in range(SSM_GROUP_CH):
        bbr_ref[h] = q_re * bre_ref[h] - q_im * bim_ref[h]
        bbi_ref[h] = q_re * bim_ref[h] + q_im * bre_ref[h]


def _s5_prep(a_re, a_im, log_dt, b_re, b_im):
    g, p, hc = SSM_GROUPS, SSM_STATE, SSM_GROUP_CH
    bt_re = jnp.transpose(b_re, (2, 0, 1))
    bt_im = jnp.transpose(b_im, (2, 0, 1))
    return pl.pallas_call(
        _s5_prep_body,
        name="s5_prep",
        out_shape=[
            jax.ShapeDtypeStruct((g, p), F32),
            jax.ShapeDtypeStruct((g, p), F32),
            jax.ShapeDtypeStruct((hc, g, p), F32),
            jax.ShapeDtypeStruct((hc, g, p), F32),
        ],
    )(a_re, a_im, log_dt.reshape(g, 1), bt_re, bt_im)


def _s5_matrices(abar_re, abar_im, bbt_re, bbt_im, c_re, c_im):
    nb, gb, hc, p = SSM_GROUPS // S5_GROUP_BLOCK, S5_GROUP_BLOCK, SSM_GROUP_CH, SSM_STATE
    eye = jnp.eye(gb, dtype=F32)

    def b_block(bbt):
        t = jnp.transpose(bbt, (1, 0, 2)).reshape(nb, gb, hc, p)
        return jnp.einsum('jghp,gk->jghkp', t, eye).reshape(nb, gb * hc, gb * p)

    def c_block(c):
        t = c.reshape(nb, gb, hc, p)
        return jnp.einsum('jghp,gk->jgpkh', t, eye).reshape(nb, gb * p, gb * hc)

    b_mat = jnp.concatenate([b_block(bbt_re), b_block(bbt_im)], axis=2).astype(BF16)
    c_mat = jnp.concatenate([c_block(c_re), c_block(-c_im)], axis=1).astype(BF16)
    ab = jnp.concatenate([abar_re.reshape(nb, 1, gb * p), abar_im.reshape(nb, 1, gb * p)], axis=2)
    ab = jnp.broadcast_to(ab, (nb, SUBLANES, 2 * gb * p))
    return b_mat, c_mat, ab


def _s5_body(u_ref, b_ref, c_ref, ab_ref, d_ref, o_ref, ubuf, utm, sbuf, state, ytm):
    n_b = u_ref.shape[0]
    chunk = u_ref.shape[1]

    @pl.when(pl.program_id(1) == 0)
    def _():
        state[...] = jnp.zeros_like(state)

    for b in range(n_b):
        ubuf[b * S5_PITCH:b * S5_PITCH + chunk, :] = u_ref[b]

    def to_time_major(l8, carry):
        for i in range(SUBLANES):
            l = l8 * SUBLANES + i
            row = pl.multiple_of(l * n_b, SUBLANES)
            utm[pl.ds(row, n_b), :] = ubuf[pl.ds(l, n_b, stride=S5_PITCH), :]
        return carry

    lax.fori_loop(0, chunk // SUBLANES, to_time_major, 0)

    sbuf[...] = jnp.dot(utm[...].astype(BF16), b_ref[...], preferred_element_type=F32)

    a_re = ab_ref[:, :S5_ST]
    a_im = ab_ref[:, S5_ST:]

    def scan(l8, carry):
        s_re, s_im = carry
        for i in range(SUBLANES):
            row = pl.multiple_of((l8 * SUBLANES + i) * n_b, SUBLANES)
            n_re = a_re * s_re - a_im * s_im + sbuf[pl.ds(row, n_b), :S5_ST]
            n_im = a_re * s_im + a_im * s_re + sbuf[pl.ds(row, n_b), S5_ST:]
            sbuf[pl.ds(row, n_b), :S5_ST] = n_re
            sbuf[pl.ds(row, n_b), S5_ST:] = n_im
            s_re, s_im = n_re, n_im
        return s_re, s_im

    s_re, s_im = lax.fori_loop(0, chunk // SUBLANES, scan, (state[:, :S5_ST], state[:, S5_ST:]))
    state[:, :S5_ST] = s_re
    state[:, S5_ST:] = s_im

    y = jnp.dot(sbuf[...].astype(BF16), c_ref[...], preferred_element_type=F32)
    y = y + d_ref[...] * utm[...]
    ytm[...] = jax.nn.gelu(y)
    for b in range(n_b):
        o_ref[b] = ytm[pl.ds(b, chunk, stride=n_b), :].astype(o_ref.dtype)


def _s5(u3, b_mat, c_mat, ab, d_flat):
    n_b, seq, _ = u3.shape
    assert n_b == SUBLANES, "the scan keeps one sequence per sublane"
    chunk = min(S5_CHUNK, seq)
    nb = SSM_GROUPS // S5_GROUP_BLOCK
    return pl.pallas_call(
        _s5_body,
        name="s5_scan",
        grid=(nb, seq // chunk),
        in_specs=[
            pl.BlockSpec((n_b, chunk, S5_CH), lambda j, c: (0, c, j)),
            pl.BlockSpec((None, S5_CH, 2 * S5_ST), lambda j, c: (j, 0, 0)),
            pl.BlockSpec((None, 2 * S5_ST, S5_CH), lambda j, c: (j, 0, 0)),
            pl.BlockSpec((None, SUBLANES, 2 * S5_ST), lambda j, c: (j, 0, 0)),
            pl.BlockSpec((1, S5_CH), lambda j, c: (0, j)),
        ],
        out_specs=pl.BlockSpec((n_b, chunk, S5_CH), lambda j, c: (0, c, j)),
        out_shape=jax.ShapeDtypeStruct((n_b, seq, SSM_WIDTH), BF16),
        scratch_shapes=[
            pltpu.VMEM((n_b * S5_PITCH, S5_CH), F32),
            pltpu.VMEM((chunk * n_b, S5_CH), F32),
            pltpu.VMEM((chunk * n_b, 2 * S5_ST), F32),
            pltpu.VMEM((n_b, 2 * S5_ST), F32),
            pltpu.VMEM((chunk * n_b, S5_CH), F32),
        ],
        compiler_params=_cparams(("parallel", "arbitrary")),
    )(u3, b_mat, c_mat, ab, d_flat)


def _attn_body(sinks_ref, q_ref, kp_ref, kc_ref, vp_ref, vc_ref, bias_ref, qg_ref, kg_ref, o_ref):
    qg = qg_ref[...] * (HEAD_DIM ** -0.5)
    kg = kg_ref[...]
    kw = jnp.concatenate([kp_ref[...], kc_ref[...]], axis=0)
    vw = jnp.concatenate([vp_ref[...], vc_ref[...]], axis=0).astype(BF16)
    outs = []
    for kv in range(N_KV_HEADS):
        k_n = _rms(kw[:, kv * HEAD_DIM:(kv + 1) * HEAD_DIM], kg).astype(BF16)
        qs = []
        for g in range(Q_PER_KV):
            n = kv * Q_PER_KV + g
            qs.append(_rms(q_ref[:, n * HEAD_DIM:(n + 1) * HEAD_DIM], qg).astype(BF16))
        q_st = jnp.concatenate(qs, axis=0)
        s = lax.dot_general(q_st, k_n, (((1,), (1,)), ((), ())), preferred_element_type=F32)
        s = s + bias_ref[kv]
        ps = []
        for g in range(Q_PER_KV):
            sg = s[g * BLOCK:(g + 1) * BLOCK]
            sink = sinks_ref[kv * Q_PER_KV + g]
            m = jnp.maximum(jnp.max(sg, axis=-1, keepdims=True), sink)
            p = jnp.exp(sg - m)
            den = jnp.sum(p, axis=-1, keepdims=True) + jnp.exp(sink - m)
            ps.append((p / den).astype(BF16))
        p_st = jnp.concatenate(ps, axis=0)
        o = jnp.dot(p_st, vw[:, kv * HEAD_DIM:(kv + 1) * HEAD_DIM], preferred_element_type=F32)
        for g in range(Q_PER_KV):
            outs.append(o[g * BLOCK:(g + 1) * BLOCK])
    o_ref[...] = jnp.concatenate(outs, axis=1).astype(o_ref.dtype)


def _attn_bias():
    h = jnp.arange(1, N_Q_HEADS + 1, dtype=F32)
    slopes = jnp.exp2(-8.0 * h / N_Q_HEADS)
    qi = jnp.arange(BLOCK)[:, None]
    kj = jnp.arange(2 * BLOCK)[None, :]
    dist = qi - kj + BLOCK
    band = (dist >= 0) & (dist < WINDOW)
    bias = -slopes[:, None, None] * dist.astype(F32)
    later = jnp.where(band[None], bias, -jnp.inf)
    first = jnp.where((band & (kj >= BLOCK))[None], bias, -jnp.inf)
    both = jnp.stack([first, later])
    return both.reshape(2, N_KV_HEADS, Q_PER_KV * BLOCK, 2 * BLOCK)


def _attention(q3, k3, v3, q_norm_g, k_norm_g, sinks):
    n_b, seq, _ = q3.shape
    n_blk = seq // BLOCK
    prev = lambda b, i: (b, jnp.maximum(i - 1, 0), 0)
    cur = lambda b, i: (b, i, 0)
    return pl.pallas_call(
        _attn_body,
        name="swa_attention",
        grid=(n_b, n_blk),
        in_specs=[
            pl.BlockSpec(memory_space=pltpu.SMEM),
            pl.BlockSpec((None, BLOCK, ATTN_WIDTH), cur),
            pl.BlockSpec((None, BLOCK, KV_WIDTH), prev),
            pl.BlockSpec((None, BLOCK, KV_WIDTH), cur),
            pl.BlockSpec((None, BLOCK, KV_WIDTH), prev),
            pl.BlockSpec((None, BLOCK, KV_WIDTH), cur),
            pl.BlockSpec((None, N_KV_HEADS, Q_PER_KV * BLOCK, 2 * BLOCK),
                         lambda b, i: (jnp.minimum(i, 1), 0, 0, 0)),
            pl.BlockSpec((1, HEAD_DIM), lambda b, i: (0, 0)),
            pl.BlockSpec((1, HEAD_DIM), lambda b, i: (0, 0)),
        ],
        out_specs=pl.BlockSpec((None, BLOCK, ATTN_WIDTH), cur),
        out_shape=jax.ShapeDtypeStruct((n_b, seq, ATTN_WIDTH), BF16),
        compiler_params=_cparams(("parallel", "arbitrary")),
    )(sinks, q3, k3, k3, v3, v3, _attn_bias(), q_norm_g.reshape(1, HEAD_DIM),
      k_norm_g.reshape(1, HEAD_DIM))


def _mix_body(y_ref, a_ref, gs_ref, ga_ref, x_ref, wglu_ref, wsp_ref, wap_ref, wo_ref,
              g2_ref, wr_ref, br_ref, x1_ref, lg_ref):
    z = jnp.dot(y_ref[...], wglu_ref[...], preferred_element_type=F32)
    glu = z[:, :SSM_WIDTH] * jax.nn.sigmoid(z[:, SSM_WIDTH:])
    y_ssm = jnp.dot(glu.astype(BF16), wsp_ref[...], preferred_element_type=F32)
    y_attn = jnp.dot(a_ref[...], wap_ref[...], preferred_element_type=F32)
    mixed = jax.nn.sigmoid(gs_ref[...]) * y_ssm + jax.nn.sigmoid(ga_ref[...]) * y_attn
    x1 = x_ref[...] + jnp.dot(mixed.astype(BF16), wo_ref[...], preferred_element_type=F32)
    x1_ref[...] = x1
    h2 = _rms(x1, g2_ref[...]).astype(BF16)
    lg_ref[...] = jnp.dot(h2, wr_ref[...], preferred_element_type=F32) + br_ref[...]


def _mix(y2, a2, gs, ga, x2, w_glu, w_sp, w_ap, w_o, norm2_g, w_r_pad, b_r_pad):
    t = x2.shape[0]
    tm = min(TM_MIX, t)
    tok = lambda w: pl.BlockSpec((tm, w), lambda i: (i, 0))
    full = lambda a: pl.BlockSpec(a.shape, lambda i: (0, 0))
    return pl.pallas_call(
        _mix_body,
        name="mix",
        grid=(t // tm,),
        in_specs=[tok(SSM_WIDTH), tok(ATTN_WIDTH), tok(D_MODEL), tok(D_MODEL), tok(D_MODEL),
                  full(w_glu), full(w_sp), full(w_ap), full(w_o), full(norm2_g),
                  full(w_r_pad), full(b_r_pad)],
        out_specs=[tok(D_MODEL), tok(LANES)],
        out_shape=[jax.ShapeDtypeStruct((t, D_MODEL), F32), jax.ShapeDtypeStruct((t, LANES), F32)],
        compiler_params=_cparams(("parallel",)),
    )(y2, a2, gs, ga, x2, w_glu, w_sp, w_ap, w_o, norm2_g, w_r_pad, b_r_pad)


def _route_body(lg_ref, tri_ref, slab_ref, cnt_ref, base):
    @pl.when(pl.program_id(0) == 0)
    def _():
        base[...] = jnp.zeros_like(base)

    tt = lg_ref.shape[0]
    lane = lax.broadcasted_iota(jnp.int32, (tt, LANES), 1)
    l = jnp.where(lane < N_EXPERTS, lg_ref[...], -jnp.inf)
    tops, hots = [], []
    for _ in range(TOP_K):
        m = jnp.max(l, axis=-1, keepdims=True)
        idx = jnp.min(jnp.where(l == m, lane, LANES), axis=-1, keepdims=True)
        hot = lane == idx
        tops.append((m, idx))
        hots.append(hot)
        l = jnp.where(hot, -jnp.inf, l)
    es = [jnp.exp(m - tops[0][0]) for m, _ in tops]
    den = es[0] + es[1] + es[2] + es[3]
    member = jnp.zeros((tt, LANES), F32)
    for hot in hots:
        member = member + jnp.where(hot, 1.0, 0.0)
    before = jnp.dot(tri_ref[...], member.astype(BF16), preferred_element_type=F32) + base[...]
    slab = jnp.zeros((tt, LANES), F32)
    for k in range(TOP_K):
        pos = jnp.sum(jnp.where(hots[k], before, 0.0), axis=-1, keepdims=True)
        slab = jnp.where(lane == k, tops[k][1].astype(F32), slab)
        slab = jnp.where(lane == TOP_K + k, es[k] / den, slab)
        slab = jnp.where(lane == 2 * TOP_K + k, pos, slab)
    slab_ref[...] = slab
    base[...] = base[...] + jnp.sum(member, axis=0, keepdims=True)
    cnt_ref[...] = base[...]


def _route(logits):
    t = logits.shape[0]
    tt = min(TT_ROUTE, t)
    tri = jnp.tril(jnp.ones((tt, tt), F32), -1).astype(BF16)
    return pl.pallas_call(
        _route_body,
        name="route",
        grid=(t // tt,),
        in_specs=[pl.BlockSpec((tt, LANES), lambda i: (i, 0)),
                  pl.BlockSpec((tt, tt), lambda i: (0, 0))],
        out_specs=[pl.BlockSpec((tt, LANES), lambda i: (i, 0)),
                   pl.BlockSpec((1, LANES), lambda i: (0, 0))],
        out_shape=[jax.ShapeDtypeStruct((t, LANES), F32), jax.ShapeDtypeStruct((1, LANES), F32)],
        scratch_shapes=[pltpu.VMEM((1, LANES), F32)],
        compiler_params=_cparams(("arbitrary",)),
    )(logits, tri)


def _row_copies(n_tok, issue_one):
    def body(t8, carry):
        for i in range(SUBLANES):
            issue_one(t8 * SUBLANES + i)
        return carry
    lax.fori_loop(0, n_tok // SUBLANES, body, 0)


def _dispatch_body(pe_ref, nu_ref, dest_ref, x_ref, xs_ref, zbuf, sem, zsem):
    tt = x_ref.shape[0]
    n_blocks = xs_ref.shape[0] // MOE_BLOCK

    @pl.when(pl.program_id(0) == 0)
    def _():
        zbuf[...] = jnp.zeros_like(zbuf)

        def zero_block(row):
            return pltpu.make_async_copy(
                zbuf, xs_ref.at[pl.ds(pl.multiple_of(row, MOE_BLOCK), MOE_BLOCK)], zsem)

        def each(act):
            for e in range(N_EXPERTS):
                @pl.when(pe_ref[e] >= MOE_BLOCK)
                def _():
                    act(zero_block(pe_ref[e] - MOE_BLOCK))

            def tail(i, carry):
                act(zero_block(i * MOE_BLOCK))
                return carry
            lax.fori_loop(nu_ref[0], n_blocks, tail, 0)

        each(lambda c: c.start())
        each(lambda c: c.wait())

    def issue(t):
        for k in range(TOP_K):
            d = dest_ref[0, 0, t * TOP_K + k]
            pltpu.make_async_copy(x_ref.at[pl.ds(t, 1)], xs_ref.at[pl.ds(d, 1)], sem).start(priority=k % 2)

    _row_copies(tt, issue)
    for _ in range(TOP_K):
        pltpu.make_async_copy(x_ref, xs_ref.at[pl.ds(0, tt)], sem).wait()


def _dispatch(pad_ends, n_used, x1, dest3, n_rows):
    t = x1.shape[0]
    tt = dest3.shape[2] // TOP_K
    grid_spec = pltpu.PrefetchScalarGridSpec(
        num_scalar_prefetch=2,
        grid=(t // tt,),
        in_specs=[pl.BlockSpec((1, 1, tt * TOP_K), lambda i, pe, nu: (i, 0, 0), memory_space=pltpu.SMEM),
                  pl.BlockSpec((tt, D_MODEL), lambda i, pe, nu: (i, 0))],
        out_specs=pl.BlockSpec(memory_space=pl.ANY),
        scratch_shapes=[pltpu.VMEM((MOE_BLOCK, D_MODEL), F32),
                        pltpu.SemaphoreType.DMA(()), pltpu.SemaphoreType.DMA(())],
    )
    return pl.pallas_call(
        _dispatch_body,
        name="dispatch",
        grid_spec=grid_spec,
        out_shape=jax.ShapeDtypeStruct((n_rows, D_MODEL), F32),
        compiler_params=_cparams(("arbitrary",)),
    )(pad_ends, n_used, dest3, x1)


def _ffn_body(be_ref, nu_ref, xs_ref, g_ref, wgu_ref, bgu_ref, wd_ref, bd_ref, ys_ref, wgu_bf, wd_bf):
    i = pl.program_id(0)

    @pl.when((i == 0) | (be_ref[i] != be_ref[jnp.maximum(i - 1, 0)]))
    def _():
        wgu_bf[...] = wgu_ref[...].astype(BF16)
        wd_bf[...] = wd_ref[...].astype(BF16)

    @pl.when(i < nu_ref[0])
    def _():
        h = _rms(xs_ref[...], g_ref[...]).astype(BF16)
        gu = jnp.dot(h, wgu_bf[...], preferred_element_type=F32) + bgu_ref[...]
        gate = jnp.minimum(gu[:, :D_FF], SWIGLU_LIMIT)
        up = jnp.clip(gu[:, D_FF:], -SWIGLU_LIMIT, SWIGLU_LIMIT)
        act = (up + 1.0) * gate * jax.nn.sigmoid(SWIGLU_ALPHA * gate)
        ys_ref[...] = jnp.dot(act.astype(BF16), wd_bf[...], preferred_element_type=F32) + bd_ref[...]

    @pl.when(pl.program_id(0) >= nu_ref[0])
    def _():
        ys_ref[...] = jnp.zeros_like(ys_ref)


def _moe_ffn(block_e, n_used, xs, norm2_g, w_gu, b_gu, w_d, b_d):
    n_rows = xs.shape[0]
    n_blocks = n_rows // MOE_BLOCK
    row_blk = lambda i, be, nu: (jnp.minimum(i, nu[0] - 1), 0)
    by_e = lambda i, be, nu: (be[i], 0, 0)
    grid_spec = pltpu.PrefetchScalarGridSpec(
        num_scalar_prefetch=2,
        grid=(n_blocks,),
        in_specs=[
            pl.BlockSpec((MOE_BLOCK, D_MODEL), row_blk),
            pl.BlockSpec((1, D_MODEL), lambda i, be, nu: (0, 0)),
            pl.BlockSpec((None, D_MODEL, 2 * D_FF), by_e),
            pl.BlockSpec((None, 1, 2 * D_FF), by_e),
            pl.BlockSpec((None, D_FF, D_MODEL), by_e),
            pl.BlockSpec((None, 1, D_MODEL), by_e),
        ],
        out_specs=pl.BlockSpec((MOE_BLOCK, D_MODEL), lambda i, be, nu: (i, 0)),
        scratch_shapes=[pltpu.VMEM((D_MODEL, 2 * D_FF), BF16), pltpu.VMEM((D_FF, D_MODEL), BF16)],
    )
    return pl.pallas_call(
        _ffn_body,
        name="moe_ffn",
        grid_spec=grid_spec,
        out_shape=jax.ShapeDtypeStruct((n_rows, D_MODEL), F32),
        compiler_params=_cparams(("arbitrary",)),
    )(block_e, n_used, xs, norm2_g, w_gu, b_gu, w_d, b_d)


def _combine_body(dest_ref, x_ref, slab_ref, ys_ref, o_ref, buf, sem):
    tt = x_ref.shape[0]

    def issue(t):
        for k in range(TOP_K):
            d = dest_ref[0, 0, t * TOP_K + k]
            pltpu.make_async_copy(ys_ref.at[pl.ds(d, 1)], buf.at[k, pl.ds(t, 1)], sem).start(priority=k % 2)

    _row_copies(tt, issue)
    for k in range(TOP_K):
        pltpu.make_async_copy(ys_ref.at[pl.ds(0, tt)], buf.at[k], sem).wait()
    acc = x_ref[...]
    for k in range(TOP_K):
        acc = acc + slab_ref[:, TOP_K + k:TOP_K + k + 1] * buf[k]
    o_ref[...] = acc


def _combine(x1, slab, dest3, ys):
    t = x1.shape[0]
    tt = dest3.shape[2] // TOP_K
    return pl.pallas_call(
        _combine_body,
        name="combine",
        grid=(t // tt,),
        in_specs=[pl.BlockSpec((1, 1, tt * TOP_K), lambda i: (i, 0, 0), memory_space=pltpu.SMEM),
                  pl.BlockSpec((tt, D_MODEL), lambda i: (i, 0)),
                  pl.BlockSpec((tt, LANES), lambda i: (i, 0)),
                  pl.BlockSpec(memory_space=pl.ANY)],
        out_specs=pl.BlockSpec((tt, D_MODEL), lambda i: (i, 0)),
        out_shape=jax.ShapeDtypeStruct((t, D_MODEL), F32),
        scratch_shapes=[pltpu.VMEM((TOP_K, tt, D_MODEL), F32), pltpu.SemaphoreType.DMA(())],
        compiler_params=_cparams(("arbitrary",)),
    )(dest3, x1, slab, ys)


def _moe(x1, logits, norm2_g, w_gate_up, b_gate_up, w_down, b_down):
    t = x1.shape[0]
    n_assign = t * TOP_K
    n_blocks = -(-n_assign // MOE_BLOCK) + N_EXPERTS
    n_rows = n_blocks * MOE_BLOCK

    slab, cnt = _route(logits)
    ids = slab[:, :TOP_K].astype(jnp.int32)
    pos = slab[:, 2 * TOP_K:3 * TOP_K].astype(jnp.int32)
    counts = cnt[0, :N_EXPERTS].astype(jnp.int32)
    padded = (counts + MOE_BLOCK - 1) // MOE_BLOCK * MOE_BLOCK
    pad_ends = jnp.cumsum(padded)
    pad_starts = pad_ends - padded
    dest = pad_starts[ids] + pos
    n_used = (pad_ends[-1] // MOE_BLOCK).astype(jnp.int32)
    blk = jnp.minimum(jnp.arange(n_blocks, dtype=jnp.int32), n_used - 1)
    block_e = jnp.minimum(jnp.sum((pad_ends[None, :] <= (blk * MOE_BLOCK)[:, None]).astype(jnp.int32), axis=1),
                          N_EXPERTS - 1)

    tt = min(TT_MOVE, t)
    dest3 = dest.reshape(t // tt, 1, tt * TOP_K)
    xs = _dispatch(pad_ends.astype(jnp.int32), n_used.reshape(1), x1, dest3, n_rows)
    ys = _moe_ffn(block_e, n_used.reshape(1), xs, norm2_g,
                  w_gate_up, b_gate_up.reshape(N_EXPERTS, 1, 2 * D_FF),
                  w_down, b_down.reshape(N_EXPERTS, 1, D_MODEL))
    return _combine(x1, slab, dest3, ys)


def _layer(x, norm1_g, w_in, ssm_a_re, ssm_a_im, ssm_log_dt, ssm_b_re, ssm_b_im, ssm_c_re,
           ssm_c_im, ssm_d, w_glu, w_ssm_proj, q_norm_g, k_norm_g, attn_sinks, w_attn_proj,
           w_out, norm2_g, w_router, b_router, w_gate_up, b_gate_up, w_down, b_down):
    n_b, seq, d = x.shape
    t = n_b * seq
    x2 = x.reshape(t, d)
    u, q, k, v, gs, ga = _in_proj(x2, norm1_g.reshape(1, d), w_in.astype(BF16))

    abar_re, abar_im, bbt_re, bbt_im = _s5_prep(ssm_a_re, ssm_a_im, ssm_log_dt, ssm_b_re, ssm_b_im)
    b_mat, c_mat, ab = _s5_matrices(abar_re, abar_im, bbt_re, bbt_im, ssm_c_re, ssm_c_im)
    y = _s5(u.reshape(n_b, seq, SSM_WIDTH), b_mat, c_mat, ab, ssm_d.reshape(1, SSM_WIDTH))

    a = _attention(q.reshape(n_b, seq, ATTN_WIDTH), k.reshape(n_b, seq, KV_WIDTH),
                   v.reshape(n_b, seq, KV_WIDTH), q_norm_g, k_norm_g, attn_sinks)

    w_r_pad = jnp.zeros((d, LANES), BF16).at[:, :N_EXPERTS].set(w_router.astype(BF16))
    b_r_pad = jnp.zeros((1, LANES), F32).at[0, :N_EXPERTS].set(b_router)
    norm2 = norm2_g.reshape(1, d)
    x1, logits = _mix(y.reshape(t, SSM_WIDTH), a.reshape(t, ATTN_WIDTH), gs, ga, x2,
                      w_glu.astype(BF16), w_ssm_proj.astype(BF16), w_attn_proj.astype(BF16),
                      w_out.astype(BF16), norm2, w_r_pad, b_r_pad)
    out = _moe(x1, logits, norm2, w_gate_up, b_gate_up, w_down, b_down)
    return out.reshape(n_b, seq, d)


def kernel(x, norm1_g, w_in, ssm_a_re, ssm_a_im, ssm_log_dt, ssm_b_re, ssm_b_im, ssm_c_re, ssm_c_im, ssm_d, w_glu, w_ssm_proj, q_norm_g, k_norm_g, attn_sinks, w_attn_proj, w_out, norm2_g, w_router, b_router, w_gate_up, b_gate_up, w_down, b_down):
    for layer in range(norm1_g.shape[0]):
        x = _layer(
            x, norm1_g[layer], w_in[layer], ssm_a_re[layer], ssm_a_im[layer], ssm_log_dt[layer],
            ssm_b_re[layer], ssm_b_im[layer], ssm_c_re[layer], ssm_c_im[layer], ssm_d[layer],
            w_glu[layer], w_ssm_proj[layer], q_norm_g[layer], k_norm_g[layer], attn_sinks[layer],
            w_attn_proj[layer], w_out[layer], norm2_g[layer], w_router[layer], b_router[layer],
            w_gate_up[layer], b_gate_up[layer], w_down[layer], b_down[layer])
    return x
```

```python
import functools

import jax
import jax.numpy as jnp
from jax import lax
from jax.experimental import pallas as pl
from jax.experimental.pallas import tpu as pltpu

D_MODEL = 1024
NORM_EPS = 1e-5
SSM_WIDTH = 1024
SSM_GROUP_CH = 16
SSM_GROUPS = SSM_WIDTH // SSM_GROUP_CH
SSM_STATE = 64
HEAD_DIM = 64
N_Q_HEADS = 16
N_KV_HEADS = 4
Q_PER_KV = N_Q_HEADS // N_KV_HEADS
ATTN_WIDTH = N_Q_HEADS * HEAD_DIM
KV_WIDTH = N_KV_HEADS * HEAD_DIM
WINDOW = 128
BLOCK = 128
N_EXPERTS = 32
TOP_K = 4
D_FF = 1024
SWIGLU_LIMIT = 7.0
SWIGLU_ALPHA = 1.702
MOE_BLOCK = 512

LANES = 128
SUBLANES = 8
VMEM_LIMIT = 56 * 1024 * 1024

S5_GROUP_BLOCK = 8
S5_CH = S5_GROUP_BLOCK * SSM_GROUP_CH
S5_ST = S5_GROUP_BLOCK * SSM_STATE
S5_CHUNK = 256
S5_PITCH = S5_CHUNK + SUBLANES

TM_IN = 256
TM_MIX = 512
TT_ROUTE = 512
TT_MOVE = 256

F32 = jnp.float32
BF16 = jnp.bfloat16


def _cparams(sem):
    return pltpu.CompilerParams(dimension_semantics=sem, vmem_limit_bytes=VMEM_LIMIT)


def _rms(x, g):
    return x * lax.rsqrt(jnp.mean(x * x, axis=-1, keepdims=True) + NORM_EPS) * g


KV_DUP = N_KV_HEADS * LANES


def _in_proj_body(x_ref, g_ref, w_ref, qg_ref, kg_ref, u_ref, q_ref, kd_ref, vd_ref, gs_ref, ga_ref):
    h = _rms(x_ref[...], g_ref[...]).astype(BF16)
    off = [0]

    def proj(width):
        lo = off[0]
        off[0] = lo + width
        return jnp.dot(h, w_ref[:, lo:lo + width], preferred_element_type=F32)

    u_ref[...] = proj(SSM_WIDTH)

    q = proj(ATTN_WIDTH)
    lo_sel = lax.broadcasted_iota(jnp.int32, (q.shape[0], LANES), 1) < HEAD_DIM
    for j in range(ATTN_WIDTH // LANES):
        qq = q[:, j * LANES:(j + 1) * LANES]
        sq = qq * qq
        ms_lo = jnp.sum(jnp.where(lo_sel, sq, 0.0), axis=-1, keepdims=True) / HEAD_DIM
        ms_hi = jnp.sum(jnp.where(lo_sel, 0.0, sq), axis=-1, keepdims=True) / HEAD_DIM
        r = jnp.where(lo_sel, lax.rsqrt(ms_lo + NORM_EPS), lax.rsqrt(ms_hi + NORM_EPS))
        q_ref[:, j * LANES:(j + 1) * LANES] = (qq * r * qg_ref[...]).astype(q_ref.dtype)

    kd = proj(KV_DUP)
    for kv in range(N_KV_HEADS):
        grp = slice(kv * LANES, (kv + 1) * LANES)
        kd_ref[:, grp] = _rms(kd[:, grp], kg_ref[...]).astype(kd_ref.dtype)
    vd_ref[...] = proj(KV_DUP).astype(vd_ref.dtype)
    gs_ref[...] = proj(D_MODEL)
    ga_ref[...] = proj(D_MODEL)


def _in_proj_weight(w_in):
    d = w_in.shape[0]
    cuts = (SSM_WIDTH, SSM_WIDTH + ATTN_WIDTH, SSM_WIDTH + ATTN_WIDTH + KV_WIDTH,
            SSM_WIDTH + ATTN_WIDTH + 2 * KV_WIDTH)
    w_uq, w_k, w_v, w_g = (w_in[:, :cuts[1]], w_in[:, cuts[1]:cuts[2]], w_in[:, cuts[2]:cuts[3]],
                           w_in[:, cuts[3]:])

    def twice(w):
        w = w.reshape(d, N_KV_HEADS, 1, HEAD_DIM)
        return jnp.concatenate([w, w], axis=2).reshape(d, KV_DUP)

    return jnp.concatenate([w_uq, twice(w_k), twice(w_v), w_g], axis=1).astype(BF16)


def _in_proj(x2, norm_g, w_in, q_norm_g, k_norm_g):
    t = x2.shape[0]
    w = _in_proj_weight(w_in)
    qg = (jnp.tile(q_norm_g, LANES // HEAD_DIM) * HEAD_DIM ** -0.5).reshape(1, LANES)
    kg = jnp.tile(k_norm_g, LANES // HEAD_DIM).reshape(1, LANES)
    outs = ((SSM_WIDTH, F32), (ATTN_WIDTH, BF16), (KV_DUP, BF16), (KV_DUP, BF16),
            (D_MODEL, F32), (D_MODEL, F32))
    const = lambda a: pl.BlockSpec(a.shape, lambda i: (0, 0))
    return pl.pallas_call(
        _in_proj_body,
        name="in_proj",
        grid=(t // TM_IN,),
        in_specs=[pl.BlockSpec((TM_IN, D_MODEL), lambda i: (i, 0)), const(norm_g), const(w),
                  const(qg), const(kg)],
        out_specs=[pl.BlockSpec((TM_IN, n), lambda i: (i, 0)) for n, _ in outs],
        out_shape=[jax.ShapeDtypeStruct((t, n), dt) for n, dt in outs],
        compiler_params=_cparams(("parallel",)),
    )(x2, norm_g, w, qg, kg)


def _s5_prep_body(are_ref, aim_ref, ldt_ref, bre_ref, bim_ref,
                  abr_ref, abi_ref, bbr_ref, bbi_ref):
    a_re = are_ref[...]
    a_im = aim_ref[...]
    dt = jnp.exp(ldt_ref[...])
    mag = jnp.exp(a_re * dt)
    abar_re = mag * jnp.cos(a_im * dt)
    abar_im = mag * jnp.sin(a_im * dt)
    den = a_re * a_re + a_im * a_im
    q_re = ((abar_re - 1.0) * a_re + abar_im * a_im) / den
    q_im = (abar_im * a_re - (abar_re - 1.0) * a_im) / den
    abr_ref[...] = abar_re
    abi_ref[...] = abar_im
    for h in range(SSM_GROUP_CH):
        bbr_ref[h] = q_re * bre_ref[h] - q_im * bim_ref[h]
        bbi_ref[h] = q_re * bim_ref[h] + q_im * bre_ref[h]


def _s5_prep(a_re, a_im, log_dt, b_re, b_im):
    g, p, hc = SSM_GROUPS, SSM_STATE, SSM_GROUP_CH
    bt_re = jnp.transpose(b_re, (2, 0, 1))
    bt_im = jnp.transpose(b_im, (2, 0, 1))
    return pl.pallas_call(
        _s5_prep_body,
        name="s5_prep",
        out_shape=[
            jax.ShapeDtypeStruct((g, p), F32),
            jax.ShapeDtypeStruct((g, p), F32),
            jax.ShapeDtypeStruct((hc, g, p), F32),
            jax.ShapeDtypeStruct((hc, g, p), F32),
        ],
    )(a_re, a_im, log_dt.reshape(g, 1), bt_re, bt_im)


def _s5_matrices(abar_re, abar_im, bbt_re, bbt_im, c_re, c_im):
    nb, gb, hc, p = SSM_GROUPS // S5_GROUP_BLOCK, S5_GROUP_BLOCK, SSM_GROUP_CH, SSM_STATE
    eye = jnp.eye(gb, dtype=F32)

    def b_block(bbt):
        t = jnp.transpose(bbt, (1, 0, 2)).reshape(nb, gb, hc, p)
        return jnp.einsum('jghp,gk->jghkp', t, eye).reshape(nb, gb * hc, gb * p)

    def c_block(c):
        t = c.reshape(nb, gb, hc, p)
        return jnp.einsum('jghp,gk->jgpkh', t, eye).reshape(nb, gb * p, gb * hc)

    b_mat = jnp.concatenate([b_block(bbt_re), b_block(bbt_im)], axis=2).astype(BF16)
    c_mat = jnp.concatenate([c_block(c_re), c_block(-c_im)], axis=1).astype(BF16)
    ab = jnp.concatenate([abar_re.reshape(nb, 1, gb * p), abar_im.reshape(nb, 1, gb * p)], axis=2)
    ab = jnp.broadcast_to(ab, (nb, SUBLANES, 2 * gb * p))
    return b_mat, c_mat, ab


def _s5_body(u_ref, b_ref, c_ref, ab_ref, d_ref, o_ref, ubuf, utm, sbuf, state, ytm):
    n_b = u_ref.shape[0]
    chunk = u_ref.shape[1]

    @pl.when(pl.program_id(1) == 0)
    def _():
        state[...] = jnp.zeros_like(state)

    for b in range(n_b):
        ubuf[b * S5_PITCH:b * S5_PITCH + chunk, :] = u_ref[b]

    def to_time_major(l8, carry):
        for i in range(SUBLANES):
            l = l8 * SUBLANES + i
            row = pl.multiple_of(l * n_b, SUBLANES)
            utm[pl.ds(row, n_b), :] = ubuf[pl.ds(l, n_b, stride=S5_PITCH), :]
        return carry

    lax.fori_loop(0, chunk // SUBLANES, to_time_major, 0)

    sbuf[...] = jnp.dot(utm[...].astype(BF16), b_ref[...], preferred_element_type=F32)

    a_re = ab_ref[:, :S5_ST]
    a_im = ab_ref[:, S5_ST:]

    def scan(l8, carry):
        s_re, s_im = carry
        for i in range(SUBLANES):
            row = pl.multiple_of((l8 * SUBLANES + i) * n_b, SUBLANES)
            n_re = a_re * s_re - a_im * s_im + sbuf[pl.ds(row, n_b), :S5_ST]
            n_im = a_re * s_im + a_im * s_re + sbuf[pl.ds(row, n_b), S5_ST:]
            sbuf[pl.ds(row, n_b), :S5_ST] = n_re
            sbuf[pl.ds(row, n_b), S5_ST:] = n_im
            s_re, s_im = n_re, n_im
        return s_re, s_im

    s_re, s_im = lax.fori_loop(0, chunk // SUBLANES, scan, (state[:, :S5_ST], state[:, S5_ST:]))
    state[:, :S5_ST] = s_re
    state[:, S5_ST:] = s_im

    y = jnp.dot(sbuf[...].astype(BF16), c_ref[...], preferred_element_type=F32)
    y = y + d_ref[...] * utm[...]
    ytm[...] = jax.nn.gelu(y)
    for b in range(n_b):
        o_ref[b] = ytm[pl.ds(b, chunk, stride=n_b), :].astype(o_ref.dtype)


def _s5(u3, b_mat, c_mat, ab, d_flat):
    n_b, seq, _ = u3.shape
    assert n_b == SUBLANES, "the scan keeps one sequence per sublane"
    chunk = min(S5_CHUNK, seq)
    nb = SSM_GROUPS // S5_GROUP_BLOCK
    return pl.pallas_call(
        _s5_body,
        name="s5_scan",
        grid=(nb, seq // chunk),
        in_specs=[
            pl.BlockSpec((n_b, chunk, S5_CH), lambda j, c: (0, c, j)),
            pl.BlockSpec((None, S5_CH, 2 * S5_ST), lambda j, c: (j, 0, 0)),
            pl.BlockSpec((None, 2 * S5_ST, S5_CH), lambda j, c: (j, 0, 0)),
            pl.BlockSpec((None, SUBLANES, 2 * S5_ST), lambda j, c: (j, 0, 0)),
            pl.BlockSpec((1, S5_CH), lambda j, c: (0, j)),
        ],
        out_specs=pl.BlockSpec((n_b, chunk, S5_CH), lambda j, c: (0, c, j)),
        out_shape=jax.ShapeDtypeStruct((n_b, seq, SSM_WIDTH), BF16),
        scratch_shapes=[
            pltpu.VMEM((n_b * S5_PITCH, S5_CH), F32),
            pltpu.VMEM((chunk * n_b, S5_CH), F32),
            pltpu.VMEM((chunk * n_b, 2 * S5_ST), F32),
            pltpu.VMEM((n_b, 2 * S5_ST), F32),
            pltpu.VMEM((chunk * n_b, S5_CH), F32),
        ],
        compiler_params=_cparams(("parallel", "arbitrary")),
    )(u3, b_mat, c_mat, ab, d_flat)


def _attn_body(sinks_ref, q_ref, kp_ref, kc_ref, vp_ref, vc_ref, bias_ref, half_ref, o_ref):
    lo_mask = half_ref[0:1, :]
    hi_mask = half_ref[1:2, :]
    lo_sel = lax.broadcasted_iota(jnp.int32, (BLOCK, LANES), 1) < HEAD_DIM
    for kv in range(N_KV_HEADS):
        grp = slice(kv * LANES, (kv + 1) * LANES)
        kw = jnp.concatenate([kp_ref[:, grp], kc_ref[:, grp]], axis=0)
        vw = jnp.concatenate([vp_ref[:, grp], vc_ref[:, grp]], axis=0)
        qs = []
        for pair in range(Q_PER_KV // 2):
            lanes = slice((2 * kv + pair) * LANES, (2 * kv + pair + 1) * LANES)
            qp = q_ref[:, lanes]
            qs += [qp * lo_mask, qp * hi_mask]
        q_st = jnp.concatenate(qs, axis=0)
        s = lax.dot_general(q_st, kw, (((1,), (1,)), ((), ())), preferred_element_type=F32)
        s = s + bias_ref[kv]
        ps = []
        for g in range(Q_PER_KV):
            sg = s[g * BLOCK:(g + 1) * BLOCK]
            sink = sinks_ref[kv * Q_PER_KV + g]
            m = jnp.maximum(jnp.max(sg, axis=-1, keepdims=True), sink)
            p = jnp.exp(sg - m)
            den = jnp.sum(p, axis=-1, keepdims=True) + jnp.exp(sink - m)
            ps.append((p / den).astype(BF16))
        p_st = jnp.concatenate(ps, axis=0)
        o = jnp.dot(p_st, vw, preferred_element_type=F32)
        for pair in range(Q_PER_KV // 2):
            even = o[(2 * pair) * BLOCK:(2 * pair + 1) * BLOCK]
            odd = o[(2 * pair + 1) * BLOCK:(2 * pair + 2) * BLOCK]
            lanes = slice((2 * kv + pair) * LANES, (2 * kv + pair + 1) * LANES)
            o_ref[:, lanes] = jnp.where(lo_sel, even, odd).astype(o_ref.dtype)


def _attn_bias():
    h = jnp.arange(1, N_Q_HEADS + 1, dtype=F32)
    slopes = jnp.exp2(-8.0 * h / N_Q_HEADS)
    qi = jnp.arange(BLOCK)[:, None]
    kj = jnp.arange(2 * BLOCK)[None, :]
    dist = qi - kj + BLOCK
    band = (dist >= 0) & (dist < WINDOW)
    bias = -slopes[:, None, None] * dist.astype(F32)
    later = jnp.where(band[None], bias, -jnp.inf)
    first = jnp.where((band & (kj >= BLOCK))[None], bias, -jnp.inf)
    both = jnp.stack([first, later])
    return both.reshape(2, N_KV_HEADS, Q_PER_KV * BLOCK, 2 * BLOCK)


def _attention(q3, kd3, vd3, sinks):
    n_b, seq, _ = q3.shape
    n_blk = seq // BLOCK
    kv_dup = N_KV_HEADS * LANES
    prev = lambda b, i: (b, jnp.maximum(i - 1, 0), 0)
    cur = lambda b, i: (b, i, 0)
    lane = jnp.arange(LANES)
    half = jnp.stack([lane < HEAD_DIM, lane >= HEAD_DIM]).astype(BF16)
    return pl.pallas_call(
        _attn_body,
        name="swa_attention",
        grid=(n_b, n_blk),
        in_specs=[
            pl.BlockSpec(memory_space=pltpu.SMEM),
            pl.BlockSpec((None, BLOCK, ATTN_WIDTH), cur),
            pl.BlockSpec((None, BLOCK, kv_dup), prev),
            pl.BlockSpec((None, BLOCK, kv_dup), cur),
            pl.BlockSpec((None, BLOCK, kv_dup), prev),
            pl.BlockSpec((None, BLOCK, kv_dup), cur),
            pl.BlockSpec((None, N_KV_HEADS, Q_PER_KV * BLOCK, 2 * BLOCK),
                         lambda b, i: (jnp.minimum(i, 1), 0, 0, 0)),
            pl.BlockSpec((2, LANES), lambda b, i: (0, 0)),
        ],
        out_specs=pl.BlockSpec((None, BLOCK, ATTN_WIDTH), cur),
        out_shape=jax.ShapeDtypeStruct((n_b, seq, ATTN_WIDTH), BF16),
        compiler_params=_cparams(("parallel", "arbitrary")),
    )(sinks, q3, kd3, kd3, vd3, vd3, _attn_bias(), half)


def _mix_body(y_ref, a_ref, gs_ref, ga_ref, x_ref, wglu_ref, wsp_ref, wap_ref, wo_ref,
              g2_ref, wr_ref, br_ref, x1_ref, lg_ref):
    z = jnp.dot(y_ref[...], wglu_ref[...], preferred_element_type=F32)
    glu = z[:, :SSM_WIDTH] * jax.nn.sigmoid(z[:, SSM_WIDTH:])
    y_ssm = jnp.dot(glu.astype(BF16), wsp_ref[...], preferred_element_type=F32)
    y_attn = jnp.dot(a_ref[...], wap_ref[...], preferred_element_type=F32)
    mixed = jax.nn.sigmoid(gs_ref[...]) * y_ssm + jax.nn.sigmoid(ga_ref[...]) * y_attn
    x1 = x_ref[...] + jnp.dot(mixed.astype(BF16), wo_ref[...], preferred_element_type=F32)
    x1_ref[...] = x1
    h2 = _rms(x1, g2_ref[...]).astype(BF16)
    lg_ref[...] = jnp.dot(h2, wr_ref[...], preferred_element_type=F32) + br_ref[...]


def _mix(y2, a2, gs, ga, x2, w_glu, w_sp, w_ap, w_o, norm2_g, w_r_pad, b_r_pad):
    t = x2.shape[0]
    tm = min(TM_MIX, t)
    tok = lambda w: pl.BlockSpec((tm, w), lambda i: (i, 0))
    full = lambda a: pl.BlockSpec(a.shape, lambda i: (0, 0))
    return pl.pallas_call(
        _mix_body,
        name="mix",
        grid=(t // tm,),
        in_specs=[tok(SSM_WIDTH), tok(ATTN_WIDTH), tok(D_MODEL), tok(D_MODEL), tok(D_MODEL),
                  full(w_glu), full(w_sp), full(w_ap), full(w_o), full(norm2_g),
                  full(w_r_pad), full(b_r_pad)],
        out_specs=[tok(D_MODEL), tok(LANES)],
        out_shape=[jax.ShapeDtypeStruct((t, D_MODEL), F32), jax.ShapeDtypeStruct((t, LANES), F32)],
        compiler_params=_cparams(("parallel",)),
    )(y2, a2, gs, ga, x2, w_glu, w_sp, w_ap, w_o, norm2_g, w_r_pad, b_r_pad)


def _route_body(lg_ref, tri_ref, slab_ref, cnt_ref, base):
    @pl.when(pl.program_id(0) == 0)
    def _():
        base[...] = jnp.zeros_like(base)

    tt = lg_ref.shape[0]
    lane = lax.broadcasted_iota(jnp.int32, (tt, LANES), 1)
    l = jnp.where(lane < N_EXPERTS, lg_ref[...], -jnp.inf)
    tops, hots = [], []
    for _ in range(TOP_K):
        m = jnp.max(l, axis=-1, keepdims=True)
        idx = jnp.min(jnp.where(l == m, lane, LANES), axis=-1, keepdims=True)
        hot = lane == idx
        tops.append((m, idx))
        hots.append(hot)
        l = jnp.where(hot, -jnp.inf, l)
    es = [jnp.exp(m - tops[0][0]) for m, _ in tops]
    den = es[0] + es[1] + es[2] + es[3]
    member = jnp.zeros((tt, LANES), F32)
    for hot in hots:
        member = member + jnp.where(hot, 1.0, 0.0)
    before = jnp.dot(tri_ref[...], member.astype(BF16), preferred_element_type=F32) + base[...]
    slab = jnp.zeros((tt, LANES), F32)
    for k in range(TOP_K):
        pos = jnp.sum(jnp.where(hots[k], before, 0.0), axis=-1, keepdims=True)
        slab = jnp.where(lane == k, tops[k][1].astype(F32), slab)
        slab = jnp.where(lane == TOP_K + k, es[k] / den, slab)
        slab = jnp.where(lane == 2 * TOP_K + k, pos, slab)
    slab_ref[...] = slab
    base[...] = base[...] + jnp.sum(member, axis=0, keepdims=True)
    cnt_ref[...] = base[...]


def _route(logits):
    t = logits.shape[0]
    tt = min(TT_ROUTE, t)
    tri = jnp.tril(jnp.ones((tt, tt), F32), -1).astype(BF16)
    return pl.pallas_call(
        _route_body,
        name="route",
        grid=(t // tt,),
        in_specs=[pl.BlockSpec((tt, LANES), lambda i: (i, 0)),
                  pl.BlockSpec((tt, tt), lambda i: (0, 0))],
        out_specs=[pl.BlockSpec((tt, LANES), lambda i: (i, 0)),
                   pl.BlockSpec((1, LANES), lambda i: (0, 0))],
        out_shape=[jax.ShapeDtypeStruct((t, LANES), F32), jax.ShapeDtypeStruct((1, LANES), F32)],
        scratch_shapes=[pltpu.VMEM((1, LANES), F32)],
        compiler_params=_cparams(("arbitrary",)),
    )(logits, tri)


def _row_copies(n_tok, issue_one):
    def body(t8, carry):
        for i in range(SUBLANES):
            issue_one(t8 * SUBLANES + i)
        return carry
    lax.fori_loop(0, n_tok // SUBLANES, body, 0)


def _dispatch_body(pe_ref, nu_ref, dest_ref, x_ref, xs_ref, zbuf, sem, zsem):
    tt = x_ref.shape[0]
    n_blocks = xs_ref.shape[0] // MOE_BLOCK

    @pl.when(pl.program_id(0) == 0)
    def _():
        zbuf[...] = jnp.zeros_like(zbuf)

        def zero_block(row):
            return pltpu.make_async_copy(
                zbuf, xs_ref.at[pl.ds(pl.multiple_of(row, MOE_BLOCK), MOE_BLOCK)], zsem)

        def each(act):
            for e in range(N_EXPERTS):
                @pl.when(pe_ref[e] >= MOE_BLOCK)
                def _():
                    act(zero_block(pe_ref[e] - MOE_BLOCK))

            def tail(i, carry):
                act(zero_block(i * MOE_BLOCK))
                return carry
            lax.fori_loop(nu_ref[0], n_blocks, tail, 0)

        each(lambda c: c.start())
        each(lambda c: c.wait())

    def issue(t):
        for k in range(TOP_K):
            d = dest_ref[0, 0, t * TOP_K + k]
            pltpu.make_async_copy(x_ref.at[pl.ds(t, 1)], xs_ref.at[pl.ds(d, 1)], sem).start(priority=k % 2)

    _row_copies(tt, issue)
    for _ in range(TOP_K):
        pltpu.make_async_copy(x_ref, xs_ref.at[pl.ds(0, tt)], sem).wait()


def _dispatch(pad_ends, n_used, x1, dest3, n_rows):
    t = x1.shape[0]
    tt = dest3.shape[2] // TOP_K
    grid_spec = pltpu.PrefetchScalarGridSpec(
        num_scalar_prefetch=2,
        grid=(t // tt,),
        in_specs=[pl.BlockSpec((1, 1, tt * TOP_K), lambda i, pe, nu: (i, 0, 0), memory_space=pltpu.SMEM),
                  pl.BlockSpec((tt, D_MODEL), lambda i, pe, nu: (i, 0))],
        out_specs=pl.BlockSpec(memory_space=pl.ANY),
        scratch_shapes=[pltpu.VMEM((MOE_BLOCK, D_MODEL), F32),
                        pltpu.SemaphoreType.DMA(()), pltpu.SemaphoreType.DMA(())],
    )
    return pl.pallas_call(
        _dispatch_body,
        name="dispatch",
        grid_spec=grid_spec,
        out_shape=jax.ShapeDtypeStruct((n_rows, D_MODEL), F32),
        compiler_params=_cparams(("arbitrary",)),
    )(pad_ends, n_used, dest3, x1)


def _ffn_body(be_ref, nu_ref, xs_ref, g_ref, wgu_ref, bgu_ref, wd_ref, bd_ref, ys_ref, wgu_bf, wd_bf):
    i = pl.program_id(0)

    @pl.when((i == 0) | (be_ref[i] != be_ref[jnp.maximum(i - 1, 0)]))
    def _():
        wgu_bf[...] = wgu_ref[...].astype(BF16)
        wd_bf[...] = wd_ref[...].astype(BF16)

    @pl.when(i < nu_ref[0])
    def _():
        h = _rms(xs_ref[...], g_ref[...]).astype(BF16)
        gu = jnp.dot(h, wgu_bf[...], preferred_element_type=F32) + bgu_ref[...]
        gate = jnp.minimum(gu[:, :D_FF], SWIGLU_LIMIT)
        up = jnp.clip(gu[:, D_FF:], -SWIGLU_LIMIT, SWIGLU_LIMIT)
        act = (up + 1.0) * gate * jax.nn.sigmoid(SWIGLU_ALPHA * gate)
        ys_ref[...] = jnp.dot(act.astype(BF16), wd_bf[...], preferred_element_type=F32) + bd_ref[...]

    @pl.when(pl.program_id(0) >= nu_ref[0])
    def _():
        ys_ref[...] = jnp.zeros_like(ys_ref)


def _moe_ffn(block_e, n_used, xs, norm2_g, w_gu, b_gu, w_d, b_d):
    n_rows = xs.shape[0]
    n_blocks = n_rows // MOE_BLOCK
    row_blk = lambda i, be, nu: (jnp.minimum(i, nu[0] - 1), 0)
    by_e = lambda i, be, nu: (be[i], 0, 0)
    grid_spec = pltpu.PrefetchScalarGridSpec(
        num_scalar_prefetch=2,
        grid=(n_blocks,),
        in_specs=[
            pl.BlockSpec((MOE_BLOCK, D_MODEL), row_blk),
            pl.BlockSpec((1, D_MODEL), lambda i, be, nu: (0, 0)),
            pl.BlockSpec((None, D_MODEL, 2 * D_FF), by_e),
            pl.BlockSpec((None, 1, 2 * D_FF), by_e),
            pl.BlockSpec((None, D_FF, D_MODEL), by_e),
            pl.BlockSpec((None, 1, D_MODEL), by_e),
        ],
        out_specs=pl.BlockSpec((MOE_BLOCK, D_MODEL), lambda i, be, nu: (i, 0)),
        scratch_shapes=[pltpu.VMEM((D_MODEL, 2 * D_FF), BF16), pltpu.VMEM((D_FF, D_MODEL), BF16)],
    )
    return pl.pallas_call(
        _ffn_body,
        name="moe_ffn",
        grid_spec=grid_spec,
        out_shape=jax.ShapeDtypeStruct((n_rows, D_MODEL), F32),
        compiler_params=_cparams(("arbitrary",)),
    )(block_e, n_used, xs, norm2_g, w_gu, b_gu, w_d, b_d)


def _combine_body(dest_ref, x_ref, slab_ref, ys_ref, o_ref, buf, sem):
    tt = x_ref.shape[0]

    def issue(t):
        for k in range(TOP_K):
            d = dest_ref[0, 0, t * TOP_K + k]
            pltpu.make_async_copy(ys_ref.at[pl.ds(d, 1)], buf.at[k, pl.ds(t, 1)], sem).start(priority=k % 2)

    _row_copies(tt, issue)
    for k in range(TOP_K):
        pltpu.make_async_copy(ys_ref.at[pl.ds(0, tt)], buf.at[k], sem).wait()
    acc = x_ref[...]
    for k in range(TOP_K):
        acc = acc + slab_ref[:, TOP_K + k:TOP_K + k + 1] * buf[k]
    o_ref[...] = acc


def _combine(x1, slab, dest3, ys):
    t = x1.shape[0]
    tt = dest3.shape[2] // TOP_K
    return pl.pallas_call(
        _combine_body,
        name="combine",
        grid=(t // tt,),
        in_specs=[pl.BlockSpec((1, 1, tt * TOP_K), lambda i: (i, 0, 0), memory_space=pltpu.SMEM),
                  pl.BlockSpec((tt, D_MODEL), lambda i: (i, 0)),
                  pl.BlockSpec((tt, LANES), lambda i: (i, 0)),
                  pl.BlockSpec(memory_space=pl.ANY)],
        out_specs=pl.BlockSpec((tt, D_MODEL), lambda i: (i, 0)),
        out_shape=jax.ShapeDtypeStruct((t, D_MODEL), F32),
        scratch_shapes=[pltpu.VMEM((TOP_K, tt, D_MODEL), F32), pltpu.SemaphoreType.DMA(())],
        compiler_params=_cparams(("arbitrary",)),
    )(dest3, x1, slab, ys)


def _moe(x1, logits, norm2_g, w_gate_up, b_gate_up, w_down, b_down):
    t = x1.shape[0]
    n_assign = t * TOP_K
    n_blocks = -(-n_assign // MOE_BLOCK) + N_EXPERTS
    n_rows = n_blocks * MOE_BLOCK

    slab, cnt = _route(logits)
    ids = slab[:, :TOP_K].astype(jnp.int32)
    pos = slab[:, 2 * TOP_K:3 * TOP_K].astype(jnp.int32)
    counts = cnt[0, :N_EXPERTS].astype(jnp.int32)
    padded = (counts + MOE_BLOCK - 1) // MOE_BLOCK * MOE_BLOCK
    pad_ends = jnp.cumsum(padded)
    pad_starts = pad_ends - padded
    dest = pad_starts[ids] + pos
    n_used = (pad_ends[-1] // MOE_BLOCK).astype(jnp.int32)
    blk = jnp.minimum(jnp.arange(n_blocks, dtype=jnp.int32), n_used - 1)
    block_e = jnp.minimum(jnp.sum((pad_ends[None, :] <= (blk * MOE_BLOCK)[:, None]).astype(jnp.int32), axis=1),
                          N_EXPERTS - 1)

    tt = min(TT_MOVE, t)
    dest3 = dest.reshape(t // tt, 1, tt * TOP_K)
    xs = _dispatch(pad_ends.astype(jnp.int32), n_used.reshape(1), x1, dest3, n_rows)
    ys = _moe_ffn(block_e, n_used.reshape(1), xs, norm2_g,
                  w_gate_up, b_gate_up.reshape(N_EXPERTS, 1, 2 * D_FF),
                  w_down, b_down.reshape(N_EXPERTS, 1, D_MODEL))
    return _combine(x1, slab, dest3, ys)


def _layer(x, norm1_g, w_in, ssm_a_re, ssm_a_im, ssm_log_dt, ssm_b_re, ssm_b_im, ssm_c_re,
           ssm_c_im, ssm_d, w_glu, w_ssm_proj, q_norm_g, k_norm_g, attn_sinks, w_attn_proj,
           w_out, norm2_g, w_router, b_router, w_gate_up, b_gate_up, w_down, b_down):
    n_b, seq, d = x.shape
    t = n_b * seq
    x2 = x.reshape(t, d)
    u, q, kd, vd, gs, ga = _in_proj(x2, norm1_g.reshape(1, d), w_in, q_norm_g, k_norm_g)

    abar_re, abar_im, bbt_re, bbt_im = _s5_prep(ssm_a_re, ssm_a_im, ssm_log_dt, ssm_b_re, ssm_b_im)
    b_mat, c_mat, ab = _s5_matrices(abar_re, abar_im, bbt_re, bbt_im, ssm_c_re, ssm_c_im)
    y = _s5(u.reshape(n_b, seq, SSM_WIDTH), b_mat, c_mat, ab, ssm_d.reshape(1, SSM_WIDTH))

    a = _attention(q.reshape(n_b, seq, ATTN_WIDTH), kd.reshape(n_b, seq, KV_DUP),
                   vd.reshape(n_b, seq, KV_DUP), attn_sinks)

    w_r_pad = jnp.zeros((d, LANES), BF16).at[:, :N_EXPERTS].set(w_router.astype(BF16))
    b_r_pad = jnp.zeros((1, LANES), F32).at[0, :N_EXPERTS].set(b_router)
    norm2 = norm2_g.reshape(1, d)
    x1, logits = _mix(y.reshape(t, SSM_WIDTH), a.reshape(t, ATTN_WIDTH), gs, ga, x2,
                      w_glu.astype(BF16), w_ssm_proj.astype(BF16), w_attn_proj.astype(BF16),
                      w_out.astype(BF16), norm2, w_r_pad, b_r_pad)
    out = _moe(x1, logits, norm2, w_gate_up, b_gate_up, w_down, b_down)
    return out.reshape(n_b, seq, d)


def kernel(x, norm1_g, w_in, ssm_a_re, ssm_a_im, ssm_log_dt, ssm_b_re, ssm_b_im, ssm_c_re, ssm_c_im, ssm_d, w_glu, w_ssm_proj, q_norm_g, k_norm_g, attn_sinks, w_attn_proj, w_out, norm2_g, w_router, b_router, w_gate_up, b_gate_up, w_down, b_down):
    for layer in range(norm1_g.shape[0]):
        x = _layer(
            x, norm1_g[layer], w_in[layer], ssm_a_re[layer], ssm_a_im[layer], ssm_log_dt[layer],
            ssm_b_re[layer], ssm_b_im[layer], ssm_c_re[layer], ssm_c_im[layer], ssm_d[layer],
            w_glu[layer], w_ssm_proj[layer], q_norm_g[layer], k_norm_g[layer], attn_sinks[layer],
            w_attn_proj[layer], w_out[layer], norm2_g[layer], w_router[layer], b_router[layer],
            w_gate_up[layer], b_gate_up[layer], w_down[layer], b_down[layer])
    return x
```

```python
import functools

import jax
import jax.numpy as jnp
from jax import lax
from jax.experimental import pallas as pl
from jax.experimental.pallas import tpu as pltpu

D_MODEL = 1024
NORM_EPS = 1e-5
SSM_WIDTH = 1024
SSM_GROUP_CH = 16
SSM_GROUPS = SSM_WIDTH // SSM_GROUP_CH
SSM_STATE = 64
HEAD_DIM = 64
N_Q_HEADS = 16
N_KV_HEADS = 4
Q_PER_KV = N_Q_HEADS // N_KV_HEADS
ATTN_WIDTH = N_Q_HEADS * HEAD_DIM
KV_WIDTH = N_KV_HEADS * HEAD_DIM
WINDOW = 128
BLOCK = 128
N_EXPERTS = 32
TOP_K = 4
D_FF = 1024
SWIGLU_LIMIT = 7.0
SWIGLU_ALPHA = 1.702
MOE_BLOCK = 512

LANES = 128
SUBLANES = 8
VMEM_LIMIT = 56 * 1024 * 1024

S5_GROUP_BLOCK = 8
S5_CH = S5_GROUP_BLOCK * SSM_GROUP_CH
S5_ST = S5_GROUP_BLOCK * SSM_STATE
S5_CHUNK = 256
S5_PITCH = S5_CHUNK + SUBLANES

TM_IN = 256
TM_MIX = 512
TT_ROUTE = 512
TT_MOVE = 256
FFN_CHUNK = 256

F32 = jnp.float32
BF16 = jnp.bfloat16


def _cparams(sem):
    return pltpu.CompilerParams(dimension_semantics=sem, vmem_limit_bytes=VMEM_LIMIT)


def _rms(x, g):
    return x * lax.rsqrt(jnp.mean(x * x, axis=-1, keepdims=True) + NORM_EPS) * g


KV_DUP = N_KV_HEADS * LANES


def _in_proj_body(x_ref, g_ref, w_ref, qg_ref, kg_ref, u_ref, q_ref, kd_ref, vd_ref, gs_ref, ga_ref):
    h = _rms(x_ref[...], g_ref[...]).astype(BF16)
    off = [0]

    def proj(width):
        lo = off[0]
        off[0] = lo + width
        return jnp.dot(h, w_ref[:, lo:lo + width], preferred_element_type=F32)

    u_ref[...] = proj(SSM_WIDTH)

    q = proj(ATTN_WIDTH)
    lo_sel = lax.broadcasted_iota(jnp.int32, (q.shape[0], LANES), 1) < HEAD_DIM
    for j in range(ATTN_WIDTH // LANES):
        qq = q[:, j * LANES:(j + 1) * LANES]
        sq = qq * qq
        ms_lo = jnp.sum(jnp.where(lo_sel, sq, 0.0), axis=-1, keepdims=True) / HEAD_DIM
        ms_hi = jnp.sum(jnp.where(lo_sel, 0.0, sq), axis=-1, keepdims=True) / HEAD_DIM
        r = jnp.where(lo_sel, lax.rsqrt(ms_lo + NORM_EPS), lax.rsqrt(ms_hi + NORM_EPS))
        q_ref[:, j * LANES:(j + 1) * LANES] = (qq * r * qg_ref[...]).astype(q_ref.dtype)

    kd = proj(KV_DUP)
    for kv in range(N_KV_HEADS):
        grp = slice(kv * LANES, (kv + 1) * LANES)
        kd_ref[:, grp] = _rms(kd[:, grp], kg_ref[...]).astype(kd_ref.dtype)
    vd_ref[...] = proj(KV_DUP).astype(vd_ref.dtype)
    gs_ref[...] = proj(D_MODEL)
    ga_ref[...] = proj(D_MODEL)


def _in_proj_weight(w_in):
    d = w_in.shape[0]
    cuts = (SSM_WIDTH, SSM_WIDTH + ATTN_WIDTH, SSM_WIDTH + ATTN_WIDTH + KV_WIDTH,
            SSM_WIDTH + ATTN_WIDTH + 2 * KV_WIDTH)
    w_uq, w_k, w_v, w_g = (w_in[:, :cuts[1]], w_in[:, cuts[1]:cuts[2]], w_in[:, cuts[2]:cuts[3]],
                           w_in[:, cuts[3]:])

    def twice(w):
        w = w.reshape(d, N_KV_HEADS, 1, HEAD_DIM)
        return jnp.concatenate([w, w], axis=2).reshape(d, KV_DUP)

    return jnp.concatenate([w_uq, twice(w_k), twice(w_v), w_g], axis=1).astype(BF16)


def _in_proj(x2, norm_g, w_in, q_norm_g, k_norm_g):
    t = x2.shape[0]
    w = _in_proj_weight(w_in)
    qg = (jnp.tile(q_norm_g, LANES // HEAD_DIM) * HEAD_DIM ** -0.5).reshape(1, LANES)
    kg = jnp.tile(k_norm_g, LANES // HEAD_DIM).reshape(1, LANES)
    outs = ((SSM_WIDTH, F32), (ATTN_WIDTH, BF16), (KV_DUP, BF16), (KV_DUP, BF16),
            (D_MODEL, F32), (D_MODEL, F32))
    const = lambda a: pl.BlockSpec(a.shape, lambda i: (0, 0))
    return pl.pallas_call(
        _in_proj_body,
        name="in_proj",
        grid=(t // TM_IN,),
        in_specs=[pl.BlockSpec((TM_IN, D_MODEL), lambda i: (i, 0)), const(norm_g), const(w),
                  const(qg), const(kg)],
        out_specs=[pl.BlockSpec((TM_IN, n), lambda i: (i, 0)) for n, _ in outs],
        out_shape=[jax.ShapeDtypeStruct((t, n), dt) for n, dt in outs],
        compiler_params=_cparams(("parallel",)),
    )(x2, norm_g, w, qg, kg)


def _s5_prep_body(are_ref, aim_ref, ldt_ref, bre_ref, bim_ref,
                  abr_ref, abi_ref, bbr_ref, bbi_ref):
    a_re = are_ref[...]
    a_im = aim_ref[...]
    dt = jnp.exp(ldt_ref[...])
    mag = jnp.exp(a_re * dt)
    abar_re = mag * jnp.cos(a_im * dt)
    abar_im = mag * jnp.sin(a_im * dt)
    den = a_re * a_re + a_im * a_im
    q_re = ((abar_re - 1.0) * a_re + abar_im * a_im) / den
    q_im = (abar_im * a_re - (abar_re - 1.0) * a_im) / den
    abr_ref[...] = abar_re
    abi_ref[...] = abar_im
    for h in range(SSM_GROUP_CH):
        bbr_ref[h] = q_re * bre_ref[h] - q_im * bim_ref[h]
        bbi_ref[h] = q_re * bim_ref[h] + q_im * bre_ref[h]


def _s5_prep(a_re, a_im, log_dt, b_re, b_im):
    g, p, hc = SSM_GROUPS, SSM_STATE, SSM_GROUP_CH
    bt_re = jnp.transpose(b_re, (2, 0, 1))
    bt_im = jnp.transpose(b_im, (2, 0, 1))
    return pl.pallas_call(
        _s5_prep_body,
        name="s5_prep",
        out_shape=[
            jax.ShapeDtypeStruct((g, p), F32),
            jax.ShapeDtypeStruct((g, p), F32),
            jax.ShapeDtypeStruct((hc, g, p), F32),
            jax.ShapeDtypeStruct((hc, g, p), F32),
        ],
    )(a_re, a_im, log_dt.reshape(g, 1), bt_re, bt_im)


def _s5_matrices(abar_re, abar_im, bbt_re, bbt_im, c_re, c_im):
    nb, gb, hc, p = SSM_GROUPS // S5_GROUP_BLOCK, S5_GROUP_BLOCK, SSM_GROUP_CH, SSM_STATE
    eye = jnp.eye(gb, dtype=F32)

    def b_block(bbt):
        t = jnp.transpose(bbt, (1, 0, 2)).reshape(nb, gb, hc, p)
        return jnp.einsum('jghp,gk->jghkp', t, eye).reshape(nb, gb * hc, gb * p)

    def c_block(c):
        t = c.reshape(nb, gb, hc, p)
        return jnp.einsum('jghp,gk->jgpkh', t, eye).reshape(nb, gb * p, gb * hc)

    b_mat = jnp.concatenate([b_block(bbt_re), b_block(bbt_im)], axis=2).astype(BF16)
    c_mat = jnp.concatenate([c_block(c_re), c_block(-c_im)], axis=1).astype(BF16)
    ab = jnp.concatenate([abar_re.reshape(nb, 1, gb * p), abar_im.reshape(nb, 1, gb * p)], axis=2)
    ab = jnp.broadcast_to(ab, (nb, SUBLANES, 2 * gb * p))
    return b_mat, c_mat, ab


def _s5_body(u_ref, b_ref, c_ref, ab_ref, d_ref, o_ref, ubuf, utm, sbuf, state, ytm):
    n_b = u_ref.shape[0]
    chunk = u_ref.shape[1]

    @pl.when(pl.program_id(1) == 0)
    def _():
        state[...] = jnp.zeros_like(state)

    for b in range(n_b):
        ubuf[b * S5_PITCH:b * S5_PITCH + chunk, :] = u_ref[b]

    def to_time_major(l8, carry):
        for i in range(SUBLANES):
            l = l8 * SUBLANES + i
            row = pl.multiple_of(l * n_b, SUBLANES)
            utm[pl.ds(row, n_b), :] = ubuf[pl.ds(l, n_b, stride=S5_PITCH), :]
        return carry

    lax.fori_loop(0, chunk // SUBLANES, to_time_major, 0)

    sbuf[...] = jnp.dot(utm[...].astype(BF16), b_ref[...], preferred_element_type=F32)

    a_re = ab_ref[:, :S5_ST]
    a_im = ab_ref[:, S5_ST:]

    def scan(l8, carry):
        s_re, s_im = carry
        for i in range(SUBLANES):
            row = pl.multiple_of((l8 * SUBLANES + i) * n_b, SUBLANES)
            n_re = a_re * s_re - a_im * s_im + sbuf[pl.ds(row, n_b), :S5_ST]
            n_im = a_re * s_im + a_im * s_re + sbuf[pl.ds(row, n_b), S5_ST:]
            sbuf[pl.ds(row, n_b), :S5_ST] = n_re
            sbuf[pl.ds(row, n_b), S5_ST:] = n_im
            s_re, s_im = n_re, n_im
        return s_re, s_im

    s_re, s_im = lax.fori_loop(0, chunk // SUBLANES, scan, (state[:, :S5_ST], state[:, S5_ST:]))
    state[:, :S5_ST] = s_re
    state[:, S5_ST:] = s_im

    y = jnp.dot(sbuf[...].astype(BF16), c_ref[...], preferred_element_type=F32)
    y = y + d_ref[...] * utm[...]
    ytm[...] = jax.nn.gelu(y)
    for b in range(n_b):
        o_ref[b] = ytm[pl.ds(b, chunk, stride=n_b), :].astype(o_ref.dtype)


def _s5(u3, b_mat, c_mat, ab, d_flat):
    n_b, seq, _ = u3.shape
    assert n_b == SUBLANES, "the scan keeps one sequence per sublane"
    chunk = min(S5_CHUNK, seq)
    nb = SSM_GROUPS // S5_GROUP_BLOCK
    return pl.pallas_call(
        _s5_body,
        name="s5_scan",
        grid=(nb, seq // chunk),
        in_specs=[
            pl.BlockSpec((n_b, chunk, S5_CH), lambda j, c: (0, c, j)),
            pl.BlockSpec((None, S5_CH, 2 * S5_ST), lambda j, c: (j, 0, 0)),
            pl.BlockSpec((None, 2 * S5_ST, S5_CH), lambda j, c: (j, 0, 0)),
            pl.BlockSpec((None, SUBLANES, 2 * S5_ST), lambda j, c: (j, 0, 0)),
            pl.BlockSpec((1, S5_CH), lambda j, c: (0, j)),
        ],
        out_specs=pl.BlockSpec((n_b, chunk, S5_CH), lambda j, c: (0, c, j)),
        out_shape=jax.ShapeDtypeStruct((n_b, seq, SSM_WIDTH), BF16),
        scratch_shapes=[
            pltpu.VMEM((n_b * S5_PITCH, S5_CH), F32),
            pltpu.VMEM((chunk * n_b, S5_CH), F32),
            pltpu.VMEM((chunk * n_b, 2 * S5_ST), F32),
            pltpu.VMEM((n_b, 2 * S5_ST), F32),
            pltpu.VMEM((chunk * n_b, S5_CH), F32),
        ],
        compiler_params=_cparams(("parallel", "arbitrary")),
    )(u3, b_mat, c_mat, ab, d_flat)


def _attn_body(sinks_ref, q_ref, kp_ref, kc_ref, vp_ref, vc_ref, bias_ref, half_ref, o_ref):
    lo_mask = half_ref[0:1, :]
    hi_mask = half_ref[1:2, :]
    lo_sel = lax.broadcasted_iota(jnp.int32, (BLOCK, LANES), 1) < HEAD_DIM
    for kv in range(N_KV_HEADS):
        grp = slice(kv * LANES, (kv + 1) * LANES)
        kw = jnp.concatenate([kp_ref[:, grp], kc_ref[:, grp]], axis=0)
        vw = jnp.concatenate([vp_ref[:, grp], vc_ref[:, grp]], axis=0)
        qs = []
        for pair in range(Q_PER_KV // 2):
            lanes = slice((2 * kv + pair) * LANES, (2 * kv + pair + 1) * LANES)
            qp = q_ref[:, lanes]
            qs += [qp * lo_mask, qp * hi_mask]
        q_st = jnp.concatenate(qs, axis=0)
        s = lax.dot_general(q_st, kw, (((1,), (1,)), ((), ())), preferred_element_type=F32)
        s = s + bias_ref[kv]
        ps = []
        for g in range(Q_PER_KV):
            sg = s[g * BLOCK:(g + 1) * BLOCK]
            sink = sinks_ref[kv * Q_PER_KV + g]
            m = jnp.maximum(jnp.max(sg, axis=-1, keepdims=True), sink)
            p = jnp.exp(sg - m)
            den = jnp.sum(p, axis=-1, keepdims=True) + jnp.exp(sink - m)
            ps.append((p / den).astype(BF16))
        p_st = jnp.concatenate(ps, axis=0)
        o = jnp.dot(p_st, vw, preferred_element_type=F32)
        for pair in range(Q_PER_KV // 2):
            even = o[(2 * pair) * BLOCK:(2 * pair + 1) * BLOCK]
            odd = o[(2 * pair + 1) * BLOCK:(2 * pair + 2) * BLOCK]
            lanes = slice((2 * kv + pair) * LANES, (2 * kv + pair + 1) * LANES)
            o_ref[:, lanes] = jnp.where(lo_sel, even, odd).astype(o_ref.dtype)


def _attn_bias():
    h = jnp.arange(1, N_Q_HEADS + 1, dtype=F32)
    slopes = jnp.exp2(-8.0 * h / N_Q_HEADS)
    qi = jnp.arange(BLOCK)[:, None]
    kj = jnp.arange(2 * BLOCK)[None, :]
    dist = qi - kj + BLOCK
    band = (dist >= 0) & (dist < WINDOW)
    bias = -slopes[:, None, None] * dist.astype(F32)
    later = jnp.where(band[None], bias, -jnp.inf)
    first = jnp.where((band & (kj >= BLOCK))[None], bias, -jnp.inf)
    both = jnp.stack([first, later])
    return both.reshape(2, N_KV_HEADS, Q_PER_KV * BLOCK, 2 * BLOCK)


def _attention(q3, kd3, vd3, sinks):
    n_b, seq, _ = q3.shape
    n_blk = seq // BLOCK
    kv_dup = N_KV_HEADS * LANES
    prev = lambda b, i: (b, jnp.maximum(i - 1, 0), 0)
    cur = lambda b, i: (b, i, 0)
    lane = jnp.arange(LANES)
    half = jnp.stack([lane < HEAD_DIM, lane >= HEAD_DIM]).astype(BF16)
    return pl.pallas_call(
        _attn_body,
        name="swa_attention",
        grid=(n_b, n_blk),
        in_specs=[
            pl.BlockSpec(memory_space=pltpu.SMEM),
            pl.BlockSpec((None, BLOCK, ATTN_WIDTH), cur),
            pl.BlockSpec((None, BLOCK, kv_dup), prev),
            pl.BlockSpec((None, BLOCK, kv_dup), cur),
            pl.BlockSpec((None, BLOCK, kv_dup), prev),
            pl.BlockSpec((None, BLOCK, kv_dup), cur),
            pl.BlockSpec((None, N_KV_HEADS, Q_PER_KV * BLOCK, 2 * BLOCK),
                         lambda b, i: (jnp.minimum(i, 1), 0, 0, 0)),
            pl.BlockSpec((2, LANES), lambda b, i: (0, 0)),
        ],
        out_specs=pl.BlockSpec((None, BLOCK, ATTN_WIDTH), cur),
        out_shape=jax.ShapeDtypeStruct((n_b, seq, ATTN_WIDTH), BF16),
        compiler_params=_cparams(("parallel", "arbitrary")),
    )(sinks, q3, kd3, kd3, vd3, vd3, _attn_bias(), half)


def _mix_body(y_ref, a_ref, gs_ref, ga_ref, x_ref, wglu_ref, wsp_ref, wap_ref, wo_ref,
              g2_ref, wr_ref, br_ref, x1_ref, lg_ref):
    z = jnp.dot(y_ref[...], wglu_ref[...], preferred_element_type=F32)
    glu = z[:, :SSM_WIDTH] * jax.nn.sigmoid(z[:, SSM_WIDTH:])
    y_ssm = jnp.dot(glu.astype(BF16), wsp_ref[...], preferred_element_type=F32)
    y_attn = jnp.dot(a_ref[...], wap_ref[...], preferred_element_type=F32)
    mixed = jax.nn.sigmoid(gs_ref[...]) * y_ssm + jax.nn.sigmoid(ga_ref[...]) * y_attn
    x1 = x_ref[...] + jnp.dot(mixed.astype(BF16), wo_ref[...], preferred_element_type=F32)
    x1_ref[...] = x1
    h2 = _rms(x1, g2_ref[...]).astype(BF16)
    lg_ref[...] = jnp.dot(h2, wr_ref[...], preferred_element_type=F32) + br_ref[...]


def _mix(y2, a2, gs, ga, x2, w_glu, w_sp, w_ap, w_o, norm2_g, w_r_pad, b_r_pad):
    t = x2.shape[0]
    tm = min(TM_MIX, t)
    tok = lambda w: pl.BlockSpec((tm, w), lambda i: (i, 0))
    full = lambda a: pl.BlockSpec(a.shape, lambda i: (0, 0))
    return pl.pallas_call(
        _mix_body,
        name="mix",
        grid=(t // tm,),
        in_specs=[tok(SSM_WIDTH), tok(ATTN_WIDTH), tok(D_MODEL), tok(D_MODEL), tok(D_MODEL),
                  full(w_glu), full(w_sp), full(w_ap), full(w_o), full(norm2_g),
                  full(w_r_pad), full(b_r_pad)],
        out_specs=[tok(D_MODEL), tok(LANES)],
        out_shape=[jax.ShapeDtypeStruct((t, D_MODEL), F32), jax.ShapeDtypeStruct((t, LANES), F32)],
        compiler_params=_cparams(("parallel",)),
    )(y2, a2, gs, ga, x2, w_glu, w_sp, w_ap, w_o, norm2_g, w_r_pad, b_r_pad)


def _route_body(lg_ref, tri_ref, slab_ref, cnt_ref, base):
    @pl.when(pl.program_id(0) == 0)
    def _():
        base[...] = jnp.zeros_like(base)

    tt = lg_ref.shape[0]
    lane = lax.broadcasted_iota(jnp.int32, (tt, LANES), 1)
    l = jnp.where(lane < N_EXPERTS, lg_ref[...], -jnp.inf)
    tops, hots = [], []
    for _ in range(TOP_K):
        m = jnp.max(l, axis=-1, keepdims=True)
        idx = jnp.min(jnp.where(l == m, lane, LANES), axis=-1, keepdims=True)
        hot = lane == idx
        tops.append((m, idx))
        hots.append(hot)
        l = jnp.where(hot, -jnp.inf, l)
    es = [jnp.exp(m - tops[0][0]) for m, _ in tops]
    den = es[0] + es[1] + es[2] + es[3]
    member = jnp.zeros((tt, LANES), F32)
    for hot in hots:
        member = member + jnp.where(hot, 1.0, 0.0)
    before = jnp.dot(tri_ref[...], member.astype(BF16), preferred_element_type=F32) + base[...]
    slab = jnp.zeros((tt, LANES), F32)
    for k in range(TOP_K):
        pos = jnp.sum(jnp.where(hots[k], before, 0.0), axis=-1, keepdims=True)
        slab = jnp.where(lane == k, tops[k][1].astype(F32), slab)
        slab = jnp.where(lane == TOP_K + k, es[k] / den, slab)
        slab = jnp.where(lane == 2 * TOP_K + k, pos, slab)
    slab_ref[...] = slab
    base[...] = base[...] + jnp.sum(member, axis=0, keepdims=True)
    cnt_ref[...] = base[...]


def _route(logits):
    t = logits.shape[0]
    tt = min(TT_ROUTE, t)
    tri = jnp.tril(jnp.ones((tt, tt), F32), -1).astype(BF16)
    return pl.pallas_call(
        _route_body,
        name="route",
        grid=(t // tt,),
        in_specs=[pl.BlockSpec((tt, LANES), lambda i: (i, 0)),
                  pl.BlockSpec((tt, tt), lambda i: (0, 0))],
        out_specs=[pl.BlockSpec((tt, LANES), lambda i: (i, 0)),
                   pl.BlockSpec((1, LANES), lambda i: (0, 0))],
        out_shape=[jax.ShapeDtypeStruct((t, LANES), F32), jax.ShapeDtypeStruct((1, LANES), F32)],
        scratch_shapes=[pltpu.VMEM((1, LANES), F32)],
        compiler_params=_cparams(("arbitrary",)),
    )(logits, tri)


def _row_copies(n_tok, issue_one):
    def body(t8, carry):
        for i in range(SUBLANES):
            issue_one(t8 * SUBLANES + i)
        return carry
    lax.fori_loop(0, n_tok // SUBLANES, body, 0)


def _dispatch_body(pe_ref, nu_ref, dest_ref, x_ref, xs_ref, zbuf, sem, zsem):
    tt = x_ref.shape[0]
    n_blocks = xs_ref.shape[0] // MOE_BLOCK

    @pl.when(pl.program_id(0) == 0)
    def _():
        zbuf[...] = jnp.zeros_like(zbuf)

        def zero_block(row):
            return pltpu.make_async_copy(
                zbuf, xs_ref.at[pl.ds(pl.multiple_of(row, MOE_BLOCK), MOE_BLOCK)], zsem)

        def each(act):
            for e in range(N_EXPERTS):
                @pl.when(pe_ref[e] >= MOE_BLOCK)
                def _():
                    act(zero_block(pe_ref[e] - MOE_BLOCK))

            def tail(i, carry):
                act(zero_block(i * MOE_BLOCK))
                return carry
            lax.fori_loop(nu_ref[0], n_blocks, tail, 0)

        each(lambda c: c.start())
        each(lambda c: c.wait())

    def issue(t):
        for k in range(TOP_K):
            d = dest_ref[0, 0, t * TOP_K + k]
            pltpu.make_async_copy(x_ref.at[pl.ds(t, 1)], xs_ref.at[pl.ds(d, 1)], sem).start(priority=k % 2)

    _row_copies(tt, issue)
    for _ in range(TOP_K):
        pltpu.make_async_copy(x_ref, xs_ref.at[pl.ds(0, tt)], sem).wait()


def _dispatch(pad_ends, n_used, x1, dest3, n_rows):
    t = x1.shape[0]
    tt = dest3.shape[2] // TOP_K
    grid_spec = pltpu.PrefetchScalarGridSpec(
        num_scalar_prefetch=2,
        grid=(t // tt,),
        in_specs=[pl.BlockSpec((1, 1, tt * TOP_K), lambda i, pe, nu: (i, 0, 0), memory_space=pltpu.SMEM),
                  pl.BlockSpec((tt, D_MODEL), lambda i, pe, nu: (i, 0))],
        out_specs=pl.BlockSpec(memory_space=pl.ANY),
        scratch_shapes=[pltpu.VMEM((MOE_BLOCK, D_MODEL), F32),
                        pltpu.SemaphoreType.DMA(()), pltpu.SemaphoreType.DMA(())],
    )
    return pl.pallas_call(
        _dispatch_body,
        name="dispatch",
        grid_spec=grid_spec,
        out_shape=jax.ShapeDtypeStruct((n_rows, D_MODEL), F32),
        compiler_params=_cparams(("arbitrary",)),
    )(pad_ends, n_used, dest3, x1)


def _ffn_body(be_ref, nu_ref, xs_ref, g_ref, wgu_ref, bgu_ref, wd_ref, bd_ref, ys_ref, wgu_bf, wd_bf):
    i = pl.program_id(0)

    @pl.when((i == 0) | (be_ref[i] != be_ref[jnp.maximum(i - 1, 0)]))
    def _():
        wgu_bf[...] = wgu_ref[...].astype(BF16)
        wd_bf[...] = wd_ref[...].astype(BF16)

    @pl.when(i < nu_ref[0])
    def _():
        h = _rms(xs_ref[...], g_ref[...]).astype(BF16)
        y = None
        for c in range(D_FF // FFN_CHUNK):
            g_cols = slice(c * FFN_CHUNK, (c + 1) * FFN_CHUNK)
            u_cols = slice(D_FF + c * FFN_CHUNK, D_FF + (c + 1) * FFN_CHUNK)
            gate = jnp.dot(h, wgu_bf[:, g_cols], preferred_element_type=F32) + bgu_ref[:, g_cols]
            up = jnp.dot(h, wgu_bf[:, u_cols], preferred_element_type=F32) + bgu_ref[:, u_cols]
            gate = jnp.minimum(gate, SWIGLU_LIMIT)
            up = jnp.clip(up, -SWIGLU_LIMIT, SWIGLU_LIMIT)
            act = (up + 1.0) * gate * jax.nn.sigmoid(SWIGLU_ALPHA * gate)
            part = jnp.dot(act.astype(BF16), wd_bf[g_cols, :], preferred_element_type=F32)
            y = part if y is None else y + part
        ys_ref[...] = y + bd_ref[...]

    @pl.when(pl.program_id(0) >= nu_ref[0])
    def _():
        ys_ref[...] = jnp.zeros_like(ys_ref)


def _moe_ffn(block_e, n_used, xs, norm2_g, w_gu, b_gu, w_d, b_d):
    n_rows = xs.shape[0]
    n_blocks = n_rows // MOE_BLOCK
    row_blk = lambda i, be, nu: (jnp.minimum(i, nu[0] - 1), 0)
    by_e = lambda i, be, nu: (be[i], 0, 0)
    grid_spec = pltpu.PrefetchScalarGridSpec(
        num_scalar_prefetch=2,
        grid=(n_blocks,),
        in_specs=[
            pl.BlockSpec((MOE_BLOCK, D_MODEL), row_blk),
            pl.BlockSpec((1, D_MODEL), lambda i, be, nu: (0, 0)),
            pl.BlockSpec((None, D_MODEL, 2 * D_FF), by_e),
            pl.BlockSpec((None, 1, 2 * D_FF), by_e),
            pl.BlockSpec((None, D_FF, D_MODEL), by_e),
            pl.BlockSpec((None, 1, D_MODEL), by_e),
        ],
        out_specs=pl.BlockSpec((MOE_BLOCK, D_MODEL), lambda i, be, nu: (i, 0)),
        scratch_shapes=[pltpu.VMEM((D_MODEL, 2 * D_FF), BF16), pltpu.VMEM((D_FF, D_MODEL), BF16)],
    )
    return pl.pallas_call(
        _ffn_body,
        name="moe_ffn",
        grid_spec=grid_spec,
        out_shape=jax.ShapeDtypeStruct((n_rows, D_MODEL), F32),
        compiler_params=_cparams(("arbitrary",)),
    )(block_e, n_used, xs, norm2_g, w_gu, b_gu, w_d, b_d)


def _combine_body(dest_ref, x_ref, slab_ref, ys_ref, o_ref, buf, sem):
    tt = x_ref.shape[0]

    def issue(t):
        for k in range(TOP_K):
            d = dest_ref[0, 0, t * TOP_K + k]
            pltpu.make_async_copy(ys_ref.at[pl.ds(d, 1)], buf.at[k, pl.ds(t, 1)], sem).start(priority=k % 2)

    _row_copies(tt, issue)
    for k in range(TOP_K):
        pltpu.make_async_copy(ys_ref.at[pl.ds(0, tt)], buf.at[k], sem).wait()
    acc = x_ref[...]
    for k in range(TOP_K):
        acc = acc + slab_ref[:, TOP_K + k:TOP_K + k + 1] * buf[k]
    o_ref[...] = acc


def _combine(x1, slab, dest3, ys):
    t = x1.shape[0]
    tt = dest3.shape[2] // TOP_K
    return pl.pallas_call(
        _combine_body,
        name="combine",
        grid=(t // tt,),
        in_specs=[pl.BlockSpec((1, 1, tt * TOP_K), lambda i: (i, 0, 0), memory_space=pltpu.SMEM),
                  pl.BlockSpec((tt, D_MODEL), lambda i: (i, 0)),
                  pl.BlockSpec((tt, LANES), lambda i: (i, 0)),
                  pl.BlockSpec(memory_space=pl.ANY)],
        out_specs=pl.BlockSpec((tt, D_MODEL), lambda i: (i, 0)),
        out_shape=jax.ShapeDtypeStruct((t, D_MODEL), F32),
        scratch_shapes=[pltpu.VMEM((TOP_K, tt, D_MODEL), F32), pltpu.SemaphoreType.DMA(())],
        compiler_params=_cparams(("arbitrary",)),
    )(dest3, x1, slab, ys)


def _moe(x1, logits, norm2_g, w_gate_up, b_gate_up, w_down, b_down):
    t = x1.shape[0]
    n_assign = t * TOP_K
    n_blocks = -(-n_assign // MOE_BLOCK) + N_EXPERTS
    n_rows = n_blocks * MOE_BLOCK

    slab, cnt = _route(logits)
    ids = slab[:, :TOP_K].astype(jnp.int32)
    pos = slab[:, 2 * TOP_K:3 * TOP_K].astype(jnp.int32)
    counts = cnt[0, :N_EXPERTS].astype(jnp.int32)
    padded = (counts + MOE_BLOCK - 1) // MOE_BLOCK * MOE_BLOCK
    pad_ends = jnp.cumsum(padded)
    pad_starts = pad_ends - padded
    dest = pad_starts[ids] + pos
    n_used = (pad_ends[-1] // MOE_BLOCK).astype(jnp.int32)
    blk = jnp.minimum(jnp.arange(n_blocks, dtype=jnp.int32), n_used - 1)
    block_e = jnp.minimum(jnp.sum((pad_ends[None, :] <= (blk * MOE_BLOCK)[:, None]).astype(jnp.int32), axis=1),
                          N_EXPERTS - 1)

    tt = min(TT_MOVE, t)
    dest3 = dest.reshape(t // tt, 1, tt * TOP_K)
    xs = _dispatch(pad_ends.astype(jnp.int32), n_used.reshape(1), x1, dest3, n_rows)
    ys = _moe_ffn(block_e, n_used.reshape(1), xs, norm2_g,
                  w_gate_up, b_gate_up.reshape(N_EXPERTS, 1, 2 * D_FF),
                  w_down, b_down.reshape(N_EXPERTS, 1, D_MODEL))
    return _combine(x1, slab, dest3, ys)


def _layer(x, norm1_g, w_in, ssm_a_re, ssm_a_im, ssm_log_dt, ssm_b_re, ssm_b_im, ssm_c_re,
           ssm_c_im, ssm_d, w_glu, w_ssm_proj, q_norm_g, k_norm_g, attn_sinks, w_attn_proj,
           w_out, norm2_g, w_router, b_router, w_gate_up, b_gate_up, w_down, b_down):
    n_b, seq, d = x.shape
    t = n_b * seq
    x2 = x.reshape(t, d)
    u, q, kd, vd, gs, ga = _in_proj(x2, norm1_g.reshape(1, d), w_in, q_norm_g, k_norm_g)

    abar_re, abar_im, bbt_re, bbt_im = _s5_prep(ssm_a_re, ssm_a_im, ssm_log_dt, ssm_b_re, ssm_b_im)
    b_mat, c_mat, ab = _s5_matrices(abar_re, abar_im, bbt_re, bbt_im, ssm_c_re, ssm_c_im)
    y = _s5(u.reshape(n_b, seq, SSM_WIDTH), b_mat, c_mat, ab, ssm_d.reshape(1, SSM_WIDTH))

    a = _attention(q.reshape(n_b, seq, ATTN_WIDTH), kd.reshape(n_b, seq, KV_DUP),
                   vd.reshape(n_b, seq, KV_DUP), attn_sinks)

    w_r_pad = jnp.zeros((d, LANES), BF16).at[:, :N_EXPERTS].set(w_router.astype(BF16))
    b_r_pad = jnp.zeros((1, LANES), F32).at[0, :N_EXPERTS].set(b_router)
    norm2 = norm2_g.reshape(1, d)
    x1, logits = _mix(y.reshape(t, SSM_WIDTH), a.reshape(t, ATTN_WIDTH), gs, ga, x2,
                      w_glu.astype(BF16), w_ssm_proj.astype(BF16), w_attn_proj.astype(BF16),
                      w_out.astype(BF16), norm2, w_r_pad, b_r_pad)
    out = _moe(x1, logits, norm2, w_gate_up, b_gate_up, w_down, b_down)
    return out.reshape(n_b, seq, d)


def kernel(x, norm1_g, w_in, ssm_a_re, ssm_a_im, ssm_log_dt, ssm_b_re, ssm_b_im, ssm_c_re, ssm_c_im, ssm_d, w_glu, w_ssm_proj, q_norm_g, k_norm_g, attn_sinks, w_attn_proj, w_out, norm2_g, w_router, b_router, w_gate_up, b_gate_up, w_down, b_down):
    for layer in range(norm1_g.shape[0]):
        x = _layer(
            x, norm1_g[layer], w_in[layer], ssm_a_re[layer], ssm_a_im[layer], ssm_log_dt[layer],
            ssm_b_re[layer], ssm_b_im[layer], ssm_c_re[layer], ssm_c_im[layer], ssm_d[layer],
            w_glu[layer], w_ssm_proj[layer], q_norm_g[layer], k_norm_g[layer], attn_sinks[layer],
            w_attn_proj[layer], w_out[layer], norm2_g[layer], w_router[layer], b_router[layer],
            w_gate_up[layer], b_gate_up[layer], w_down[layer], b_down[layer])
    return x
```

```python
import functools

import jax
import jax.numpy as jnp
from jax import lax
from jax.experimental import pallas as pl
from jax.experimental.pallas import tpu as pltpu

D_MODEL = 1024
NORM_EPS = 1e-5
SSM_WIDTH = 1024
SSM_GROUP_CH = 16
SSM_GROUPS = SSM_WIDTH // SSM_GROUP_CH
SSM_STATE = 64
HEAD_DIM = 64
N_Q_HEADS = 16
N_KV_HEADS = 4
Q_PER_KV = N_Q_HEADS // N_KV_HEADS
ATTN_WIDTH = N_Q_HEADS * HEAD_DIM
KV_WIDTH = N_KV_HEADS * HEAD_DIM
WINDOW = 128
BLOCK = 128
N_EXPERTS = 32
TOP_K = 4
D_FF = 1024
SWIGLU_LIMIT = 7.0
SWIGLU_ALPHA = 1.702
MOE_BLOCK = 512

LANES = 128
SUBLANES = 8
VMEM_LIMIT = 56 * 1024 * 1024

S5_GROUP_BLOCK = 8
S5_CH = S5_GROUP_BLOCK * SSM_GROUP_CH
S5_ST = S5_GROUP_BLOCK * SSM_STATE
S5_CHUNK = 256
S5_PITCH = S5_CHUNK + SUBLANES

TM_IN = 256
TM_MIX = 512
TT_ROUTE = 512
TT_MOVE = 256

F32 = jnp.float32
BF16 = jnp.bfloat16


def _cparams(sem):
    return pltpu.CompilerParams(dimension_semantics=sem, vmem_limit_bytes=VMEM_LIMIT)


def _rms(x, g):
    return x * lax.rsqrt(jnp.mean(x * x, axis=-1, keepdims=True) + NORM_EPS) * g


KV_DUP = N_KV_HEADS * LANES


def _in_proj_body(x_ref, g_ref, w_ref, qg_ref, kg_ref, u_ref, q_ref, kd_ref, vd_ref, gs_ref, ga_ref):
    h = _rms(x_ref[...], g_ref[...]).astype(BF16)
    off = [0]

    def proj(width):
        lo = off[0]
        off[0] = lo + width
        return jnp.dot(h, w_ref[:, lo:lo + width], preferred_element_type=F32)

    u_ref[...] = proj(SSM_WIDTH)

    q = proj(ATTN_WIDTH)
    lo_sel = lax.broadcasted_iota(jnp.int32, (q.shape[0], LANES), 1) < HEAD_DIM
    for j in range(ATTN_WIDTH // LANES):
        qq = q[:, j * LANES:(j + 1) * LANES]
        sq = qq * qq
        ms_lo = jnp.sum(jnp.where(lo_sel, sq, 0.0), axis=-1, keepdims=True) / HEAD_DIM
        ms_hi = jnp.sum(jnp.where(lo_sel, 0.0, sq), axis=-1, keepdims=True) / HEAD_DIM
        r = jnp.where(lo_sel, lax.rsqrt(ms_lo + NORM_EPS), lax.rsqrt(ms_hi + NORM_EPS))
        q_ref[:, j * LANES:(j + 1) * LANES] = (qq * r * qg_ref[...]).astype(q_ref.dtype)

    kd = proj(KV_DUP)
    for kv in range(N_KV_HEADS):
        grp = slice(kv * LANES, (kv + 1) * LANES)
        kd_ref[:, grp] = _rms(kd[:, grp], kg_ref[...]).astype(kd_ref.dtype)
    vd_ref[...] = proj(KV_DUP).astype(vd_ref.dtype)
    gs_ref[...] = proj(D_MODEL)
    ga_ref[...] = proj(D_MODEL)


def _in_proj_weight(w_in):
    d = w_in.shape[0]
    cuts = (SSM_WIDTH, SSM_WIDTH + ATTN_WIDTH, SSM_WIDTH + ATTN_WIDTH + KV_WIDTH,
            SSM_WIDTH + ATTN_WIDTH + 2 * KV_WIDTH)
    w_uq, w_k, w_v, w_g = (w_in[:, :cuts[1]], w_in[:, cuts[1]:cuts[2]], w_in[:, cuts[2]:cuts[3]],
                           w_in[:, cuts[3]:])

    def twice(w):
        w = w.reshape(d, N_KV_HEADS, 1, HEAD_DIM)
        return jnp.concatenate([w, w], axis=2).reshape(d, KV_DUP)

    return jnp.concatenate([w_uq, twice(w_k), twice(w_v), w_g], axis=1).astype(BF16)


def _in_proj(x2, norm_g, w_in, q_norm_g, k_norm_g):
    t = x2.shape[0]
    w = _in_proj_weight(w_in)
    qg = (jnp.tile(q_norm_g, LANES // HEAD_DIM) * HEAD_DIM ** -0.5).reshape(1, LANES)
    kg = jnp.tile(k_norm_g, LANES // HEAD_DIM).reshape(1, LANES)
    outs = ((SSM_WIDTH, F32), (ATTN_WIDTH, BF16), (KV_DUP, BF16), (KV_DUP, BF16),
            (D_MODEL, F32), (D_MODEL, F32))
    const = lambda a: pl.BlockSpec(a.shape, lambda i: (0, 0))
    return pl.pallas_call(
        _in_proj_body,
        name="in_proj",
        grid=(t // TM_IN,),
        in_specs=[pl.BlockSpec((TM_IN, D_MODEL), lambda i: (i, 0)), const(norm_g), const(w),
                  const(qg), const(kg)],
        out_specs=[pl.BlockSpec((TM_IN, n), lambda i: (i, 0)) for n, _ in outs],
        out_shape=[jax.ShapeDtypeStruct((t, n), dt) for n, dt in outs],
        compiler_params=_cparams(("parallel",)),
    )(x2, norm_g, w, qg, kg)


def _s5_prep_body(are_ref, aim_ref, ldt_ref, bre_ref, bim_ref,
                  abr_ref, abi_ref, bbr_ref, bbi_ref):
    a_re = are_ref[...]
    a_im = aim_ref[...]
    dt = jnp.exp(ldt_ref[...])
    mag = jnp.exp(a_re * dt)
    abar_re = mag * jnp.cos(a_im * dt)
    abar_im = mag * jnp.sin(a_im * dt)
    den = a_re * a_re + a_im * a_im
    q_re = ((abar_re - 1.0) * a_re + abar_im * a_im) / den
    q_im = (abar_im * a_re - (abar_re - 1.0) * a_im) / den
    abr_ref[...] = abar_re
    abi_ref[...] = abar_im
    for h in range(SSM_GROUP_CH):
        bbr_ref[h] = q_re * bre_ref[h] - q_im * bim_ref[h]
        bbi_ref[h] = q_re * bim_ref[h] + q_im * bre_ref[h]


def _s5_prep(a_re, a_im, log_dt, b_re, b_im):
    g, p, hc = SSM_GROUPS, SSM_STATE, SSM_GROUP_CH
    bt_re = jnp.transpose(b_re, (2, 0, 1))
    bt_im = jnp.transpose(b_im, (2, 0, 1))
    return pl.pallas_call(
        _s5_prep_body,
        name="s5_prep",
        out_shape=[
            jax.ShapeDtypeStruct((g, p), F32),
            jax.ShapeDtypeStruct((g, p), F32),
            jax.ShapeDtypeStruct((hc, g, p), F32),
            jax.ShapeDtypeStruct((hc, g, p), F32),
        ],
    )(a_re, a_im, log_dt.reshape(g, 1), bt_re, bt_im)


def _s5_matrices(abar_re, abar_im, bbt_re, bbt_im, c_re, c_im):
    nb, gb, hc, p = SSM_GROUPS // S5_GROUP_BLOCK, S5_GROUP_BLOCK, SSM_GROUP_CH, SSM_STATE
    eye = jnp.eye(gb, dtype=F32)

    def b_block(bbt):
        t = jnp.transpose(bbt, (1, 0, 2)).reshape(nb, gb, hc, p)
        return jnp.einsum('jghp,gk->jghkp', t, eye).reshape(nb, gb * hc, gb * p)

    def c_block(c):
        t = c.reshape(nb, gb, hc, p)
        return jnp.einsum('jghp,gk->jgpkh', t, eye).reshape(nb, gb * p, gb * hc)

    b_mat = jnp.concatenate([b_block(bbt_re), b_block(bbt_im)], axis=2).astype(BF16)
    c_mat = jnp.concatenate([c_block(c_re), c_block(-c_im)], axis=1).astype(BF16)
    ab = jnp.concatenate([abar_re.reshape(nb, 1, gb * p), abar_im.reshape(nb, 1, gb * p)], axis=2)
    ab = jnp.broadcast_to(ab, (nb, SUBLANES, 2 * gb * p))
    return b_mat, c_mat, ab


def _s5_body(u_ref, b_ref, c_ref, ab_ref, d_ref, o_ref, ubuf, utm, sbuf, state, ytm):
    n_b = u_ref.shape[0]
    chunk = u_ref.shape[1]

    @pl.when(pl.program_id(1) == 0)
    def _():
        state[...] = jnp.zeros_like(state)

    for b in range(n_b):
        ubuf[b * S5_PITCH:b * S5_PITCH + chunk, :] = u_ref[b]

    def to_time_major(l8, carry):
        for i in range(SUBLANES):
            l = l8 * SUBLANES + i
            row = pl.multiple_of(l * n_b, SUBLANES)
            utm[pl.ds(row, n_b), :] = ubuf[pl.ds(l, n_b, stride=S5_PITCH), :]
        return carry

    lax.fori_loop(0, chunk // SUBLANES, to_time_major, 0)

    sbuf[...] = jnp.dot(utm[...].astype(BF16), b_ref[...], preferred_element_type=F32)

    a_re = ab_ref[:, :S5_ST]
    a_im = ab_ref[:, S5_ST:]

    def scan(l8, carry):
        s_re, s_im = carry
        for i in range(SUBLANES):
            row = pl.multiple_of((l8 * SUBLANES + i) * n_b, SUBLANES)
            n_re = a_re * s_re - a_im * s_im + sbuf[pl.ds(row, n_b), :S5_ST]
            n_im = a_re * s_im + a_im * s_re + sbuf[pl.ds(row, n_b), S5_ST:]
            sbuf[pl.ds(row, n_b), :S5_ST] = n_re
            sbuf[pl.ds(row, n_b), S5_ST:] = n_im
            s_re, s_im = n_re, n_im
        return s_re, s_im

    s_re, s_im = lax.fori_loop(0, chunk // SUBLANES, scan, (state[:, :S5_ST], state[:, S5_ST:]))
    state[:, :S5_ST] = s_re
    state[:, S5_ST:] = s_im

    y = jnp.dot(sbuf[...].astype(BF16), c_ref[...], preferred_element_type=F32)
    y = y + d_ref[...] * utm[...]
    ytm[...] = jax.nn.gelu(y)
    for b in range(n_b):
        o_ref[b] = ytm[pl.ds(b, chunk, stride=n_b), :].astype(o_ref.dtype)


def _s5(u3, b_mat, c_mat, ab, d_flat):
    n_b, seq, _ = u3.shape
    assert n_b == SUBLANES, "the scan keeps one sequence per sublane"
    chunk = min(S5_CHUNK, seq)
    nb = SSM_GROUPS // S5_GROUP_BLOCK
    return pl.pallas_call(
        _s5_body,
        name="s5_scan",
        grid=(nb, seq // chunk),
        in_specs=[
            pl.BlockSpec((n_b, chunk, S5_CH), lambda j, c: (0, c, j)),
            pl.BlockSpec((None, S5_CH, 2 * S5_ST), lambda j, c: (j, 0, 0)),
            pl.BlockSpec((None, 2 * S5_ST, S5_CH), lambda j, c: (j, 0, 0)),
            pl.BlockSpec((None, SUBLANES, 2 * S5_ST), lambda j, c: (j, 0, 0)),
            pl.BlockSpec((1, S5_CH), lambda j, c: (0, j)),
        ],
        out_specs=pl.BlockSpec((n_b, chunk, S5_CH), lambda j, c: (0, c, j)),
        out_shape=jax.ShapeDtypeStruct((n_b, seq, SSM_WIDTH), BF16),
        scratch_shapes=[
            pltpu.VMEM((n_b * S5_PITCH, S5_CH), F32),
            pltpu.VMEM((chunk * n_b, S5_CH), F32),
            pltpu.VMEM((chunk * n_b, 2 * S5_ST), F32),
            pltpu.VMEM((n_b, 2 * S5_ST), F32),
            pltpu.VMEM((chunk * n_b, S5_CH), F32),
        ],
        compiler_params=_cparams(("parallel", "arbitrary")),
    )(u3, b_mat, c_mat, ab, d_flat)


def _attn_body(sinks_ref, q_ref, kp_ref, kc_ref, vp_ref, vc_ref, bias_ref, half_ref, o_ref):
    lo_mask = half_ref[0:1, :]
    hi_mask = half_ref[1:2, :]
    lo_sel = lax.broadcasted_iota(jnp.int32, (BLOCK, LANES), 1) < HEAD_DIM
    for kv in range(N_KV_HEADS):
        grp = slice(kv * LANES, (kv + 1) * LANES)
        kw = jnp.concatenate([kp_ref[:, grp], kc_ref[:, grp]], axis=0)
        vw = jnp.concatenate([vp_ref[:, grp], vc_ref[:, grp]], axis=0)
        qs = []
        for pair in range(Q_PER_KV // 2):
            lanes = slice((2 * kv + pair) * LANES, (2 * kv + pair + 1) * LANES)
            qp = q_ref[:, lanes]
            qs += [qp * lo_mask, qp * hi_mask]
        q_st = jnp.concatenate(qs, axis=0)
        s = lax.dot_general(q_st, kw, (((1,), (1,)), ((), ())), preferred_element_type=F32)
        s = s + bias_ref[kv]
        ps = []
        for g in range(Q_PER_KV):
            sg = s[g * BLOCK:(g + 1) * BLOCK]
            sink = sinks_ref[kv * Q_PER_KV + g]
            m = jnp.maximum(jnp.max(sg, axis=-1, keepdims=True), sink)
            p = jnp.exp(sg - m)
            den = jnp.sum(p, axis=-1, keepdims=True) + jnp.exp(sink - m)
            ps.append((p / den).astype(BF16))
        p_st = jnp.concatenate(ps, axis=0)
        o = jnp.dot(p_st, vw, preferred_element_type=F32)
        for pair in range(Q_PER_KV // 2):
            even = o[(2 * pair) * BLOCK:(2 * pair + 1) * BLOCK]
            odd = o[(2 * pair + 1) * BLOCK:(2 * pair + 2) * BLOCK]
            lanes = slice((2 * kv + pair) * LANES, (2 * kv + pair + 1) * LANES)
            o_ref[:, lanes] = jnp.where(lo_sel, even, odd).astype(o_ref.dtype)


def _attn_bias():
    h = jnp.arange(1, N_Q_HEADS + 1, dtype=F32)
    slopes = jnp.exp2(-8.0 * h / N_Q_HEADS)
    qi = jnp.arange(BLOCK)[:, None]
    kj = jnp.arange(2 * BLOCK)[None, :]
    dist = qi - kj + BLOCK
    band = (dist >= 0) & (dist < WINDOW)
    bias = -slopes[:, None, None] * dist.astype(F32)
    later = jnp.where(band[None], bias, -jnp.inf)
    first = jnp.where((band & (kj >= BLOCK))[None], bias, -jnp.inf)
    both = jnp.stack([first, later])
    return both.reshape(2, N_KV_HEADS, Q_PER_KV * BLOCK, 2 * BLOCK)


def _attention(q3, kd3, vd3, sinks):
    n_b, seq, _ = q3.shape
    n_blk = seq // BLOCK
    kv_dup = N_KV_HEADS * LANES
    prev = lambda b, i: (b, jnp.maximum(i - 1, 0), 0)
    cur = lambda b, i: (b, i, 0)
    lane = jnp.arange(LANES)
    half = jnp.stack([lane < HEAD_DIM, lane >= HEAD_DIM]).astype(BF16)
    return pl.pallas_call(
        _attn_body,
        name="swa_attention",
        grid=(n_b, n_blk),
        in_specs=[
            pl.BlockSpec(memory_space=pltpu.SMEM),
            pl.BlockSpec((None, BLOCK, ATTN_WIDTH), cur),
            pl.BlockSpec((None, BLOCK, kv_dup), prev),
            pl.BlockSpec((None, BLOCK, kv_dup), cur),
            pl.BlockSpec((None, BLOCK, kv_dup), prev),
            pl.BlockSpec((None, BLOCK, kv_dup), cur),
            pl.BlockSpec((None, N_KV_HEADS, Q_PER_KV * BLOCK, 2 * BLOCK),
                         lambda b, i: (jnp.minimum(i, 1), 0, 0, 0)),
            pl.BlockSpec((2, LANES), lambda b, i: (0, 0)),
        ],
        out_specs=pl.BlockSpec((None, BLOCK, ATTN_WIDTH), cur),
        out_shape=jax.ShapeDtypeStruct((n_b, seq, ATTN_WIDTH), BF16),
        compiler_params=_cparams(("parallel", "arbitrary")),
    )(sinks, q3, kd3, kd3, vd3, vd3, _attn_bias(), half)


ROW_TILE = D_MODEL // LANES
assert ROW_TILE == SUBLANES


def _store_row_tiles(ref, x):
    for i in range(x.shape[0] // SUBLANES):
        for j in range(ROW_TILE):
            ref[pl.ds(i * SUBLANES * ROW_TILE + j, SUBLANES, stride=ROW_TILE), :] = (
                x[i * SUBLANES:(i + 1) * SUBLANES, j * LANES:(j + 1) * LANES])


def _load_row_tiles(ref, n):
    cols = []
    for j in range(ROW_TILE):
        cols.append(jnp.concatenate(
            [ref[pl.ds(i * SUBLANES * ROW_TILE + j, SUBLANES, stride=ROW_TILE), :]
             for i in range(n // SUBLANES)], axis=0))
    return jnp.concatenate(cols, axis=1)


def _mix_body(y_ref, a_ref, gs_ref, ga_ref, x_ref, wglu_ref, wsp_ref, wap_ref, wo_ref,
              g2_ref, wr_ref, br_ref, x1_ref, lg_ref):
    z = jnp.dot(y_ref[...], wglu_ref[...], preferred_element_type=F32)
    glu = z[:, :SSM_WIDTH] * jax.nn.sigmoid(z[:, SSM_WIDTH:])
    y_ssm = jnp.dot(glu.astype(BF16), wsp_ref[...], preferred_element_type=F32)
    y_attn = jnp.dot(a_ref[...], wap_ref[...], preferred_element_type=F32)
    mixed = jax.nn.sigmoid(gs_ref[...]) * y_ssm + jax.nn.sigmoid(ga_ref[...]) * y_attn
    x1 = x_ref[...] + jnp.dot(mixed.astype(BF16), wo_ref[...], preferred_element_type=F32)
    _store_row_tiles(x1_ref, x1)
    h2 = _rms(x1, g2_ref[...]).astype(BF16)
    lg_ref[...] = jnp.dot(h2, wr_ref[...], preferred_element_type=F32) + br_ref[...]


def _mix(y2, a2, gs, ga, x2, w_glu, w_sp, w_ap, w_o, norm2_g, w_r_pad, b_r_pad):
    t = x2.shape[0]
    tm = min(TM_MIX, t)
    tok = lambda w: pl.BlockSpec((tm, w), lambda i: (i, 0))
    full = lambda a: pl.BlockSpec(a.shape, lambda i: (0, 0))
    return pl.pallas_call(
        _mix_body,
        name="mix",
        grid=(t // tm,),
        in_specs=[tok(SSM_WIDTH), tok(ATTN_WIDTH), tok(D_MODEL), tok(D_MODEL), tok(D_MODEL),
                  full(w_glu), full(w_sp), full(w_ap), full(w_o), full(norm2_g),
                  full(w_r_pad), full(b_r_pad)],
        out_specs=[pl.BlockSpec((tm * ROW_TILE, LANES), lambda i: (i, 0)), tok(LANES)],
        out_shape=[jax.ShapeDtypeStruct((t * ROW_TILE, LANES), F32),
                   jax.ShapeDtypeStruct((t, LANES), F32)],
        compiler_params=_cparams(("parallel",)),
    )(y2, a2, gs, ga, x2, w_glu, w_sp, w_ap, w_o, norm2_g, w_r_pad, b_r_pad)


def _route_body(lg_ref, tri_ref, slab_ref, cnt_ref, base):
    @pl.when(pl.program_id(0) == 0)
    def _():
        base[...] = jnp.zeros_like(base)

    tt = lg_ref.shape[0]
    lane = lax.broadcasted_iota(jnp.int32, (tt, LANES), 1)
    l = jnp.where(lane < N_EXPERTS, lg_ref[...], -jnp.inf)
    tops, hots = [], []
    for _ in range(TOP_K):
        m = jnp.max(l, axis=-1, keepdims=True)
        idx = jnp.min(jnp.where(l == m, lane, LANES), axis=-1, keepdims=True)
        hot = lane == idx
        tops.append((m, idx))
        hots.append(hot)
        l = jnp.where(hot, -jnp.inf, l)
    es = [jnp.exp(m - tops[0][0]) for m, _ in tops]
    den = es[0] + es[1] + es[2] + es[3]
    member = jnp.zeros((tt, LANES), F32)
    for hot in hots:
        member = member + jnp.where(hot, 1.0, 0.0)
    before = jnp.dot(tri_ref[...], member.astype(BF16), preferred_element_type=F32) + base[...]
    slab = jnp.zeros((tt, LANES), F32)
    for k in range(TOP_K):
        pos = jnp.sum(jnp.where(hots[k], before, 0.0), axis=-1, keepdims=True)
        slab = jnp.where(lane == k, tops[k][1].astype(F32), slab)
        slab = jnp.where(lane == TOP_K + k, es[k] / den, slab)
        slab = jnp.where(lane == 2 * TOP_K + k, pos, slab)
    slab_ref[...] = slab
    base[...] = base[...] + jnp.sum(member, axis=0, keepdims=True)
    cnt_ref[...] = base[...]


def _route(logits):
    t = logits.shape[0]
    tt = min(TT_ROUTE, t)
    tri = jnp.tril(jnp.ones((tt, tt), F32), -1).astype(BF16)
    return pl.pallas_call(
        _route_body,
        name="route",
        grid=(t // tt,),
        in_specs=[pl.BlockSpec((tt, LANES), lambda i: (i, 0)),
                  pl.BlockSpec((tt, tt), lambda i: (0, 0))],
        out_specs=[pl.BlockSpec((tt, LANES), lambda i: (i, 0)),
                   pl.BlockSpec((1, LANES), lambda i: (0, 0))],
        out_shape=[jax.ShapeDtypeStruct((t, LANES), F32), jax.ShapeDtypeStruct((1, LANES), F32)],
        scratch_shapes=[pltpu.VMEM((1, LANES), F32)],
        compiler_params=_cparams(("arbitrary",)),
    )(logits, tri)


def _row_copies(n_tok, issue_one):
    def body(t8, carry):
        for i in range(SUBLANES):
            issue_one(t8 * SUBLANES + i)
        return carry
    lax.fori_loop(0, n_tok // SUBLANES, body, 0)


def _row_tile(ref, row):
    return ref.at[pl.ds(pl.multiple_of(row * ROW_TILE, ROW_TILE), ROW_TILE)]


def _dispatch_body(pe_ref, nu_ref, dest_ref, x_ref, xs_ref, zbuf, sem, zsem):
    tt = x_ref.shape[0] // ROW_TILE
    blk_rows = MOE_BLOCK * ROW_TILE
    n_blocks = xs_ref.shape[0] // blk_rows

    @pl.when(pl.program_id(0) == 0)
    def _():
        zbuf[...] = jnp.zeros_like(zbuf)

        def zero_block(row):
            return pltpu.make_async_copy(
                zbuf, xs_ref.at[pl.ds(pl.multiple_of(row * ROW_TILE, blk_rows), blk_rows)], zsem)

        def each(act):
            for e in range(N_EXPERTS):
                @pl.when(pe_ref[e] >= MOE_BLOCK)
                def _():
                    act(zero_block(pe_ref[e] - MOE_BLOCK))

            def tail(i, carry):
                act(zero_block(i * MOE_BLOCK))
                return carry
            lax.fori_loop(nu_ref[0], n_blocks, tail, 0)

        each(lambda c: c.start())
        each(lambda c: c.wait())

    def issue(t):
        for k in range(TOP_K):
            d = dest_ref[0, 0, t * TOP_K + k]
            pltpu.make_async_copy(_row_tile(x_ref, t), _row_tile(xs_ref, d), sem).start(priority=k % 2)

    _row_copies(tt, issue)
    for _ in range(TOP_K):
        pltpu.make_async_copy(x_ref, xs_ref.at[pl.ds(0, tt * ROW_TILE)], sem).wait()


def _dispatch(pad_ends, n_used, x1r, dest3, n_rows):
    t = x1r.shape[0] // ROW_TILE
    tt = dest3.shape[2] // TOP_K
    grid_spec = pltpu.PrefetchScalarGridSpec(
        num_scalar_prefetch=2,
        grid=(t // tt,),
        in_specs=[pl.BlockSpec((1, 1, tt * TOP_K), lambda i, pe, nu: (i, 0, 0), memory_space=pltpu.SMEM),
                  pl.BlockSpec((tt * ROW_TILE, LANES), lambda i, pe, nu: (i, 0))],
        out_specs=pl.BlockSpec(memory_space=pl.ANY),
        scratch_shapes=[pltpu.VMEM((MOE_BLOCK * ROW_TILE, LANES), F32),
                        pltpu.SemaphoreType.DMA(()), pltpu.SemaphoreType.DMA(())],
    )
    return pl.pallas_call(
        _dispatch_body,
        name="dispatch",
        grid_spec=grid_spec,
        out_shape=jax.ShapeDtypeStruct((n_rows * ROW_TILE, LANES), F32),
        compiler_params=_cparams(("arbitrary",)),
    )(pad_ends, n_used, dest3, x1r)


def _ffn_body(be_ref, nu_ref, xs_ref, g_ref, wgu_ref, bgu_ref, wd_ref, bd_ref, ys_ref, wgu_bf, wd_bf):
    i = pl.program_id(0)

    @pl.when((i == 0) | (be_ref[i] != be_ref[jnp.maximum(i - 1, 0)]))
    def _():
        wgu_bf[...] = wgu_ref[...].astype(BF16)
        wd_bf[...] = wd_ref[...].astype(BF16)

    @pl.when(i < nu_ref[0])
    def _():
        h = _rms(_load_row_tiles(xs_ref, MOE_BLOCK), g_ref[...]).astype(BF16)
        gu = jnp.dot(h, wgu_bf[...], preferred_element_type=F32) + bgu_ref[...]
        gate = jnp.minimum(gu[:, :D_FF], SWIGLU_LIMIT)
        up = jnp.clip(gu[:, D_FF:], -SWIGLU_LIMIT, SWIGLU_LIMIT)
        act = (up + 1.0) * gate * jax.nn.sigmoid(SWIGLU_ALPHA * gate)
        y = jnp.dot(act.astype(BF16), wd_bf[...], preferred_element_type=F32) + bd_ref[...]
        _store_row_tiles(ys_ref, y)

    @pl.when(pl.program_id(0) >= nu_ref[0])
    def _():
        ys_ref[...] = jnp.zeros_like(ys_ref)


def _moe_ffn(block_e, n_used, xs, norm2_g, w_gu, b_gu, w_d, b_d):
    blk_rows = MOE_BLOCK * ROW_TILE
    n_blocks = xs.shape[0] // blk_rows
    row_blk = lambda i, be, nu: (jnp.minimum(i, nu[0] - 1), 0)
    by_e = lambda i, be, nu: (be[i], 0, 0)
    grid_spec = pltpu.PrefetchScalarGridSpec(
        num_scalar_prefetch=2,
        grid=(n_blocks,),
        in_specs=[
            pl.BlockSpec((blk_rows, LANES), row_blk),
            pl.BlockSpec((1, D_MODEL), lambda i, be, nu: (0, 0)),
            pl.BlockSpec((None, D_MODEL, 2 * D_FF), by_e),
            pl.BlockSpec((None, 1, 2 * D_FF), by_e),
            pl.BlockSpec((None, D_FF, D_MODEL), by_e),
            pl.BlockSpec((None, 1, D_MODEL), by_e),
        ],
        out_specs=pl.BlockSpec((blk_rows, LANES), lambda i, be, nu: (i, 0)),
        scratch_shapes=[pltpu.VMEM((D_MODEL, 2 * D_FF), BF16), pltpu.VMEM((D_FF, D_MODEL), BF16)],
    )
    return pl.pallas_call(
        _ffn_body,
        name="moe_ffn",
        grid_spec=grid_spec,
        out_shape=jax.ShapeDtypeStruct(xs.shape, F32),
        compiler_params=_cparams(("arbitrary",)),
    )(block_e, n_used, xs, norm2_g, w_gu, b_gu, w_d, b_d)


def _combine_body(dest_ref, x_ref, slab_ref, ys_ref, o_ref, buf, sem):
    tt = o_ref.shape[0]

    def issue(t):
        for k in range(TOP_K):
            d = dest_ref[0, 0, t * TOP_K + k]
            pltpu.make_async_copy(_row_tile(ys_ref, d), _row_tile(buf.at[k], t), sem).start(priority=k % 2)

    _row_copies(tt, issue)
    for k in range(TOP_K):
        pltpu.make_async_copy(ys_ref.at[pl.ds(0, tt * ROW_TILE)], buf.at[k], sem).wait()
    acc = _load_row_tiles(x_ref, tt)
    for k in range(TOP_K):
        acc = acc + slab_ref[:, TOP_K + k:TOP_K + k + 1] * _load_row_tiles(buf.at[k], tt)
    o_ref[...] = acc


def _combine(x1r, slab, dest3, ys):
    t = x1r.shape[0] // ROW_TILE
    tt = dest3.shape[2] // TOP_K
    return pl.pallas_call(
        _combine_body,
        name="combine",
        grid=(t // tt,),
        in_specs=[pl.BlockSpec((1, 1, tt * TOP_K), lambda i: (i, 0, 0), memory_space=pltpu.SMEM),
                  pl.BlockSpec((tt * ROW_TILE, LANES), lambda i: (i, 0)),
                  pl.BlockSpec((tt, LANES), lambda i: (i, 0)),
                  pl.BlockSpec(memory_space=pl.ANY)],
        out_specs=pl.BlockSpec((tt, D_MODEL), lambda i: (i, 0)),
        out_shape=jax.ShapeDtypeStruct((t, D_MODEL), F32),
        scratch_shapes=[pltpu.VMEM((TOP_K, tt * ROW_TILE, LANES), F32), pltpu.SemaphoreType.DMA(())],
        compiler_params=_cparams(("arbitrary",)),
    )(dest3, x1r, slab, ys)


def _moe(x1, logits, norm2_g, w_gate_up, b_gate_up, w_down, b_down):
    t = x1.shape[0] // ROW_TILE
    n_assign = t * TOP_K
    n_blocks = -(-n_assign // MOE_BLOCK) + N_EXPERTS
    n_rows = n_blocks * MOE_BLOCK

    slab, cnt = _route(logits)
    ids = slab[:, :TOP_K].astype(jnp.int32)
    pos = slab[:, 2 * TOP_K:3 * TOP_K].astype(jnp.int32)
    counts = cnt[0, :N_EXPERTS].astype(jnp.int32)
    padded = (counts + MOE_BLOCK - 1) // MOE_BLOCK * MOE_BLOCK
    pad_ends = jnp.cumsum(padded)
    pad_starts = pad_ends - padded
    dest = pad_starts[ids] + pos
    n_used = (pad_ends[-1] // MOE_BLOCK).astype(jnp.int32)
    blk = jnp.minimum(jnp.arange(n_blocks, dtype=jnp.int32), n_used - 1)
    block_e = jnp.minimum(jnp.sum((pad_ends[None, :] <= (blk * MOE_BLOCK)[:, None]).astype(jnp.int32), axis=1),
                          N_EXPERTS - 1)

    tt = min(TT_MOVE, t)
    dest3 = dest.reshape(t // tt, 1, tt * TOP_K)
    xs = _dispatch(pad_ends.astype(jnp.int32), n_used.reshape(1), x1, dest3, n_rows)
    ys = _moe_ffn(block_e, n_used.reshape(1), xs, norm2_g,
                  w_gate_up, b_gate_up.reshape(N_EXPERTS, 1, 2 * D_FF),
                  w_down, b_down.reshape(N_EXPERTS, 1, D_MODEL))
    return _combine(x1, slab, dest3, ys)


def _layer(x, norm1_g, w_in, ssm_a_re, ssm_a_im, ssm_log_dt, ssm_b_re, ssm_b_im, ssm_c_re,
           ssm_c_im, ssm_d, w_glu, w_ssm_proj, q_norm_g, k_norm_g, attn_sinks, w_attn_proj,
           w_out, norm2_g, w_router, b_router, w_gate_up, b_gate_up, w_down, b_down):
    n_b, seq, d = x.shape
    t = n_b * seq
    x2 = x.reshape(t, d)
    u, q, kd, vd, gs, ga = _in_proj(x2, norm1_g.reshape(1, d), w_in, q_norm_g, k_norm_g)

    abar_re, abar_im, bbt_re, bbt_im = _s5_prep(ssm_a_re, ssm_a_im, ssm_log_dt, ssm_b_re, ssm_b_im)
    b_mat, c_mat, ab = _s5_matrices(abar_re, abar_im, bbt_re, bbt_im, ssm_c_re, ssm_c_im)
    y = _s5(u.reshape(n_b, seq, SSM_WIDTH), b_mat, c_mat, ab, ssm_d.reshape(1, SSM_WIDTH))

    a = _attention(q.reshape(n_b, seq, ATTN_WIDTH), kd.reshape(n_b, seq, KV_DUP),
                   vd.reshape(n_b, seq, KV_DUP), attn_sinks)

    w_r_pad = jnp.zeros((d, LANES), BF16).at[:, :N_EXPERTS].set(w_router.astype(BF16))
    b_r_pad = jnp.zeros((1, LANES), F32).at[0, :N_EXPERTS].set(b_router)
    norm2 = norm2_g.reshape(1, d)
    x1, logits = _mix(y.reshape(t, SSM_WIDTH), a.reshape(t, ATTN_WIDTH), gs, ga, x2,
                      w_glu.astype(BF16), w_ssm_proj.astype(BF16), w_attn_proj.astype(BF16),
                      w_out.astype(BF16), norm2, w_r_pad, b_r_pad)
    out = _moe(x1, logits, norm2, w_gate_up, b_gate_up, w_down, b_down)
    return out.reshape(n_b, seq, d)


def kernel(x, norm1_g, w_in, ssm_a_re, ssm_a_im, ssm_log_dt, ssm_b_re, ssm_b_im, ssm_c_re, ssm_c_im, ssm_d, w_glu, w_ssm_proj, q_norm_g, k_norm_g, attn_sinks, w_attn_proj, w_out, norm2_g, w_router, b_router, w_gate_up, b_gate_up, w_down, b_down):
    for layer in range(norm1_g.shape[0]):
        x = _layer(
            x, norm1_g[layer], w_in[layer], ssm_a_re[layer], ssm_a_im[layer], ssm_log_dt[layer],
            ssm_b_re[layer], ssm_b_im[layer], ssm_c_re[layer], ssm_c_im[layer], ssm_d[layer],
            w_glu[layer], w_ssm_proj[layer], q_norm_g[layer], k_norm_g[layer], attn_sinks[layer],
            w_attn_proj[layer], w_out[layer], norm2_g[layer], w_router[layer], b_router[layer],
            w_gate_up[layer], b_gate_up[layer], w_down[layer], b_down[layer])
    return x
```

```python
import functools

import jax
import jax.numpy as jnp
from jax import lax
from jax.experimental import pallas as pl
from jax.experimental.pallas import tpu as pltpu

D_MODEL = 1024
NORM_EPS = 1e-5
SSM_WIDTH = 1024
SSM_GROUP_CH = 16
SSM_GROUPS = SSM_WIDTH // SSM_GROUP_CH
SSM_STATE = 64
HEAD_DIM = 64
N_Q_HEADS = 16
N_KV_HEADS = 4
Q_PER_KV = N_Q_HEADS // N_KV_HEADS
ATTN_WIDTH = N_Q_HEADS * HEAD_DIM
KV_WIDTH = N_KV_HEADS * HEAD_DIM
WINDOW = 128
BLOCK = 128
N_EXPERTS = 32
TOP_K = 4
D_FF = 1024
SWIGLU_LIMIT = 7.0
SWIGLU_ALPHA = 1.702
MOE_BLOCK = 512

LANES = 128
SUBLANES = 8
VMEM_LIMIT = 56 * 1024 * 1024

S5_GROUP_BLOCK = 8
S5_CH = S5_GROUP_BLOCK * SSM_GROUP_CH
S5_ST = S5_GROUP_BLOCK * SSM_STATE
S5_CHUNK = 256
S5_PITCH = S5_CHUNK + SUBLANES

TM_IN = 256
TM_MIX = 512
TT_ROUTE = 512
TT_MOVE = 256

F32 = jnp.float32
BF16 = jnp.bfloat16


def _cparams(sem):
    return pltpu.CompilerParams(dimension_semantics=sem, vmem_limit_bytes=VMEM_LIMIT)


def _rms(x, g):
    return x * lax.rsqrt(jnp.mean(x * x, axis=-1, keepdims=True) + NORM_EPS) * g


KV_DUP = N_KV_HEADS * LANES


def _in_proj_body(x_ref, g_ref, w_ref, qg_ref, kg_ref, u_ref, q_ref, kd_ref, vd_ref, gs_ref, ga_ref):
    h = _rms(x_ref[...], g_ref[...]).astype(BF16)
    off = [0]

    def proj(width):
        lo = off[0]
        off[0] = lo + width
        return jnp.dot(h, w_ref[:, lo:lo + width], preferred_element_type=F32)

    u_ref[...] = proj(SSM_WIDTH)

    q = proj(ATTN_WIDTH)
    lo_sel = lax.broadcasted_iota(jnp.int32, (q.shape[0], LANES), 1) < HEAD_DIM
    for j in range(ATTN_WIDTH // LANES):
        qq = q[:, j * LANES:(j + 1) * LANES]
        sq = qq * qq
        ms_lo = jnp.sum(jnp.where(lo_sel, sq, 0.0), axis=-1, keepdims=True) / HEAD_DIM
        ms_hi = jnp.sum(jnp.where(lo_sel, 0.0, sq), axis=-1, keepdims=True) / HEAD_DIM
        r = jnp.where(lo_sel, lax.rsqrt(ms_lo + NORM_EPS), lax.rsqrt(ms_hi + NORM_EPS))
        q_ref[:, j * LANES:(j + 1) * LANES] = (qq * r * qg_ref[...]).astype(q_ref.dtype)

    kd = proj(KV_DUP)
    for kv in range(N_KV_HEADS):
        grp = slice(kv * LANES, (kv + 1) * LANES)
        kd_ref[:, grp] = _rms(kd[:, grp], kg_ref[...]).astype(kd_ref.dtype)
    vd_ref[...] = proj(KV_DUP).astype(vd_ref.dtype)
    gs_ref[...] = proj(D_MODEL)
    ga_ref[...] = proj(D_MODEL)


def _in_proj_weight(w_in):
    d = w_in.shape[0]
    cuts = (SSM_WIDTH, SSM_WIDTH + ATTN_WIDTH, SSM_WIDTH + ATTN_WIDTH + KV_WIDTH,
            SSM_WIDTH + ATTN_WIDTH + 2 * KV_WIDTH)
    w_uq, w_k, w_v, w_g = (w_in[:, :cuts[1]], w_in[:, cuts[1]:cuts[2]], w_in[:, cuts[2]:cuts[3]],
                           w_in[:, cuts[3]:])

    def twice(w):
        w = w.reshape(d, N_KV_HEADS, 1, HEAD_DIM)
        return jnp.concatenate([w, w], axis=2).reshape(d, KV_DUP)

    return jnp.concatenate([w_uq, twice(w_k), twice(w_v), w_g], axis=1).astype(BF16)


def _in_proj(x2, norm_g, w_in, q_norm_g, k_norm_g):
    t = x2.shape[0]
    w = _in_proj_weight(w_in)
    qg = (jnp.tile(q_norm_g, LANES // HEAD_DIM) * HEAD_DIM ** -0.5).reshape(1, LANES)
    kg = jnp.tile(k_norm_g, LANES // HEAD_DIM).reshape(1, LANES)
    outs = ((SSM_WIDTH, F32), (ATTN_WIDTH, BF16), (KV_DUP, BF16), (KV_DUP, BF16),
            (D_MODEL, F32), (D_MODEL, F32))
    const = lambda a: pl.BlockSpec(a.shape, lambda i: (0, 0))
    return pl.pallas_call(
        _in_proj_body,
        name="in_proj",
        grid=(t // TM_IN,),
        in_specs=[pl.BlockSpec((TM_IN, D_MODEL), lambda i: (i, 0)), const(norm_g), const(w),
                  const(qg), const(kg)],
        out_specs=[pl.BlockSpec((TM_IN, n), lambda i: (i, 0)) for n, _ in outs],
        out_shape=[jax.ShapeDtypeStruct((t, n), dt) for n, dt in outs],
        compiler_params=_cparams(("parallel",)),
    )(x2, norm_g, w, qg, kg)


def _s5_prep_body(are_ref, aim_ref, ldt_ref, bre_ref, bim_ref,
                  abr_ref, abi_ref, bbr_ref, bbi_ref):
    a_re = are_ref[...]
    a_im = aim_ref[...]
    dt = jnp.exp(ldt_ref[...])
    mag = jnp.exp(a_re * dt)
    abar_re = mag * jnp.cos(a_im * dt)
    abar_im = mag * jnp.sin(a_im * dt)
    den = a_re * a_re + a_im * a_im
    q_re = ((abar_re - 1.0) * a_re + abar_im * a_im) / den
    q_im = (abar_im * a_re - (abar_re - 1.0) * a_im) / den
    abr_ref[...] = abar_re
    abi_ref[...] = abar_im
    for h in range(SSM_GROUP_CH):
        bbr_ref[h] = q_re * bre_ref[h] - q_im * bim_ref[h]
        bbi_ref[h] = q_re * bim_ref[h] + q_im * bre_ref[h]


def _s5_prep(a_re, a_im, log_dt, b_re, b_im):
    g, p, hc = SSM_GROUPS, SSM_STATE, SSM_GROUP_CH
    bt_re = jnp.transpose(b_re, (2, 0, 1))
    bt_im = jnp.transpose(b_im, (2, 0, 1))
    return pl.pallas_call(
        _s5_prep_body,
        name="s5_prep",
        out_shape=[
            jax.ShapeDtypeStruct((g, p), F32),
            jax.ShapeDtypeStruct((g, p), F32),
            jax.ShapeDtypeStruct((hc, g, p), F32),
            jax.ShapeDtypeStruct((hc, g, p), F32),
        ],
    )(a_re, a_im, log_dt.reshape(g, 1), bt_re, bt_im)


def _s5_matrices(abar_re, abar_im, bbt_re, bbt_im, c_re, c_im):
    nb, gb, hc, p = SSM_GROUPS // S5_GROUP_BLOCK, S5_GROUP_BLOCK, SSM_GROUP_CH, SSM_STATE
    eye = jnp.eye(gb, dtype=F32)

    def b_block(bbt):
        t = jnp.transpose(bbt, (1, 0, 2)).reshape(nb, gb, hc, p)
        return jnp.einsum('jghp,gk->jghkp', t, eye).reshape(nb, gb * hc, gb * p)

    def c_block(c):
        t = c.reshape(nb, gb, hc, p)
        return jnp.einsum('jghp,gk->jgpkh', t, eye).reshape(nb, gb * p, gb * hc)

    b_mat = jnp.concatenate([b_block(bbt_re), b_block(bbt_im)], axis=2).astype(BF16)
    c_mat = jnp.concatenate([c_block(c_re), c_block(-c_im)], axis=1).astype(BF16)
    ab = jnp.concatenate([abar_re.reshape(nb, 1, gb * p), abar_im.reshape(nb, 1, gb * p)], axis=2)
    ab = jnp.broadcast_to(ab, (nb, SUBLANES, 2 * gb * p))
    return b_mat, c_mat, ab


def _s5_body(u_ref, b_ref, c_ref, ab_ref, d_ref, o_ref, ubuf, utm, sbuf, state, ytm):
    n_b = u_ref.shape[0]
    chunk = u_ref.shape[1]

    @pl.when(pl.program_id(1) == 0)
    def _():
        state[...] = jnp.zeros_like(state)

    for b in range(n_b):
        ubuf[b * S5_PITCH:b * S5_PITCH + chunk, :] = u_ref[b]

    def to_time_major(l8, carry):
        for i in range(SUBLANES):
            l = l8 * SUBLANES + i
            row = pl.multiple_of(l * n_b, SUBLANES)
            utm[pl.ds(row, n_b), :] = ubuf[pl.ds(l, n_b, stride=S5_PITCH), :]
        return carry

    lax.fori_loop(0, chunk // SUBLANES, to_time_major, 0)

    sbuf[...] = jnp.dot(utm[...].astype(BF16), b_ref[...], preferred_element_type=F32)

    a_re = ab_ref[:, :S5_ST]
    a_im = ab_ref[:, S5_ST:]

    def scan(l8, carry):
        s_re, s_im = carry
        for i in range(SUBLANES):
            row = pl.multiple_of((l8 * SUBLANES + i) * n_b, SUBLANES)
            n_re = a_re * s_re - a_im * s_im + sbuf[pl.ds(row, n_b), :S5_ST]
            n_im = a_re * s_im + a_im * s_re + sbuf[pl.ds(row, n_b), S5_ST:]
            sbuf[pl.ds(row, n_b), :S5_ST] = n_re
            sbuf[pl.ds(row, n_b), S5_ST:] = n_im
            s_re, s_im = n_re, n_im
        return s_re, s_im

    s_re, s_im = lax.fori_loop(0, chunk // SUBLANES, scan, (state[:, :S5_ST], state[:, S5_ST:]))
    state[:, :S5_ST] = s_re
    state[:, S5_ST:] = s_im

    y = jnp.dot(sbuf[...].astype(BF16), c_ref[...], preferred_element_type=F32)
    y = y + d_ref[...] * utm[...]
    ytm[...] = jax.nn.gelu(y)
    for b in range(n_b):
        o_ref[b] = ytm[pl.ds(b, chunk, stride=n_b), :].astype(o_ref.dtype)


def _s5(u3, b_mat, c_mat, ab, d_flat):
    n_b, seq, _ = u3.shape
    assert n_b == SUBLANES, "the scan keeps one sequence per sublane"
    chunk = min(S5_CHUNK, seq)
    nb = SSM_GROUPS // S5_GROUP_BLOCK
    return pl.pallas_call(
        _s5_body,
        name="s5_scan",
        grid=(nb, seq // chunk),
        in_specs=[
            pl.BlockSpec((n_b, chunk, S5_CH), lambda j, c: (0, c, j)),
            pl.BlockSpec((None, S5_CH, 2 * S5_ST), lambda j, c: (j, 0, 0)),
            pl.BlockSpec((None, 2 * S5_ST, S5_CH), lambda j, c: (j, 0, 0)),
            pl.BlockSpec((None, SUBLANES, 2 * S5_ST), lambda j, c: (j, 0, 0)),
            pl.BlockSpec((1, S5_CH), lambda j, c: (0, j)),
        ],
        out_specs=pl.BlockSpec((n_b, chunk, S5_CH), lambda j, c: (0, c, j)),
        out_shape=jax.ShapeDtypeStruct((n_b, seq, SSM_WIDTH), BF16),
        scratch_shapes=[
            pltpu.VMEM((n_b * S5_PITCH, S5_CH), F32),
            pltpu.VMEM((chunk * n_b, S5_CH), F32),
            pltpu.VMEM((chunk * n_b, 2 * S5_ST), F32),
            pltpu.VMEM((n_b, 2 * S5_ST), F32),
            pltpu.VMEM((chunk * n_b, S5_CH), F32),
        ],
        compiler_params=_cparams(("parallel", "arbitrary")),
    )(u3, b_mat, c_mat, ab, d_flat)


def _attn_body(sinks_ref, q_ref, kp_ref, kc_ref, vp_ref, vc_ref, bias_ref, half_ref, o_ref):
    lo_mask = half_ref[0:1, :]
    hi_mask = half_ref[1:2, :]
    lo_sel = lax.broadcasted_iota(jnp.int32, (BLOCK, LANES), 1) < HEAD_DIM
    for kv in range(N_KV_HEADS):
        grp = slice(kv * LANES, (kv + 1) * LANES)
        kw = jnp.concatenate([kp_ref[:, grp], kc_ref[:, grp]], axis=0)
        vw = jnp.concatenate([vp_ref[:, grp], vc_ref[:, grp]], axis=0)
        qs = []
        for pair in range(Q_PER_KV // 2):
            lanes = slice((2 * kv + pair) * LANES, (2 * kv + pair + 1) * LANES)
            qp = q_ref[:, lanes]
            qs += [qp * lo_mask, qp * hi_mask]
        q_st = jnp.concatenate(qs, axis=0)
        s = lax.dot_general(q_st, kw, (((1,), (1,)), ((), ())), preferred_element_type=F32)
        s = s + bias_ref[kv]
        ps = []
        for g in range(Q_PER_KV):
            sg = s[g * BLOCK:(g + 1) * BLOCK]
            sink = sinks_ref[kv * Q_PER_KV + g]
            m = jnp.maximum(jnp.max(sg, axis=-1, keepdims=True), sink)
            p = jnp.exp(sg - m)
            den = jnp.sum(p, axis=-1, keepdims=True) + jnp.exp(sink - m)
            ps.append((p / den).astype(BF16))
        p_st = jnp.concatenate(ps, axis=0)
        o = jnp.dot(p_st, vw, preferred_element_type=F32)
        for pair in range(Q_PER_KV // 2):
            even = o[(2 * pair) * BLOCK:(2 * pair + 1) * BLOCK]
            odd = o[(2 * pair + 1) * BLOCK:(2 * pair + 2) * BLOCK]
            lanes = slice((2 * kv + pair) * LANES, (2 * kv + pair + 1) * LANES)
            o_ref[:, lanes] = jnp.where(lo_sel, even, odd).astype(o_ref.dtype)


def _attn_bias():
    h = jnp.arange(1, N_Q_HEADS + 1, dtype=F32)
    slopes = jnp.exp2(-8.0 * h / N_Q_HEADS)
    qi = jnp.arange(BLOCK)[:, None]
    kj = jnp.arange(2 * BLOCK)[None, :]
    dist = qi - kj + BLOCK
    band = (dist >= 0) & (dist < WINDOW)
    bias = -slopes[:, None, None] * dist.astype(F32)
    later = jnp.where(band[None], bias, -jnp.inf)
    first = jnp.where((band & (kj >= BLOCK))[None], bias, -jnp.inf)
    both = jnp.stack([first, later])
    return both.reshape(2, N_KV_HEADS, Q_PER_KV * BLOCK, 2 * BLOCK)


def _attention(q3, kd3, vd3, sinks):
    n_b, seq, _ = q3.shape
    n_blk = seq // BLOCK
    kv_dup = N_KV_HEADS * LANES
    prev = lambda b, i: (b, jnp.maximum(i - 1, 0), 0)
    cur = lambda b, i: (b, i, 0)
    lane = jnp.arange(LANES)
    half = jnp.stack([lane < HEAD_DIM, lane >= HEAD_DIM]).astype(BF16)
    return pl.pallas_call(
        _attn_body,
        name="swa_attention",
        grid=(n_b, n_blk),
        in_specs=[
            pl.BlockSpec(memory_space=pltpu.SMEM),
            pl.BlockSpec((None, BLOCK, ATTN_WIDTH), cur),
            pl.BlockSpec((None, BLOCK, kv_dup), prev),
            pl.BlockSpec((None, BLOCK, kv_dup), cur),
            pl.BlockSpec((None, BLOCK, kv_dup), prev),
            pl.BlockSpec((None, BLOCK, kv_dup), cur),
            pl.BlockSpec((None, N_KV_HEADS, Q_PER_KV * BLOCK, 2 * BLOCK),
                         lambda b, i: (jnp.minimum(i, 1), 0, 0, 0)),
            pl.BlockSpec((2, LANES), lambda b, i: (0, 0)),
        ],
        out_specs=pl.BlockSpec((None, BLOCK, ATTN_WIDTH), cur),
        out_shape=jax.ShapeDtypeStruct((n_b, seq, ATTN_WIDTH), BF16),
        compiler_params=_cparams(("parallel", "arbitrary")),
    )(sinks, q3, kd3, kd3, vd3, vd3, _attn_bias(), half)


ROW_TILE = D_MODEL // LANES
assert ROW_TILE == SUBLANES


def _store_row_tiles(ref, x):
    for i in range(x.shape[0] // SUBLANES):
        for j in range(ROW_TILE):
            ref[pl.ds(i * SUBLANES * ROW_TILE + j, SUBLANES, stride=ROW_TILE), :] = (
                x[i * SUBLANES:(i + 1) * SUBLANES, j * LANES:(j + 1) * LANES])


def _load_row_tiles(ref, n):
    cols = []
    for j in range(ROW_TILE):
        cols.append(jnp.concatenate(
            [ref[pl.ds(i * SUBLANES * ROW_TILE + j, SUBLANES, stride=ROW_TILE), :]
             for i in range(n // SUBLANES)], axis=0))
    return jnp.concatenate(cols, axis=1)


def _mix_body(y_ref, a_ref, gs_ref, ga_ref, x_ref, wglu_ref, wsp_ref, wap_ref, wo_ref,
              g2_ref, wr_ref, br_ref, x1_ref, lg_ref):
    z = jnp.dot(y_ref[...], wglu_ref[...], preferred_element_type=F32)
    glu = z[:, :SSM_WIDTH] * jax.nn.sigmoid(z[:, SSM_WIDTH:])
    y_ssm = jnp.dot(glu.astype(BF16), wsp_ref[...], preferred_element_type=F32)
    y_attn = jnp.dot(a_ref[...], wap_ref[...], preferred_element_type=F32)
    mixed = jax.nn.sigmoid(gs_ref[...]) * y_ssm + jax.nn.sigmoid(ga_ref[...]) * y_attn
    x1 = x_ref[...] + jnp.dot(mixed.astype(BF16), wo_ref[...], preferred_element_type=F32)
    _store_row_tiles(x1_ref, x1)
    h2 = _rms(x1, g2_ref[...]).astype(BF16)
    lg_ref[...] = jnp.dot(h2, wr_ref[...], preferred_element_type=F32) + br_ref[...]


def _mix(y2, a2, gs, ga, x2, w_glu, w_sp, w_ap, w_o, norm2_g, w_r_pad, b_r_pad):
    t = x2.shape[0]
    tm = min(TM_MIX, t)
    tok = lambda w: pl.BlockSpec((tm, w), lambda i: (i, 0))
    full = lambda a: pl.BlockSpec(a.shape, lambda i: (0, 0))
    return pl.pallas_call(
        _mix_body,
        name="mix",
        grid=(t // tm,),
        in_specs=[tok(SSM_WIDTH), tok(ATTN_WIDTH), tok(D_MODEL), tok(D_MODEL), tok(D_MODEL),
                  full(w_glu), full(w_sp), full(w_ap), full(w_o), full(norm2_g),
                  full(w_r_pad), full(b_r_pad)],
        out_specs=[pl.BlockSpec((tm * ROW_TILE, LANES), lambda i: (i, 0)), tok(LANES)],
        out_shape=[jax.ShapeDtypeStruct((t * ROW_TILE, LANES), F32),
                   jax.ShapeDtypeStruct((t, LANES), F32)],
        compiler_params=_cparams(("parallel",)),
    )(y2, a2, gs, ga, x2, w_glu, w_sp, w_ap, w_o, norm2_g, w_r_pad, b_r_pad)


def _route_body(lg_ref, tri_ref, slab_ref, cnt_ref, base):
    @pl.when(pl.program_id(0) == 0)
    def _():
        base[...] = jnp.zeros_like(base)

    tt = lg_ref.shape[0]
    lane = lax.broadcasted_iota(jnp.int32, (tt, LANES), 1)
    l = jnp.where(lane < N_EXPERTS, lg_ref[...], -jnp.inf)
    tops, hots = [], []
    for _ in range(TOP_K):
        m = jnp.max(l, axis=-1, keepdims=True)
        idx = jnp.min(jnp.where(l == m, lane, LANES), axis=-1, keepdims=True)
        hot = lane == idx
        tops.append((m, idx))
        hots.append(hot)
        l = jnp.where(hot, -jnp.inf, l)
    es = [jnp.exp(m - tops[0][0]) for m, _ in tops]
    den = es[0] + es[1] + es[2] + es[3]
    member = jnp.zeros((tt, LANES), F32)
    for hot in hots:
        member = member + jnp.where(hot, 1.0, 0.0)
    before = jnp.dot(tri_ref[...], member.astype(BF16), preferred_element_type=F32) + base[...]
    slab = jnp.zeros((tt, LANES), F32)
    for k in range(TOP_K):
        pos = jnp.sum(jnp.where(hots[k], before, 0.0), axis=-1, keepdims=True)
        slab = jnp.where(lane == k, tops[k][1].astype(F32), slab)
        slab = jnp.where(lane == TOP_K + k, es[k] / den, slab)
        slab = jnp.where(lane == 2 * TOP_K + k, pos, slab)
    slab_ref[...] = slab
    base[...] = base[...] + jnp.sum(member, axis=0, keepdims=True)
    cnt_ref[...] = base[...]


def _route(logits):
    t = logits.shape[0]
    tt = min(TT_ROUTE, t)
    tri = jnp.tril(jnp.ones((tt, tt), F32), -1).astype(BF16)
    return pl.pallas_call(
        _route_body,
        name="route",
        grid=(t // tt,),
        in_specs=[pl.BlockSpec((tt, LANES), lambda i: (i, 0)),
                  pl.BlockSpec((tt, tt), lambda i: (0, 0))],
        out_specs=[pl.BlockSpec((tt, LANES), lambda i: (i, 0)),
                   pl.BlockSpec((1, LANES), lambda i: (0, 0))],
        out_shape=[jax.ShapeDtypeStruct((t, LANES), F32), jax.ShapeDtypeStruct((1, LANES), F32)],
        scratch_shapes=[pltpu.VMEM((1, LANES), F32)],
        compiler_params=_cparams(("arbitrary",)),
    )(logits, tri)


def _row_copies(n_tok, issue_one):
    def body(t8, carry):
        for i in range(SUBLANES):
            issue_one(t8 * SUBLANES + i)
        return carry
    lax.fori_loop(0, n_tok // SUBLANES, body, 0)


def _row_tile(ref, row):
    return ref.at[pl.ds(pl.multiple_of(row * ROW_TILE, ROW_TILE), ROW_TILE)]


def _dispatch_body(pe_ref, nu_ref, dest_ref, x_ref, xs_ref, zbuf, sem, zsem):
    tt = x_ref.shape[0] // ROW_TILE
    blk_rows = MOE_BLOCK * ROW_TILE
    n_blocks = xs_ref.shape[0] // blk_rows

    @pl.when(pl.program_id(0) == 0)
    def _():
        zbuf[...] = jnp.zeros_like(zbuf)

        def zero_block(row):
            return pltpu.make_async_copy(
                zbuf, xs_ref.at[pl.ds(pl.multiple_of(row * ROW_TILE, blk_rows), blk_rows)], zsem)

        def each(act):
            for e in range(N_EXPERTS):
                @pl.when(pe_ref[e] >= MOE_BLOCK)
                def _():
                    act(zero_block(pe_ref[e] - MOE_BLOCK))

            def tail(i, carry):
                act(zero_block(i * MOE_BLOCK))
                return carry
            lax.fori_loop(nu_ref[0], n_blocks, tail, 0)

        each(lambda c: c.start())
        each(lambda c: c.wait())

    def issue(t):
        for k in range(TOP_K):
            d = dest_ref[0, 0, t * TOP_K + k]
            pltpu.make_async_copy(_row_tile(x_ref, t), _row_tile(xs_ref, d), sem).start(priority=k % 2)

    _row_copies(tt, issue)
    for _ in range(TOP_K):
        pltpu.make_async_copy(x_ref, xs_ref.at[pl.ds(0, tt * ROW_TILE)], sem).wait()


def _dispatch(pad_ends, n_used, x1r, dest3, n_rows):
    t = x1r.shape[0] // ROW_TILE
    tt = dest3.shape[2] // TOP_K
    grid_spec = pltpu.PrefetchScalarGridSpec(
        num_scalar_prefetch=2,
        grid=(t // tt,),
        in_specs=[pl.BlockSpec((1, 1, tt * TOP_K), lambda i, pe, nu: (i, 0, 0), memory_space=pltpu.SMEM),
                  pl.BlockSpec((tt * ROW_TILE, LANES), lambda i, pe, nu: (i, 0))],
        out_specs=pl.BlockSpec(memory_space=pl.ANY),
        scratch_shapes=[pltpu.VMEM((MOE_BLOCK * ROW_TILE, LANES), F32),
                        pltpu.SemaphoreType.DMA(()), pltpu.SemaphoreType.DMA(())],
    )
    return pl.pallas_call(
        _dispatch_body,
        name="dispatch",
        grid_spec=grid_spec,
        out_shape=jax.ShapeDtypeStruct((n_rows * ROW_TILE, LANES), F32),
        compiler_params=_cparams(("arbitrary",)),
    )(pad_ends, n_used, dest3, x1r)


def _ffn_body(b0_ref, nb_ref, nu_ref, g_ref, wgu_ref, bgu_ref, wd_ref, bd_ref, xs_ref, ys_ref,
              xbuf, ybuf, in_sems, out_sems, wgu_bf, wd_bf):
    e = pl.program_id(0)
    blk_rows = MOE_BLOCK * ROW_TILE
    n_blocks = xs_ref.shape[0] // blk_rows
    first = b0_ref[e]
    n = nb_ref[e]

    def rows(j):
        return pl.ds(pl.multiple_of((first + j) * blk_rows, blk_rows), blk_rows)

    def in_copy(j, slot):
        return pltpu.make_async_copy(xs_ref.at[rows(j)], xbuf.at[slot], in_sems.at[slot])

    def out_copy(j, slot):
        return pltpu.make_async_copy(ybuf.at[slot], ys_ref.at[rows(j)], out_sems.at[slot])

    @pl.when(n > 0)
    def _():
        in_copy(0, 0).start()

    wgu_bf[...] = wgu_ref[...].astype(BF16)
    wd_bf[...] = wd_ref[...].astype(BF16)

    def block(j, slot):
        in_copy(j, slot).wait()

        @pl.when(j + 1 < n)
        def _():
            in_copy(j + 1, 1 - slot).start()

        h = _rms(_load_row_tiles(xbuf.at[slot], MOE_BLOCK), g_ref[...]).astype(BF16)
        gu = jnp.dot(h, wgu_bf[...], preferred_element_type=F32) + bgu_ref[...]
        gate = jnp.minimum(gu[:, :D_FF], SWIGLU_LIMIT)
        up = jnp.clip(gu[:, D_FF:], -SWIGLU_LIMIT, SWIGLU_LIMIT)
        act = (up + 1.0) * gate * jax.nn.sigmoid(SWIGLU_ALPHA * gate)
        y = jnp.dot(act.astype(BF16), wd_bf[...], preferred_element_type=F32) + bd_ref[...]

        @pl.when(j >= 2)
        def _():
            out_copy(j - 2, slot).wait()

        _store_row_tiles(ybuf.at[slot], y)
        out_copy(j, slot).start()

    def pair(p, carry):
        for slot in range(2):
            j = 2 * p + slot

            @pl.when(j < n)
            def _():
                block(j, slot)
        return carry

    lax.fori_loop(0, (n + 1) // 2, pair, 0)

    for slot in range(2):
        @pl.when(n > slot)
        def _():
            out_copy(0, slot).wait()

    @pl.when(e == pl.num_programs(0) - 1)
    def _():
        ybuf[0] = jnp.zeros((blk_rows, LANES), F32)

        def tail(act):
            def body(i, carry):
                act(pltpu.make_async_copy(
                    ybuf.at[0], ys_ref.at[pl.ds(pl.multiple_of(i * blk_rows, blk_rows), blk_rows)],
                    out_sems.at[0]))
                return carry
            lax.fori_loop(nu_ref[0], n_blocks, body, 0)

        tail(lambda c: c.start())
        tail(lambda c: c.wait())


def _moe_ffn(blk_first, blk_count, n_used, xs, norm2_g, w_gu, b_gu, w_d, b_d):
    blk_rows = MOE_BLOCK * ROW_TILE
    by_e = lambda e, b0, nb, nu: (e, 0, 0)
    grid_spec = pltpu.PrefetchScalarGridSpec(
        num_scalar_prefetch=3,
        grid=(N_EXPERTS,),
        in_specs=[
            pl.BlockSpec((1, D_MODEL), lambda e, b0, nb, nu: (0, 0)),
            pl.BlockSpec((None, D_MODEL, 2 * D_FF), by_e),
            pl.BlockSpec((None, 1, 2 * D_FF), by_e),
            pl.BlockSpec((None, D_FF, D_MODEL), by_e),
            pl.BlockSpec((None, 1, D_MODEL), by_e),
            pl.BlockSpec(memory_space=pl.ANY),
        ],
        out_specs=pl.BlockSpec(memory_space=pl.ANY),
        scratch_shapes=[pltpu.VMEM((2, blk_rows, LANES), F32), pltpu.VMEM((2, blk_rows, LANES), F32),
                        pltpu.SemaphoreType.DMA((2,)), pltpu.SemaphoreType.DMA((2,)),
                        pltpu.VMEM((D_MODEL, 2 * D_FF), BF16), pltpu.VMEM((D_FF, D_MODEL), BF16)],
    )
    return pl.pallas_call(
        _ffn_body,
        name="moe_ffn",
        grid_spec=grid_spec,
        out_shape=jax.ShapeDtypeStruct(xs.shape, F32),
        compiler_params=_cparams(("arbitrary",)),
    )(blk_first, blk_count, n_used, norm2_g, w_gu, b_gu, w_d, b_d, xs)


def _combine_body(dest_ref, x_ref, slab_ref, ys_ref, o_ref, buf, sems):
    i = pl.program_id(0)
    n_tiles = pl.num_programs(0) - 1
    tt = o_ref.shape[0]

    def gathers(slot):
        def issue(t):
            for k in range(TOP_K):
                d = dest_ref[0, 0, t * TOP_K + k]
                pltpu.make_async_copy(_row_tile(ys_ref, d), _row_tile(buf.at[slot, k], t),
                                      sems.at[slot]).start(priority=k % 2)
        _row_copies(tt, issue)

    def finish(slot):
        for k in range(TOP_K):
            pltpu.make_async_copy(ys_ref.at[pl.ds(0, tt * ROW_TILE)], buf.at[slot, k], sems.at[slot]).wait()
        acc = _load_row_tiles(x_ref, tt)
        for k in range(TOP_K):
            acc = acc + slab_ref[:, TOP_K + k:TOP_K + k + 1] * _load_row_tiles(buf.at[slot, k], tt)
        o_ref[...] = acc

    for slot in range(2):
        @pl.when((i < n_tiles) & (i % 2 == slot))
        def _():
            gathers(slot)

        @pl.when((i > 0) & ((i - 1) % 2 == slot))
        def _():
            finish(slot)


def _combine(x1r, slab, dest3, ys):
    t = x1r.shape[0] // ROW_TILE
    tt = dest3.shape[2] // TOP_K
    n_tiles = t // tt
    ahead = lambda i: (jnp.minimum(i, n_tiles - 1), 0, 0)
    behind = lambda i: (jnp.maximum(i - 1, 0), 0)
    return pl.pallas_call(
        _combine_body,
        name="combine",
        grid=(n_tiles + 1,),
        in_specs=[pl.BlockSpec((1, 1, tt * TOP_K), ahead, memory_space=pltpu.SMEM),
                  pl.BlockSpec((tt * ROW_TILE, LANES), behind),
                  pl.BlockSpec((tt, LANES), behind),
                  pl.BlockSpec(memory_space=pl.ANY)],
        out_specs=pl.BlockSpec((tt, D_MODEL), behind),
        out_shape=jax.ShapeDtypeStruct((t, D_MODEL), F32),
        scratch_shapes=[pltpu.VMEM((2, TOP_K, tt * ROW_TILE, LANES), F32), pltpu.SemaphoreType.DMA((2,))],
        compiler_params=_cparams(("arbitrary",)),
    )(dest3, x1r, slab, ys)


def _moe(x1, logits, norm2_g, w_gate_up, b_gate_up, w_down, b_down):
    t = x1.shape[0] // ROW_TILE
    n_assign = t * TOP_K
    n_blocks = -(-n_assign // MOE_BLOCK) + N_EXPERTS
    n_rows = n_blocks * MOE_BLOCK

    slab, cnt = _route(logits)
    ids = slab[:, :TOP_K].astype(jnp.int32)
    pos = slab[:, 2 * TOP_K:3 * TOP_K].astype(jnp.int32)
    counts = cnt[0, :N_EXPERTS].astype(jnp.int32)
    padded = (counts + MOE_BLOCK - 1) // MOE_BLOCK * MOE_BLOCK
    pad_ends = jnp.cumsum(padded)
    pad_starts = pad_ends - padded
    dest = pad_starts[ids] + pos
    n_used = (pad_ends[-1] // MOE_BLOCK).astype(jnp.int32).reshape(1)

    tt = min(TT_MOVE, t)
    dest3 = dest.reshape(t // tt, 1, tt * TOP_K)
    xs = _dispatch(pad_ends.astype(jnp.int32), n_used, x1, dest3, n_rows)
    ys = _moe_ffn((pad_starts // MOE_BLOCK).astype(jnp.int32), (padded // MOE_BLOCK).astype(jnp.int32),
                  n_used, xs, norm2_g,
                  w_gate_up, b_gate_up.reshape(N_EXPERTS, 1, 2 * D_FF),
                  w_down, b_down.reshape(N_EXPERTS, 1, D_MODEL))
    return _combine(x1, slab, dest3, ys)


def _layer(x, norm1_g, w_in, ssm_a_re, ssm_a_im, ssm_log_dt, ssm_b_re, ssm_b_im, ssm_c_re,
           ssm_c_im, ssm_d, w_glu, w_ssm_proj, q_norm_g, k_norm_g, attn_sinks, w_attn_proj,
           w_out, norm2_g, w_router, b_router, w_gate_up, b_gate_up, w_down, b_down):
    n_b, seq, d = x.shape
    t = n_b * seq
    x2 = x.reshape(t, d)
    u, q, kd, vd, gs, ga = _in_proj(x2, norm1_g.reshape(1, d), w_in, q_norm_g, k_norm_g)

    abar_re, abar_im, bbt_re, bbt_im = _s5_prep(ssm_a_re, ssm_a_im, ssm_log_dt, ssm_b_re, ssm_b_im)
    b_mat, c_mat, ab = _s5_matrices(abar_re, abar_im, bbt_re, bbt_im, ssm_c_re, ssm_c_im)
    y = _s5(u.reshape(n_b, seq, SSM_WIDTH), b_mat, c_mat, ab, ssm_d.reshape(1, SSM_WIDTH))

    a = _attention(q.reshape(n_b, seq, ATTN_WIDTH), kd.reshape(n_b, seq, KV_DUP),
                   vd.reshape(n_b, seq, KV_DUP), attn_sinks)

    w_r_pad = jnp.zeros((d, LANES), BF16).at[:, :N_EXPERTS].set(w_router.astype(BF16))
    b_r_pad = jnp.zeros((1, LANES), F32).at[0, :N_EXPERTS].set(b_router)
    norm2 = norm2_g.reshape(1, d)
    x1, logits = _mix(y.reshape(t, SSM_WIDTH), a.reshape(t, ATTN_WIDTH), gs, ga, x2,
                      w_glu.astype(BF16), w_ssm_proj.astype(BF16), w_attn_proj.astype(BF16),
                      w_out.astype(BF16), norm2, w_r_pad, b_r_pad)
    out = _moe(x1, logits, norm2, w_gate_up, b_gate_up, w_down, b_down)
    return out.reshape(n_b, seq, d)


def kernel(x, norm1_g, w_in, ssm_a_re, ssm_a_im, ssm_log_dt, ssm_b_re, ssm_b_im, ssm_c_re, ssm_c_im, ssm_d, w_glu, w_ssm_proj, q_norm_g, k_norm_g, attn_sinks, w_attn_proj, w_out, norm2_g, w_router, b_router, w_gate_up, b_gate_up, w_down, b_down):
    for layer in range(norm1_g.shape[0]):
        x = _layer(
            x, norm1_g[layer], w_in[layer], ssm_a_re[layer], ssm_a_im[layer], ssm_log_dt[layer],
            ssm_b_re[layer], ssm_b_im[layer], ssm_c_re[layer], ssm_c_im[layer], ssm_d[layer],
            w_glu[layer], w_ssm_proj[layer], q_norm_g[layer], k_norm_g[layer], attn_sinks[layer],
            w_attn_proj[layer], w_out[layer], norm2_g[layer], w_router[layer], b_router[layer],
            w_gate_up[layer], b_gate_up[layer], w_down[layer], b_down[layer])
    return x
```

```python
import functools

import jax
import jax.numpy as jnp
from jax import lax
from jax.experimental import pallas as pl
from jax.experimental.pallas import tpu as pltpu

D_MODEL = 1024
NORM_EPS = 1e-5
SSM_WIDTH = 1024
SSM_GROUP_CH = 16
SSM_GROUPS = SSM_WIDTH // SSM_GROUP_CH
SSM_STATE = 64
HEAD_DIM = 64
N_Q_HEADS = 16
N_KV_HEADS = 4
Q_PER_KV = N_Q_HEADS // N_KV_HEADS
ATTN_WIDTH = N_Q_HEADS * HEAD_DIM
KV_WIDTH = N_KV_HEADS * HEAD_DIM
WINDOW = 128
BLOCK = 128
N_EXPERTS = 32
TOP_K = 4
D_FF = 1024
SWIGLU_LIMIT = 7.0
SWIGLU_ALPHA = 1.702
MOE_BLOCK = 512

LANES = 128
SUBLANES = 8
VMEM_LIMIT = 56 * 1024 * 1024

S5_GROUP_BLOCK = 8
S5_CH = S5_GROUP_BLOCK * SSM_GROUP_CH
S5_ST = S5_GROUP_BLOCK * SSM_STATE
S5_CHUNK = 256
S5_PITCH = S5_CHUNK + SUBLANES

TM_IN = 256
TM_MIX = 512
TT_ROUTE = 512
TT_MOVE = 256

F32 = jnp.float32
BF16 = jnp.bfloat16


def _cparams(sem):
    return pltpu.CompilerParams(dimension_semantics=sem, vmem_limit_bytes=VMEM_LIMIT)


def _rms(x, g):
    return x * lax.rsqrt(jnp.mean(x * x, axis=-1, keepdims=True) + NORM_EPS) * g


KV_DUP = N_KV_HEADS * LANES


def _in_proj_body(x_ref, g_ref, w_ref, qg_ref, kg_ref, u_ref, q_ref, kd_ref, vd_ref, gs_ref, ga_ref):
    h = _rms(x_ref[...], g_ref[...]).astype(BF16)
    off = [0]

    def proj(width):
        lo = off[0]
        off[0] = lo + width
        return jnp.dot(h, w_ref[:, lo:lo + width], preferred_element_type=F32)

    u_ref[...] = proj(SSM_WIDTH)

    q = proj(ATTN_WIDTH)
    lo_sel = lax.broadcasted_iota(jnp.int32, (q.shape[0], LANES), 1) < HEAD_DIM
    for j in range(ATTN_WIDTH // LANES):
        qq = q[:, j * LANES:(j + 1) * LANES]
        sq = qq * qq
        ms_lo = jnp.sum(jnp.where(lo_sel, sq, 0.0), axis=-1, keepdims=True) / HEAD_DIM
        ms_hi = jnp.sum(jnp.where(lo_sel, 0.0, sq), axis=-1, keepdims=True) / HEAD_DIM
        r = jnp.where(lo_sel, lax.rsqrt(ms_lo + NORM_EPS), lax.rsqrt(ms_hi + NORM_EPS))
        q_ref[:, j * LANES:(j + 1) * LANES] = (qq * r * qg_ref[...]).astype(q_ref.dtype)

    kd = proj(KV_DUP)
    for kv in range(N_KV_HEADS):
        grp = slice(kv * LANES, (kv + 1) * LANES)
        kd_ref[:, grp] = _rms(kd[:, grp], kg_ref[...]).astype(kd_ref.dtype)
    vd_ref[...] = proj(KV_DUP).astype(vd_ref.dtype)
    gs_ref[...] = proj(D_MODEL)
    ga_ref[...] = proj(D_MODEL)


def _in_proj_weight(w_in):
    d = w_in.shape[0]
    cuts = (SSM_WIDTH, SSM_WIDTH + ATTN_WIDTH, SSM_WIDTH + ATTN_WIDTH + KV_WIDTH,
            SSM_WIDTH + ATTN_WIDTH + 2 * KV_WIDTH)
    w_uq, w_k, w_v, w_g = (w_in[:, :cuts[1]], w_in[:, cuts[1]:cuts[2]], w_in[:, cuts[2]:cuts[3]],
                           w_in[:, cuts[3]:])

    def twice(w):
        w = w.reshape(d, N_KV_HEADS, 1, HEAD_DIM)
        return jnp.concatenate([w, w], axis=2).reshape(d, KV_DUP)

    return jnp.concatenate([w_uq, twice(w_k), twice(w_v), w_g], axis=1).astype(BF16)


def _in_proj(x2, norm_g, w_in, q_norm_g, k_norm_g):
    t = x2.shape[0]
    w = _in_proj_weight(w_in)
    qg = (jnp.tile(q_norm_g, LANES // HEAD_DIM) * HEAD_DIM ** -0.5).reshape(1, LANES)
    kg = jnp.tile(k_norm_g, LANES // HEAD_DIM).reshape(1, LANES)
    outs = ((SSM_WIDTH, F32), (ATTN_WIDTH, BF16), (KV_DUP, BF16), (KV_DUP, BF16),
            (D_MODEL, F32), (D_MODEL, F32))
    const = lambda a: pl.BlockSpec(a.shape, lambda i: (0, 0))
    return pl.pallas_call(
        _in_proj_body,
        name="in_proj",
        grid=(t // TM_IN,),
        in_specs=[pl.BlockSpec((TM_IN, D_MODEL), lambda i: (i, 0)), const(norm_g), const(w),
                  const(qg), const(kg)],
        out_specs=[pl.BlockSpec((TM_IN, n), lambda i: (i, 0)) for n, _ in outs],
        out_shape=[jax.ShapeDtypeStruct((t, n), dt) for n, dt in outs],
        compiler_params=_cparams(("parallel",)),
    )(x2, norm_g, w, qg, kg)


def _s5_prep_body(are_ref, aim_ref, ldt_ref, bre_ref, bim_ref,
                  abr_ref, abi_ref, bbr_ref, bbi_ref):
    a_re = are_ref[...]
    a_im = aim_ref[...]
    dt = jnp.exp(ldt_ref[...])
    mag = jnp.exp(a_re * dt)
    abar_re = mag * jnp.cos(a_im * dt)
    abar_im = mag * jnp.sin(a_im * dt)
    den = a_re * a_re + a_im * a_im
    q_re = ((abar_re - 1.0) * a_re + abar_im * a_im) / den
    q_im = (abar_im * a_re - (abar_re - 1.0) * a_im) / den
    abr_ref[...] = abar_re
    abi_ref[...] = abar_im
    for h in range(SSM_GROUP_CH):
        bbr_ref[h] = q_re * bre_ref[h] - q_im * bim_ref[h]
        bbi_ref[h] = q_re * bim_ref[h] + q_im * bre_ref[h]


def _s5_prep(a_re, a_im, log_dt, b_re, b_im):
    g, p, hc = SSM_GROUPS, SSM_STATE, SSM_GROUP_CH
    bt_re = jnp.transpose(b_re, (2, 0, 1))
    bt_im = jnp.transpose(b_im, (2, 0, 1))
    return pl.pallas_call(
        _s5_prep_body,
        name="s5_prep",
        out_shape=[
            jax.ShapeDtypeStruct((g, p), F32),
            jax.ShapeDtypeStruct((g, p), F32),
            jax.ShapeDtypeStruct((hc, g, p), F32),
            jax.ShapeDtypeStruct((hc, g, p), F32),
        ],
    )(a_re, a_im, log_dt.reshape(g, 1), bt_re, bt_im)


def _s5_matrices(abar_re, abar_im, bbt_re, bbt_im, c_re, c_im):
    nb, gb, hc, p = SSM_GROUPS // S5_GROUP_BLOCK, S5_GROUP_BLOCK, SSM_GROUP_CH, SSM_STATE
    eye = jnp.eye(gb, dtype=F32)

    def b_block(bbt):
        t = jnp.transpose(bbt, (1, 0, 2)).reshape(nb, gb, hc, p)
        return jnp.einsum('jghp,gk->jghkp', t, eye).reshape(nb, gb * hc, gb * p)

    def c_block(c):
        t = c.reshape(nb, gb, hc, p)
        return jnp.einsum('jghp,gk->jgpkh', t, eye).reshape(nb, gb * p, gb * hc)

    b_mat = jnp.concatenate([b_block(bbt_re), b_block(bbt_im)], axis=2).astype(BF16)
    c_mat = jnp.concatenate([c_block(c_re), c_block(-c_im)], axis=1).astype(BF16)
    ab = jnp.concatenate([abar_re.reshape(nb, 1, gb * p), abar_im.reshape(nb, 1, gb * p)], axis=2)
    ab = jnp.broadcast_to(ab, (nb, SUBLANES, 2 * gb * p))
    return b_mat, c_mat, ab


def _s5_body(u_ref, b_ref, c_ref, ab_ref, d_ref, o_ref, ubuf, utm, sbuf, state, ytm):
    n_b = u_ref.shape[0]
    chunk = u_ref.shape[1]

    @pl.when(pl.program_id(1) == 0)
    def _():
        state[...] = jnp.zeros_like(state)

    for b in range(n_b):
        ubuf[b * S5_PITCH:b * S5_PITCH + chunk, :] = u_ref[b]

    def to_time_major(l8, carry):
        for i in range(SUBLANES):
            l = l8 * SUBLANES + i
            row = pl.multiple_of(l * n_b, SUBLANES)
            utm[pl.ds(row, n_b), :] = ubuf[pl.ds(l, n_b, stride=S5_PITCH), :]
        return carry

    lax.fori_loop(0, chunk // SUBLANES, to_time_major, 0)

    sbuf[...] = jnp.dot(utm[...].astype(BF16), b_ref[...], preferred_element_type=F32)

    a_re = ab_ref[:, :S5_ST]
    a_im = ab_ref[:, S5_ST:]

    def scan(l8, carry):
        s_re, s_im = carry
        for i in range(SUBLANES):
            row = pl.multiple_of((l8 * SUBLANES + i) * n_b, SUBLANES)
            n_re = a_re * s_re - a_im * s_im + sbuf[pl.ds(row, n_b), :S5_ST]
            n_im = a_re * s_im + a_im * s_re + sbuf[pl.ds(row, n_b), S5_ST:]
            sbuf[pl.ds(row, n_b), :S5_ST] = n_re
            sbuf[pl.ds(row, n_b), S5_ST:] = n_im
            s_re, s_im = n_re, n_im
        return s_re, s_im

    s_re, s_im = lax.fori_loop(0, chunk // SUBLANES, scan, (state[:, :S5_ST], state[:, S5_ST:]))
    state[:, :S5_ST] = s_re
    state[:, S5_ST:] = s_im

    y = jnp.dot(sbuf[...].astype(BF16), c_ref[...], preferred_element_type=F32)
    y = y + d_ref[...] * utm[...]
    ytm[...] = jax.nn.gelu(y)
    for b in range(n_b):
        o_ref[b] = ytm[pl.ds(b, chunk, stride=n_b), :].astype(o_ref.dtype)


def _s5(u3, b_mat, c_mat, ab, d_flat):
    n_b, seq, _ = u3.shape
    assert n_b == SUBLANES, "the scan keeps one sequence per sublane"
    chunk = min(S5_CHUNK, seq)
    nb = SSM_GROUPS // S5_GROUP_BLOCK
    return pl.pallas_call(
        _s5_body,
        name="s5_scan",
        grid=(nb, seq // chunk),
        in_specs=[
            pl.BlockSpec((n_b, chunk, S5_CH), lambda j, c: (0, c, j)),
            pl.BlockSpec((None, S5_CH, 2 * S5_ST), lambda j, c: (j, 0, 0)),
            pl.BlockSpec((None, 2 * S5_ST, S5_CH), lambda j, c: (j, 0, 0)),
            pl.BlockSpec((None, SUBLANES, 2 * S5_ST), lambda j, c: (j, 0, 0)),
            pl.BlockSpec((1, S5_CH), lambda j, c: (0, j)),
        ],
        out_specs=pl.BlockSpec((n_b, chunk, S5_CH), lambda j, c: (0, c, j)),
        out_shape=jax.ShapeDtypeStruct((n_b, seq, SSM_WIDTH), BF16),
        scratch_shapes=[
            pltpu.VMEM((n_b * S5_PITCH, S5_CH), F32),
            pltpu.VMEM((chunk * n_b, S5_CH), F32),
            pltpu.VMEM((chunk * n_b, 2 * S5_ST), F32),
            pltpu.VMEM((n_b, 2 * S5_ST), F32),
            pltpu.VMEM((chunk * n_b, S5_CH), F32),
        ],
        compiler_params=_cparams(("parallel", "arbitrary")),
    )(u3, b_mat, c_mat, ab, d_flat)


def _attn_body(sinks_ref, q_ref, kp_ref, kc_ref, vp_ref, vc_ref, bias_ref, half_ref, o_ref):
    lo_mask = half_ref[0:1, :]
    hi_mask = half_ref[1:2, :]
    lo_sel = lax.broadcasted_iota(jnp.int32, (BLOCK, LANES), 1) < HEAD_DIM
    for kv in range(N_KV_HEADS):
        grp = slice(kv * LANES, (kv + 1) * LANES)
        kw = jnp.concatenate([kp_ref[:, grp], kc_ref[:, grp]], axis=0)
        vw = jnp.concatenate([vp_ref[:, grp], vc_ref[:, grp]], axis=0)
        qs = []
        for pair in range(Q_PER_KV // 2):
            lanes = slice((2 * kv + pair) * LANES, (2 * kv + pair + 1) * LANES)
            qp = q_ref[:, lanes]
            qs += [qp * lo_mask, qp * hi_mask]
        q_st = jnp.concatenate(qs, axis=0)
        s = lax.dot_general(q_st, kw, (((1,), (1,)), ((), ())), preferred_element_type=F32)
        s = s + bias_ref[kv]
        ps = []
        for g in range(Q_PER_KV):
            sg = s[g * BLOCK:(g + 1) * BLOCK]
            sink = sinks_ref[kv * Q_PER_KV + g]
            m = jnp.maximum(jnp.max(sg, axis=-1, keepdims=True), sink)
            p = jnp.exp(sg - m)
            den = jnp.sum(p, axis=-1, keepdims=True) + jnp.exp(sink - m)
            ps.append((p / den).astype(BF16))
        p_st = jnp.concatenate(ps, axis=0)
        o = jnp.dot(p_st, vw, preferred_element_type=F32)
        for pair in range(Q_PER_KV // 2):
            even = o[(2 * pair) * BLOCK:(2 * pair + 1) * BLOCK]
            odd = o[(2 * pair + 1) * BLOCK:(2 * pair + 2) * BLOCK]
            lanes = slice((2 * kv + pair) * LANES, (2 * kv + pair + 1) * LANES)
            o_ref[:, lanes] = jnp.where(lo_sel, even, odd).astype(o_ref.dtype)


def _attn_bias():
    h = jnp.arange(1, N_Q_HEADS + 1, dtype=F32)
    slopes = jnp.exp2(-8.0 * h / N_Q_HEADS)
    qi = jnp.arange(BLOCK)[:, None]
    kj = jnp.arange(2 * BLOCK)[None, :]
    dist = qi - kj + BLOCK
    band = (dist >= 0) & (dist < WINDOW)
    bias = -slopes[:, None, None] * dist.astype(F32)
    later = jnp.where(band[None], bias, -jnp.inf)
    first = jnp.where((band & (kj >= BLOCK))[None], bias, -jnp.inf)
    both = jnp.stack([first, later])
    return both.reshape(2, N_KV_HEADS, Q_PER_KV * BLOCK, 2 * BLOCK)


def _attention(q3, kd3, vd3, sinks):
    n_b, seq, _ = q3.shape
    n_blk = seq // BLOCK
    kv_dup = N_KV_HEADS * LANES
    prev = lambda b, i: (b, jnp.maximum(i - 1, 0), 0)
    cur = lambda b, i: (b, i, 0)
    lane = jnp.arange(LANES)
    half = jnp.stack([lane < HEAD_DIM, lane >= HEAD_DIM]).astype(BF16)
    return pl.pallas_call(
        _attn_body,
        name="swa_attention",
        grid=(n_b, n_blk),
        in_specs=[
            pl.BlockSpec(memory_space=pltpu.SMEM),
            pl.BlockSpec((None, BLOCK, ATTN_WIDTH), cur),
            pl.BlockSpec((None, BLOCK, kv_dup), prev),
            pl.BlockSpec((None, BLOCK, kv_dup), cur),
            pl.BlockSpec((None, BLOCK, kv_dup), prev),
            pl.BlockSpec((None, BLOCK, kv_dup), cur),
            pl.BlockSpec((None, N_KV_HEADS, Q_PER_KV * BLOCK, 2 * BLOCK),
                         lambda b, i: (jnp.minimum(i, 1), 0, 0, 0)),
            pl.BlockSpec((2, LANES), lambda b, i: (0, 0)),
        ],
        out_specs=pl.BlockSpec((None, BLOCK, ATTN_WIDTH), cur),
        out_shape=jax.ShapeDtypeStruct((n_b, seq, ATTN_WIDTH), BF16),
        compiler_params=_cparams(("parallel", "arbitrary")),
    )(sinks, q3, kd3, kd3, vd3, vd3, _attn_bias(), half)


ROW_TILE = D_MODEL // LANES
assert ROW_TILE == SUBLANES


def _store_row_tiles(ref, x):
    for i in range(x.shape[0] // SUBLANES):
        for j in range(ROW_TILE):
            ref[pl.ds(i * SUBLANES * ROW_TILE + j, SUBLANES, stride=ROW_TILE), :] = (
                x[i * SUBLANES:(i + 1) * SUBLANES, j * LANES:(j + 1) * LANES])


def _load_row_tiles(ref, n):
    cols = []
    for j in range(ROW_TILE):
        cols.append(jnp.concatenate(
            [ref[pl.ds(i * SUBLANES * ROW_TILE + j, SUBLANES, stride=ROW_TILE), :]
             for i in range(n // SUBLANES)], axis=0))
    return jnp.concatenate(cols, axis=1)


def _mix_body(y_ref, a_ref, gs_ref, ga_ref, x_ref, wglu_ref, wsp_ref, wap_ref, wo_ref,
              g2_ref, wr_ref, br_ref, x1_ref, lg_ref):
    z = jnp.dot(y_ref[...], wglu_ref[...], preferred_element_type=F32)
    glu = z[:, :SSM_WIDTH] * jax.nn.sigmoid(z[:, SSM_WIDTH:])
    y_ssm = jnp.dot(glu.astype(BF16), wsp_ref[...], preferred_element_type=F32)
    y_attn = jnp.dot(a_ref[...], wap_ref[...], preferred_element_type=F32)
    mixed = jax.nn.sigmoid(gs_ref[...]) * y_ssm + jax.nn.sigmoid(ga_ref[...]) * y_attn
    x1 = x_ref[...] + jnp.dot(mixed.astype(BF16), wo_ref[...], preferred_element_type=F32)
    _store_row_tiles(x1_ref, x1)
    h2 = _rms(x1, g2_ref[...]).astype(BF16)
    lg_ref[...] = jnp.dot(h2, wr_ref[...], preferred_element_type=F32) + br_ref[...]


def _mix(y2, a2, gs, ga, x2, w_glu, w_sp, w_ap, w_o, norm2_g, w_r_pad, b_r_pad):
    t = x2.shape[0]
    tm = min(TM_MIX, t)
    tok = lambda w: pl.BlockSpec((tm, w), lambda i: (i, 0))
    full = lambda a: pl.BlockSpec(a.shape, lambda i: (0, 0))
    return pl.pallas_call(
        _mix_body,
        name="mix",
        grid=(t // tm,),
        in_specs=[tok(SSM_WIDTH), tok(ATTN_WIDTH), tok(D_MODEL), tok(D_MODEL), tok(D_MODEL),
                  full(w_glu), full(w_sp), full(w_ap), full(w_o), full(norm2_g),
                  full(w_r_pad), full(b_r_pad)],
        out_specs=[pl.BlockSpec((tm * ROW_TILE, LANES), lambda i: (i, 0)), tok(LANES)],
        out_shape=[jax.ShapeDtypeStruct((t * ROW_TILE, LANES), F32),
                   jax.ShapeDtypeStruct((t, LANES), F32)],
        compiler_params=_cparams(("parallel",)),
    )(y2, a2, gs, ga, x2, w_glu, w_sp, w_ap, w_o, norm2_g, w_r_pad, b_r_pad)


def _route_body(lg_ref, tri_ref, slab_ref, cnt_ref, base):
    @pl.when(pl.program_id(0) == 0)
    def _():
        base[...] = jnp.zeros_like(base)

    tt = lg_ref.shape[0]
    lane = lax.broadcasted_iota(jnp.int32, (tt, LANES), 1)
    l = jnp.where(lane < N_EXPERTS, lg_ref[...], -jnp.inf)
    tops, hots = [], []
    for _ in range(TOP_K):
        m = jnp.max(l, axis=-1, keepdims=True)
        idx = jnp.min(jnp.where(l == m, lane, LANES), axis=-1, keepdims=True)
        hot = lane == idx
        tops.append((m, idx))
        hots.append(hot)
        l = jnp.where(hot, -jnp.inf, l)
    es = [jnp.exp(m - tops[0][0]) for m, _ in tops]
    den = es[0] + es[1] + es[2] + es[3]
    member = jnp.zeros((tt, LANES), F32)
    for hot in hots:
        member = member + jnp.where(hot, 1.0, 0.0)
    before = jnp.dot(tri_ref[...], member.astype(BF16), preferred_element_type=F32) + base[...]
    slab = jnp.zeros((tt, LANES), F32)
    for k in range(TOP_K):
        pos = jnp.sum(jnp.where(hots[k], before, 0.0), axis=-1, keepdims=True)
        slab = jnp.where(lane == k, tops[k][1].astype(F32), slab)
        slab = jnp.where(lane == TOP_K + k, es[k] / den, slab)
        slab = jnp.where(lane == 2 * TOP_K + k, pos, slab)
    slab_ref[...] = slab
    base[...] = base[...] + jnp.sum(member, axis=0, keepdims=True)
    cnt_ref[...] = base[...]


def _route(logits):
    t = logits.shape[0]
    tt = min(TT_ROUTE, t)
    tri = jnp.tril(jnp.ones((tt, tt), F32), -1).astype(BF16)
    return pl.pallas_call(
        _route_body,
        name="route",
        grid=(t // tt,),
        in_specs=[pl.BlockSpec((tt, LANES), lambda i: (i, 0)),
                  pl.BlockSpec((tt, tt), lambda i: (0, 0))],
        out_specs=[pl.BlockSpec((tt, LANES), lambda i: (i, 0)),
                   pl.BlockSpec((1, LANES), lambda i: (0, 0))],
        out_shape=[jax.ShapeDtypeStruct((t, LANES), F32), jax.ShapeDtypeStruct((1, LANES), F32)],
        scratch_shapes=[pltpu.VMEM((1, LANES), F32)],
        compiler_params=_cparams(("arbitrary",)),
    )(logits, tri)


def _row_copies(n_tok, issue_one):
    def body(t8, carry):
        for i in range(SUBLANES):
            issue_one(t8 * SUBLANES + i)
        return carry
    lax.fori_loop(0, n_tok // SUBLANES, body, 0)


def _row_tile(ref, row):
    return ref.at[pl.ds(pl.multiple_of(row * ROW_TILE, ROW_TILE), ROW_TILE)]


def _dispatch_body(pe_ref, nu_ref, dest_ref, x_ref, xs_ref, zbuf, sem, zsem):
    tt = x_ref.shape[0] // ROW_TILE
    blk_rows = MOE_BLOCK * ROW_TILE
    n_blocks = xs_ref.shape[0] // blk_rows

    @pl.when(pl.program_id(0) == 0)
    def _():
        zbuf[...] = jnp.zeros_like(zbuf)

        def zero_block(row):
            return pltpu.make_async_copy(
                zbuf, xs_ref.at[pl.ds(pl.multiple_of(row * ROW_TILE, blk_rows), blk_rows)], zsem)

        def each(act):
            for e in range(N_EXPERTS):
                @pl.when(pe_ref[e] >= MOE_BLOCK)
                def _():
                    act(zero_block(pe_ref[e] - MOE_BLOCK))

            def tail(i, carry):
                act(zero_block(i * MOE_BLOCK))
                return carry
            lax.fori_loop(nu_ref[0], n_blocks, tail, 0)

        each(lambda c: c.start())
        each(lambda c: c.wait())

    def issue(t):
        for k in range(TOP_K):
            d = dest_ref[0, 0, t * TOP_K + k]
            pltpu.make_async_copy(_row_tile(x_ref, t), _row_tile(xs_ref, d), sem).start(priority=k % 2)

    _row_copies(tt, issue)
    for _ in range(TOP_K):
        pltpu.make_async_copy(x_ref, xs_ref.at[pl.ds(0, tt * ROW_TILE)], sem).wait()


def _dispatch(pad_ends, n_used, x1r, dest3, n_rows):
    t = x1r.shape[0] // ROW_TILE
    tt = dest3.shape[2] // TOP_K
    grid_spec = pltpu.PrefetchScalarGridSpec(
        num_scalar_prefetch=2,
        grid=(t // tt,),
        in_specs=[pl.BlockSpec((1, 1, tt * TOP_K), lambda i, pe, nu: (i, 0, 0), memory_space=pltpu.SMEM),
                  pl.BlockSpec((tt * ROW_TILE, LANES), lambda i, pe, nu: (i, 0))],
        out_specs=pl.BlockSpec(memory_space=pl.ANY),
        scratch_shapes=[pltpu.VMEM((MOE_BLOCK * ROW_TILE, LANES), F32),
                        pltpu.SemaphoreType.DMA(()), pltpu.SemaphoreType.DMA(())],
    )
    return pl.pallas_call(
        _dispatch_body,
        name="dispatch",
        grid_spec=grid_spec,
        out_shape=jax.ShapeDtypeStruct((n_rows * ROW_TILE, LANES), F32),
        compiler_params=_cparams(("arbitrary",)),
    )(pad_ends, n_used, dest3, x1r)


def _ffn_body(b0_ref, nb_ref, nu_ref, g_ref, wgu_ref, bgu_ref, wd_ref, bd_ref, xs_ref, ys_ref,
              xbuf, ybuf, in_sems, out_sems, wgu_bf, wd_bf):
    e = pl.program_id(0)
    blk_rows = MOE_BLOCK * ROW_TILE
    n_blocks = xs_ref.shape[0] // blk_rows
    first = b0_ref[e]
    n_used = nu_ref[0]

    def rows(b):
        return pl.ds(pl.multiple_of(b * blk_rows, blk_rows), blk_rows)

    def in_copy(b):
        return pltpu.make_async_copy(xs_ref.at[rows(b)], xbuf.at[b % 2], in_sems.at[b % 2])

    def out_copy(b):
        return pltpu.make_async_copy(ybuf.at[b % 2], ys_ref.at[rows(b)], out_sems.at[b % 2])

    @pl.when(e == 0)
    def _():
        in_copy(0).start(priority=1)

    wgu_bf[...] = wgu_ref[...].astype(BF16)
    wd_bf[...] = wd_ref[...].astype(BF16)

    def block(j, carry):
        b = first + j
        in_copy(b).wait()

        @pl.when(b + 1 < n_used)
        def _():
            in_copy(b + 1).start(priority=1)

        h = _rms(_load_row_tiles(xbuf.at[b % 2], MOE_BLOCK), g_ref[...]).astype(BF16)
        gu = jnp.dot(h, wgu_bf[...], preferred_element_type=F32) + bgu_ref[...]
        gate = jnp.minimum(gu[:, :D_FF], SWIGLU_LIMIT)
        up = jnp.clip(gu[:, D_FF:], -SWIGLU_LIMIT, SWIGLU_LIMIT)
        act = (up + 1.0) * gate * jax.nn.sigmoid(SWIGLU_ALPHA * gate)
        y = jnp.dot(act.astype(BF16), wd_bf[...], preferred_element_type=F32) + bd_ref[...]

        @pl.when(b >= 2)
        def _():
            out_copy(b - 2).wait()

        _store_row_tiles(ybuf.at[b % 2], y)
        out_copy(b).start(priority=1)
        return carry

    lax.fori_loop(0, nb_ref[e], block, 0)

    @pl.when(e == pl.num_programs(0) - 1)
    def _():
        @pl.when(n_used >= 2)
        def _():
            out_copy(n_used - 2).wait()
        out_copy(n_used - 1).wait()

        ybuf[0] = jnp.zeros((blk_rows, LANES), F32)

        def tail(act):
            def body(i, carry):
                act(pltpu.make_async_copy(
                    ybuf.at[0], ys_ref.at[pl.ds(pl.multiple_of(i * blk_rows, blk_rows), blk_rows)],
                    out_sems.at[0]))
                return carry
            lax.fori_loop(nu_ref[0], n_blocks, body, 0)

        tail(lambda c: c.start())
        tail(lambda c: c.wait())


def _moe_ffn(blk_first, blk_count, n_used, xs, norm2_g, w_gu, b_gu, w_d, b_d):
    blk_rows = MOE_BLOCK * ROW_TILE
    by_e = lambda e, b0, nb, nu: (e, 0, 0)
    grid_spec = pltpu.PrefetchScalarGridSpec(
        num_scalar_prefetch=3,
        grid=(N_EXPERTS,),
        in_specs=[
            pl.BlockSpec((1, D_MODEL), lambda e, b0, nb, nu: (0, 0)),
            pl.BlockSpec((None, D_MODEL, 2 * D_FF), by_e),
            pl.BlockSpec((None, 1, 2 * D_FF), by_e),
            pl.BlockSpec((None, D_FF, D_MODEL), by_e),
            pl.BlockSpec((None, 1, D_MODEL), by_e),
            pl.BlockSpec(memory_space=pl.ANY),
        ],
        out_specs=pl.BlockSpec(memory_space=pl.ANY),
        scratch_shapes=[pltpu.VMEM((2, blk_rows, LANES), F32), pltpu.VMEM((2, blk_rows, LANES), F32),
                        pltpu.SemaphoreType.DMA((2,)), pltpu.SemaphoreType.DMA((2,)),
                        pltpu.VMEM((D_MODEL, 2 * D_FF), BF16), pltpu.VMEM((D_FF, D_MODEL), BF16)],
    )
    return pl.pallas_call(
        _ffn_body,
        name="moe_ffn",
        grid_spec=grid_spec,
        out_shape=jax.ShapeDtypeStruct(xs.shape, F32),
        compiler_params=_cparams(("arbitrary",)),
    )(blk_first, blk_count, n_used, norm2_g, w_gu, b_gu, w_d, b_d, xs)


def _combine_body(dest_ref, x_ref, slab_ref, ys_ref, o_ref, buf, sems):
    i = pl.program_id(0)
    n_tiles = pl.num_programs(0) - 1
    tt = o_ref.shape[0]

    def gathers(slot):
        def issue(t):
            for k in range(TOP_K):
                d = dest_ref[0, 0, t * TOP_K + k]
                pltpu.make_async_copy(_row_tile(ys_ref, d), _row_tile(buf.at[slot, k], t),
                                      sems.at[slot]).start(priority=k % 2)
        _row_copies(tt, issue)

    def finish(slot):
        for k in range(TOP_K):
            pltpu.make_async_copy(ys_ref.at[pl.ds(0, tt * ROW_TILE)], buf.at[slot, k], sems.at[slot]).wait()
        acc = _load_row_tiles(x_ref, tt)
        for k in range(TOP_K):
            acc = acc + slab_ref[:, TOP_K + k:TOP_K + k + 1] * _load_row_tiles(buf.at[slot, k], tt)
        o_ref[...] = acc

    for slot in range(2):
        @pl.when((i < n_tiles) & (i % 2 == slot))
        def _():
            gathers(slot)

        @pl.when((i > 0) & ((i - 1) % 2 == slot))
        def _():
            finish(slot)


def _combine(x1r, slab, dest3, ys):
    t = x1r.shape[0] // ROW_TILE
    tt = dest3.shape[2] // TOP_K
    n_tiles = t // tt
    ahead = lambda i: (jnp.minimum(i, n_tiles - 1), 0, 0)
    behind = lambda i: (jnp.maximum(i - 1, 0), 0)
    return pl.pallas_call(
        _combine_body,
        name="combine",
        grid=(n_tiles + 1,),
        in_specs=[pl.BlockSpec((1, 1, tt * TOP_K), ahead, memory_space=pltpu.SMEM),
                  pl.BlockSpec((tt * ROW_TILE, LANES), behind),
                  pl.BlockSpec((tt, LANES), behind),
                  pl.BlockSpec(memory_space=pl.ANY)],
        out_specs=pl.BlockSpec((tt, D_MODEL), behind),
        out_shape=jax.ShapeDtypeStruct((t, D_MODEL), F32),
        scratch_shapes=[pltpu.VMEM((2, TOP_K, tt * ROW_TILE, LANES), F32), pltpu.SemaphoreType.DMA((2,))],
        compiler_params=_cparams(("arbitrary",)),
    )(dest3, x1r, slab, ys)


def _moe(x1, logits, norm2_g, w_gate_up, b_gate_up, w_down, b_down):
    t = x1.shape[0] // ROW_TILE
    n_assign = t * TOP_K
    n_blocks = -(-n_assign // MOE_BLOCK) + N_EXPERTS
    n_rows = n_blocks * MOE_BLOCK

    slab, cnt = _route(logits)
    ids = slab[:, :TOP_K].astype(jnp.int32)
    pos = slab[:, 2 * TOP_K:3 * TOP_K].astype(jnp.int32)
    counts = cnt[0, :N_EXPERTS].astype(jnp.int32)
    padded = (counts + MOE_BLOCK - 1) // MOE_BLOCK * MOE_BLOCK
    pad_ends = jnp.cumsum(padded)
    pad_starts = pad_ends - padded
    dest = pad_starts[ids] + pos
    n_used = (pad_ends[-1] // MOE_BLOCK).astype(jnp.int32).reshape(1)

    tt = min(TT_MOVE, t)
    dest3 = dest.reshape(t // tt, 1, tt * TOP_K)
    xs = _dispatch(pad_ends.astype(jnp.int32), n_used, x1, dest3, n_rows)
    ys = _moe_ffn((pad_starts // MOE_BLOCK).astype(jnp.int32), (padded // MOE_BLOCK).astype(jnp.int32),
                  n_used, xs, norm2_g,
                  w_gate_up, b_gate_up.reshape(N_EXPERTS, 1, 2 * D_FF),
                  w_down, b_down.reshape(N_EXPERTS, 1, D_MODEL))
    return _combine(x1, slab, dest3, ys)


def _layer(x, norm1_g, w_in, ssm_a_re, ssm_a_im, ssm_log_dt, ssm_b_re, ssm_b_im, ssm_c_re,
           ssm_c_im, ssm_d, w_glu, w_ssm_proj, q_norm_g, k_norm_g, attn_sinks, w_attn_proj,
           w_out, norm2_g, w_router, b_router, w_gate_up, b_gate_up, w_down, b_down):
    n_b, seq, d = x.shape
    t = n_b * seq
    x2 = x.reshape(t, d)
    u, q, kd, vd, gs, ga = _in_proj(x2, norm1_g.reshape(1, d), w_in, q_norm_g, k_norm_g)

    abar_re, abar_im, bbt_re, bbt_im = _s5_prep(ssm_a_re, ssm_a_im, ssm_log_dt, ssm_b_re, ssm_b_im)
    b_mat, c_mat, ab = _s5_matrices(abar_re, abar_im, bbt_re, bbt_im, ssm_c_re, ssm_c_im)
    y = _s5(u.reshape(n_b, seq, SSM_WIDTH), b_mat, c_mat, ab, ssm_d.reshape(1, SSM_WIDTH))

    a = _attention(q.reshape(n_b, seq, ATTN_WIDTH), kd.reshape(n_b, seq, KV_DUP),
                   vd.reshape(n_b, seq, KV_DUP), attn_sinks)

    w_r_pad = jnp.zeros((d, LANES), BF16).at[:, :N_EXPERTS].set(w_router.astype(BF16))
    b_r_pad = jnp.zeros((1, LANES), F32).at[0, :N_EXPERTS].set(b_router)
    norm2 = norm2_g.reshape(1, d)
    x1, logits = _mix(y.reshape(t, SSM_WIDTH), a.reshape(t, ATTN_WIDTH), gs, ga, x2,
                      w_glu.astype(BF16), w_ssm_proj.astype(BF16), w_attn_proj.astype(BF16),
                      w_out.astype(BF16), norm2, w_r_pad, b_r_pad)
    out = _moe(x1, logits, norm2, w_gate_up, b_gate_up, w_down, b_down)
    return out.reshape(n_b, seq, d)


def kernel(x, norm1_g, w_in, ssm_a_re, ssm_a_im, ssm_log_dt, ssm_b_re, ssm_b_im, ssm_c_re, ssm_c_im, ssm_d, w_glu, w_ssm_proj, q_norm_g, k_norm_g, attn_sinks, w_attn_proj, w_out, norm2_g, w_router, b_router, w_gate_up, b_gate_up, w_down, b_down):
    for layer in range(norm1_g.shape[0]):
        x = _layer(
            x, norm1_g[layer], w_in[layer], ssm_a_re[layer], ssm_a_im[layer], ssm_log_dt[layer],
            ssm_b_re[layer], ssm_b_im[layer], ssm_c_re[layer], ssm_c_im[layer], ssm_d[layer],
            w_glu[layer], w_ssm_proj[layer], q_norm_g[layer], k_norm_g[layer], attn_sinks[layer],
            w_attn_proj[layer], w_out[layer], norm2_g[layer], w_router[layer], b_router[layer],
            w_gate_up[layer], b_gate_up[layer], w_down[layer], b_down[layer])
    return x
```

```python
import functools

import jax
import jax.numpy as jnp
from jax import lax
from jax.experimental import pallas as pl
from jax.experimental.pallas import tpu as pltpu

D_MODEL = 1024
NORM_EPS = 1e-5
SSM_WIDTH = 1024
SSM_GROUP_CH = 16
SSM_GROUPS = SSM_WIDTH // SSM_GROUP_CH
SSM_STATE = 64
HEAD_DIM = 64
N_Q_HEADS = 16
N_KV_HEADS = 4
Q_PER_KV = N_Q_HEADS // N_KV_HEADS
ATTN_WIDTH = N_Q_HEADS * HEAD_DIM
KV_WIDTH = N_KV_HEADS * HEAD_DIM
WINDOW = 128
BLOCK = 128
N_EXPERTS = 32
TOP_K = 4
D_FF = 1024
SWIGLU_LIMIT = 7.0
SWIGLU_ALPHA = 1.702
MOE_BLOCK = 512

LANES = 128
SUBLANES = 8
VMEM_LIMIT = 56 * 1024 * 1024

S5_GROUP_BLOCK = 8
S5_CH = S5_GROUP_BLOCK * SSM_GROUP_CH
S5_ST = S5_GROUP_BLOCK * SSM_STATE
S5_CHUNK = 512
S5_SEG = 64
assert S5_CHUNK // S5_SEG >= 4 and (S5_CHUNK // S5_SEG) % 2 == 0
S5_PITCH = S5_CHUNK + SUBLANES

TM_IN = 256
TM_MIX = 512
TT_ROUTE = 512
TT_MOVE = 256

F32 = jnp.float32
BF16 = jnp.bfloat16


def _cparams(sem):
    return pltpu.CompilerParams(dimension_semantics=sem, vmem_limit_bytes=VMEM_LIMIT)


def _rms(x, g):
    return x * lax.rsqrt(jnp.mean(x * x, axis=-1, keepdims=True) + NORM_EPS) * g


KV_DUP = N_KV_HEADS * LANES


def _in_proj_body(x_ref, g_ref, w_ref, qg_ref, kg_ref, u_ref, q_ref, kd_ref, vd_ref, gs_ref, ga_ref):
    h = _rms(x_ref[...], g_ref[...]).astype(BF16)
    off = [0]

    def proj(width):
        lo = off[0]
        off[0] = lo + width
        return jnp.dot(h, w_ref[:, lo:lo + width], preferred_element_type=F32)

    u_ref[...] = proj(SSM_WIDTH)

    q = proj(ATTN_WIDTH)
    lo_sel = lax.broadcasted_iota(jnp.int32, (q.shape[0], LANES), 1) < HEAD_DIM
    for j in range(ATTN_WIDTH // LANES):
        qq = q[:, j * LANES:(j + 1) * LANES]
        sq = qq * qq
        ms_lo = jnp.sum(jnp.where(lo_sel, sq, 0.0), axis=-1, keepdims=True) / HEAD_DIM
        ms_hi = jnp.sum(jnp.where(lo_sel, 0.0, sq), axis=-1, keepdims=True) / HEAD_DIM
        r = jnp.where(lo_sel, lax.rsqrt(ms_lo + NORM_EPS), lax.rsqrt(ms_hi + NORM_EPS))
        q_ref[:, j * LANES:(j + 1) * LANES] = (qq * r * qg_ref[...]).astype(q_ref.dtype)

    kd = proj(KV_DUP)
    for kv in range(N_KV_HEADS):
        grp = slice(kv * LANES, (kv + 1) * LANES)
        kd_ref[:, grp] = _rms(kd[:, grp], kg_ref[...]).astype(kd_ref.dtype)
    vd_ref[...] = proj(KV_DUP).astype(vd_ref.dtype)
    gs_ref[...] = proj(D_MODEL)
    ga_ref[...] = proj(D_MODEL)


def _in_proj_weight(w_in):
    d = w_in.shape[0]
    cuts = (SSM_WIDTH, SSM_WIDTH + ATTN_WIDTH, SSM_WIDTH + ATTN_WIDTH + KV_WIDTH,
            SSM_WIDTH + ATTN_WIDTH + 2 * KV_WIDTH)
    w_uq, w_k, w_v, w_g = (w_in[:, :cuts[1]], w_in[:, cuts[1]:cuts[2]], w_in[:, cuts[2]:cuts[3]],
                           w_in[:, cuts[3]:])

    def twice(w):
        w = w.reshape(d, N_KV_HEADS, 1, HEAD_DIM)
        return jnp.concatenate([w, w], axis=2).reshape(d, KV_DUP)

    return jnp.concatenate([w_uq, twice(w_k), twice(w_v), w_g], axis=1).astype(BF16)


def _in_proj(x2, norm_g, w_in, q_norm_g, k_norm_g):
    t = x2.shape[0]
    w = _in_proj_weight(w_in)
    qg = (jnp.tile(q_norm_g, LANES // HEAD_DIM) * HEAD_DIM ** -0.5).reshape(1, LANES)
    kg = jnp.tile(k_norm_g, LANES // HEAD_DIM).reshape(1, LANES)
    outs = ((SSM_WIDTH, F32), (ATTN_WIDTH, BF16), (KV_DUP, BF16), (KV_DUP, BF16),
            (D_MODEL, F32), (D_MODEL, F32))
    const = lambda a: pl.BlockSpec(a.shape, lambda i: (0, 0))
    return pl.pallas_call(
        _in_proj_body,
        name="in_proj",
        grid=(t // TM_IN,),
        in_specs=[pl.BlockSpec((TM_IN, D_MODEL), lambda i: (i, 0)), const(norm_g), const(w),
                  const(qg), const(kg)],
        out_specs=[pl.BlockSpec((TM_IN, n), lambda i: (i, 0)) for n, _ in outs],
        out_shape=[jax.ShapeDtypeStruct((t, n), dt) for n, dt in outs],
        compiler_params=_cparams(("parallel",)),
    )(x2, norm_g, w, qg, kg)


def _s5_prep_body(are_ref, aim_ref, ldt_ref, bre_ref, bim_ref,
                  abr_ref, abi_ref, bbr_ref, bbi_ref):
    a_re = are_ref[...]
    a_im = aim_ref[...]
    dt = jnp.exp(ldt_ref[...])
    mag = jnp.exp(a_re * dt)
    abar_re = mag * jnp.cos(a_im * dt)
    abar_im = mag * jnp.sin(a_im * dt)
    den = a_re * a_re + a_im * a_im
    q_re = ((abar_re - 1.0) * a_re + abar_im * a_im) / den
    q_im = (abar_im * a_re - (abar_re - 1.0) * a_im) / den
    abr_ref[...] = abar_re
    abi_ref[...] = abar_im
    for h in range(SSM_GROUP_CH):
        bbr_ref[h] = q_re * bre_ref[h] - q_im * bim_ref[h]
        bbi_ref[h] = q_re * bim_ref[h] + q_im * bre_ref[h]


def _s5_prep(a_re, a_im, log_dt, b_re, b_im):
    g, p, hc = SSM_GROUPS, SSM_STATE, SSM_GROUP_CH
    bt_re = jnp.transpose(b_re, (2, 0, 1))
    bt_im = jnp.transpose(b_im, (2, 0, 1))
    return pl.pallas_call(
        _s5_prep_body,
        name="s5_prep",
        out_shape=[
            jax.ShapeDtypeStruct((g, p), F32),
            jax.ShapeDtypeStruct((g, p), F32),
            jax.ShapeDtypeStruct((hc, g, p), F32),
            jax.ShapeDtypeStruct((hc, g, p), F32),
        ],
    )(a_re, a_im, log_dt.reshape(g, 1), bt_re, bt_im)


def _s5_matrices(abar_re, abar_im, bbt_re, bbt_im, c_re, c_im):
    nb, gb, hc, p = SSM_GROUPS // S5_GROUP_BLOCK, S5_GROUP_BLOCK, SSM_GROUP_CH, SSM_STATE
    eye = jnp.eye(gb, dtype=F32)

    def b_block(bbt):
        t = jnp.transpose(bbt, (1, 0, 2)).reshape(nb, gb, hc, p)
        return jnp.einsum('jghp,gk->jghkp', t, eye).reshape(nb, gb * hc, gb * p)

    def c_block(c):
        t = c.reshape(nb, gb, hc, p)
        return jnp.einsum('jghp,gk->jgpkh', t, eye).reshape(nb, gb * p, gb * hc)

    b_mat = jnp.concatenate([b_block(bbt_re), b_block(bbt_im)], axis=2).astype(BF16)
    c_mat = jnp.concatenate([c_block(c_re), c_block(-c_im)], axis=1).astype(BF16)
    ab = jnp.concatenate([abar_re.reshape(nb, 1, gb * p), abar_im.reshape(nb, 1, gb * p)], axis=2)
    ab = jnp.broadcast_to(ab, (nb, SUBLANES, 2 * gb * p))
    return b_mat, c_mat, ab


def _s5_body(u_ref, b_ref, c_ref, ab_ref, d_ref, o_ref, ubuf, utm, bu0, bu1, st0, st1, state, ytm):
    n_b = u_ref.shape[0]
    chunk = u_ref.shape[1]

    @pl.when(pl.program_id(1) == 0)
    def _():
        state[...] = jnp.zeros_like(state)

    for b in range(n_b):
        ubuf[b * S5_PITCH:b * S5_PITCH + chunk, :] = u_ref[b]

    def to_time_major(l8, carry):
        for i in range(SUBLANES):
            l = l8 * SUBLANES + i
            row = pl.multiple_of(l * n_b, SUBLANES)
            utm[pl.ds(row, n_b), :] = ubuf[pl.ds(l, n_b, stride=S5_PITCH), :]
        return carry

    lax.fori_loop(0, chunk // SUBLANES, to_time_major, 0)

    seg_rows = S5_SEG * n_b
    n_seg = chunk // S5_SEG
    a_re = ab_ref[:, :S5_ST]
    a_im = ab_ref[:, S5_ST:]

    bu = (bu0, bu1)
    st = (st0, st1)

    def seg(s):
        return pl.ds(pl.multiple_of(s * seg_rows, seg_rows), seg_rows)

    def b_proj(s, par):
        bu[par][...] = jnp.dot(utm[seg(s), :].astype(BF16), b_ref[...], preferred_element_type=F32)

    def c_proj(s, par):
        y = jnp.dot(st[par][...], c_ref[...], preferred_element_type=F32)
        ytm[seg(s), :] = jax.nn.gelu(y + d_ref[...] * utm[seg(s), :])

    def recurrence(par, carry):
        s_re, s_im = carry
        for i in range(0, S5_SEG, 2):
            both = []
            for ii in (i, i + 1):
                rows = slice(ii * n_b, (ii + 1) * n_b)
                n_re = a_re * s_re - a_im * s_im + bu[par][rows, :S5_ST]
                n_im = a_re * s_im + a_im * s_re + bu[par][rows, S5_ST:]
                both.append(jnp.concatenate([n_re, n_im], axis=1))
                s_re, s_im = n_re, n_im
            st[par][i * n_b:(i + 2) * n_b, :] = jnp.concatenate(both, axis=0).astype(BF16)
        return s_re, s_im

    def stage(s, par, carry):
        b_proj(s + 1, 1 - par)
        c_proj(s - 1, 1 - par)
        return recurrence(par, carry)

    def stage_pair(p, carry):
        s = 2 * p + 1
        carry = stage(s, 1, carry)
        return stage(s + 1, 0, carry)

    b_proj(0, 0)
    b_proj(1, 1)
    carry = recurrence(0, (state[:, :S5_ST], state[:, S5_ST:]))
    carry = lax.fori_loop(0, (n_seg - 2) // 2, stage_pair, carry)
    c_proj(n_seg - 2, 0)
    s_re, s_im = recurrence(1, carry)
    c_proj(n_seg - 1, 1)
    state[:, :S5_ST] = s_re
    state[:, S5_ST:] = s_im

    for b in range(n_b):
        o_ref[b] = ytm[pl.ds(b, chunk, stride=n_b), :].astype(o_ref.dtype)


def _s5(u3, b_mat, c_mat, ab, d_flat):
    n_b, seq, _ = u3.shape
    assert n_b == SUBLANES, "the scan keeps one sequence per sublane"
    chunk = min(S5_CHUNK, seq)
    nb = SSM_GROUPS // S5_GROUP_BLOCK
    return pl.pallas_call(
        _s5_body,
        name="s5_scan",
        grid=(nb, seq // chunk),
        in_specs=[
            pl.BlockSpec((n_b, chunk, S5_CH), lambda j, c: (0, c, j)),
            pl.BlockSpec((None, S5_CH, 2 * S5_ST), lambda j, c: (j, 0, 0)),
            pl.BlockSpec((None, 2 * S5_ST, S5_CH), lambda j, c: (j, 0, 0)),
            pl.BlockSpec((None, SUBLANES, 2 * S5_ST), lambda j, c: (j, 0, 0)),
            pl.BlockSpec((1, S5_CH), lambda j, c: (0, j)),
        ],
        out_specs=pl.BlockSpec((n_b, chunk, S5_CH), lambda j, c: (0, c, j)),
        out_shape=jax.ShapeDtypeStruct((n_b, seq, SSM_WIDTH), BF16),
        scratch_shapes=[
            pltpu.VMEM((n_b * S5_PITCH, S5_CH), F32),
            pltpu.VMEM((chunk * n_b, S5_CH), F32),
            pltpu.VMEM((S5_SEG * n_b, 2 * S5_ST), F32),
            pltpu.VMEM((S5_SEG * n_b, 2 * S5_ST), F32),
            pltpu.VMEM((S5_SEG * n_b, 2 * S5_ST), BF16),
            pltpu.VMEM((S5_SEG * n_b, 2 * S5_ST), BF16),
            pltpu.VMEM((n_b, 2 * S5_ST), F32),
            pltpu.VMEM((chunk * n_b, S5_CH), F32),
        ],
        compiler_params=_cparams(("parallel", "arbitrary")),
    )(u3, b_mat, c_mat, ab, d_flat)


def _attn_body(sinks_ref, q_ref, kp_ref, kc_ref, vp_ref, vc_ref, bias_ref, half_ref, o_ref):
    lo_mask = half_ref[0:1, :]
    hi_mask = half_ref[1:2, :]
    lo_sel = lax.broadcasted_iota(jnp.int32, (BLOCK, LANES), 1) < HEAD_DIM
    for kv in range(N_KV_HEADS):
        grp = slice(kv * LANES, (kv + 1) * LANES)
        kw = jnp.concatenate([kp_ref[:, grp], kc_ref[:, grp]], axis=0)
        vw = jnp.concatenate([vp_ref[:, grp], vc_ref[:, grp]], axis=0)
        qs = []
        for pair in range(Q_PER_KV // 2):
            lanes = slice((2 * kv + pair) * LANES, (2 * kv + pair + 1) * LANES)
            qp = q_ref[:, lanes]
            qs += [qp * lo_mask, qp * hi_mask]
        q_st = jnp.concatenate(qs, axis=0)
        s = lax.dot_general(q_st, kw, (((1,), (1,)), ((), ())), preferred_element_type=F32)
        s = s + bias_ref[kv]
        ps = []
        for g in range(Q_PER_KV):
            sg = s[g * BLOCK:(g + 1) * BLOCK]
            sink = sinks_ref[kv * Q_PER_KV + g]
            m = jnp.maximum(jnp.max(sg, axis=-1, keepdims=True), sink)
            p = jnp.exp(sg - m)
            den = jnp.sum(p, axis=-1, keepdims=True) + jnp.exp(sink - m)
            ps.append((p / den).astype(BF16))
        p_st = jnp.concatenate(ps, axis=0)
        o = jnp.dot(p_st, vw, preferred_element_type=F32)
        for pair in range(Q_PER_KV // 2):
            even = o[(2 * pair) * BLOCK:(2 * pair + 1) * BLOCK]
            odd = o[(2 * pair + 1) * BLOCK:(2 * pair + 2) * BLOCK]
            lanes = slice((2 * kv + pair) * LANES, (2 * kv + pair + 1) * LANES)
            o_ref[:, lanes] = jnp.where(lo_sel, even, odd).astype(o_ref.dtype)


def _attn_bias():
    h = jnp.arange(1, N_Q_HEADS + 1, dtype=F32)
    slopes = jnp.exp2(-8.0 * h / N_Q_HEADS)
    qi = jnp.arange(BLOCK)[:, None]
    kj = jnp.arange(2 * BLOCK)[None, :]
    dist = qi - kj + BLOCK
    band = (dist >= 0) & (dist < WINDOW)
    bias = -slopes[:, None, None] * dist.astype(F32)
    later = jnp.where(band[None], bias, -jnp.inf)
    first = jnp.where((band & (kj >= BLOCK))[None], bias, -jnp.inf)
    both = jnp.stack([first, later])
    return both.reshape(2, N_KV_HEADS, Q_PER_KV * BLOCK, 2 * BLOCK)


def _attention(q3, kd3, vd3, sinks):
    n_b, seq, _ = q3.shape
    n_blk = seq // BLOCK
    kv_dup = N_KV_HEADS * LANES
    prev = lambda b, i: (b, jnp.maximum(i - 1, 0), 0)
    cur = lambda b, i: (b, i, 0)
    lane = jnp.arange(LANES)
    half = jnp.stack([lane < HEAD_DIM, lane >= HEAD_DIM]).astype(BF16)
    return pl.pallas_call(
        _attn_body,
        name="swa_attention",
        grid=(n_b, n_blk),
        in_specs=[
            pl.BlockSpec(memory_space=pltpu.SMEM),
            pl.BlockSpec((None, BLOCK, ATTN_WIDTH), cur),
            pl.BlockSpec((None, BLOCK, kv_dup), prev),
            pl.BlockSpec((None, BLOCK, kv_dup), cur),
            pl.BlockSpec((None, BLOCK, kv_dup), prev),
            pl.BlockSpec((None, BLOCK, kv_dup), cur),
            pl.BlockSpec((None, N_KV_HEADS, Q_PER_KV * BLOCK, 2 * BLOCK),
                         lambda b, i: (jnp.minimum(i, 1), 0, 0, 0)),
            pl.BlockSpec((2, LANES), lambda b, i: (0, 0)),
        ],
        out_specs=pl.BlockSpec((None, BLOCK, ATTN_WIDTH), cur),
        out_shape=jax.ShapeDtypeStruct((n_b, seq, ATTN_WIDTH), BF16),
        compiler_params=_cparams(("parallel", "arbitrary")),
    )(sinks, q3, kd3, kd3, vd3, vd3, _attn_bias(), half)


ROW_TILE = D_MODEL // LANES
assert ROW_TILE == SUBLANES


def _store_row_tiles(ref, x):
    for i in range(x.shape[0] // SUBLANES):
        for j in range(ROW_TILE):
            ref[pl.ds(i * SUBLANES * ROW_TILE + j, SUBLANES, stride=ROW_TILE), :] = (
                x[i * SUBLANES:(i + 1) * SUBLANES, j * LANES:(j + 1) * LANES])


def _load_row_tiles(ref, n):
    cols = []
    for j in range(ROW_TILE):
        cols.append(jnp.concatenate(
            [ref[pl.ds(i * SUBLANES * ROW_TILE + j, SUBLANES, stride=ROW_TILE), :]
             for i in range(n // SUBLANES)], axis=0))
    return jnp.concatenate(cols, axis=1)


def _mix_body(y_ref, a_ref, gs_ref, ga_ref, x_ref, wglu_ref, wsp_ref, wap_ref, wo_ref,
              g2_ref, wr_ref, br_ref, x1_ref, lg_ref):
    z = jnp.dot(y_ref[...], wglu_ref[...], preferred_element_type=F32)
    glu = z[:, :SSM_WIDTH] * jax.nn.sigmoid(z[:, SSM_WIDTH:])
    y_ssm = jnp.dot(glu.astype(BF16), wsp_ref[...], preferred_element_type=F32)
    y_attn = jnp.dot(a_ref[...], wap_ref[...], preferred_element_type=F32)
    mixed = jax.nn.sigmoid(gs_ref[...]) * y_ssm + jax.nn.sigmoid(ga_ref[...]) * y_attn
    x1 = x_ref[...] + jnp.dot(mixed.astype(BF16), wo_ref[...], preferred_element_type=F32)
    _store_row_tiles(x1_ref, x1)
    h2 = _rms(x1, g2_ref[...]).astype(BF16)
    lg_ref[...] = jnp.dot(h2, wr_ref[...], preferred_element_type=F32) + br_ref[...]


def _mix(y2, a2, gs, ga, x2, w_glu, w_sp, w_ap, w_o, norm2_g, w_r_pad, b_r_pad):
    t = x2.shape[0]
    tm = min(TM_MIX, t)
    tok = lambda w: pl.BlockSpec((tm, w), lambda i: (i, 0))
    full = lambda a: pl.BlockSpec(a.shape, lambda i: (0, 0))
    return pl.pallas_call(
        _mix_body,
        name="mix",
        grid=(t // tm,),
        in_specs=[tok(SSM_WIDTH), tok(ATTN_WIDTH), tok(D_MODEL), tok(D_MODEL), tok(D_MODEL),
                  full(w_glu), full(w_sp), full(w_ap), full(w_o), full(norm2_g),
                  full(w_r_pad), full(b_r_pad)],
        out_specs=[pl.BlockSpec((tm * ROW_TILE, LANES), lambda i: (i, 0)), tok(LANES)],
        out_shape=[jax.ShapeDtypeStruct((t * ROW_TILE, LANES), F32),
                   jax.ShapeDtypeStruct((t, LANES), F32)],
        compiler_params=_cparams(("parallel",)),
    )(y2, a2, gs, ga, x2, w_glu, w_sp, w_ap, w_o, norm2_g, w_r_pad, b_r_pad)


def _route_body(lg_ref, tri_ref, slab_ref, cnt_ref, base):
    @pl.when(pl.program_id(0) == 0)
    def _():
        base[...] = jnp.zeros_like(base)

    tt = lg_ref.shape[0]
    lane = lax.broadcasted_iota(jnp.int32, (tt, LANES), 1)
    l = jnp.where(lane < N_EXPERTS, lg_ref[...], -jnp.inf)
    tops, hots = [], []
    for _ in range(TOP_K):
        m = jnp.max(l, axis=-1, keepdims=True)
        idx = jnp.min(jnp.where(l == m, lane, LANES), axis=-1, keepdims=True)
        hot = lane == idx
        tops.append((m, idx))
        hots.append(hot)
        l = jnp.where(hot, -jnp.inf, l)
    es = [jnp.exp(m - tops[0][0]) for m, _ in tops]
    den = es[0] + es[1] + es[2] + es[3]
    member = jnp.zeros((tt, LANES), F32)
    for hot in hots:
        member = member + jnp.where(hot, 1.0, 0.0)
    before = jnp.dot(tri_ref[...], member.astype(BF16), preferred_element_type=F32) + base[...]
    slab = jnp.zeros((tt, LANES), F32)
    for k in range(TOP_K):
        pos = jnp.sum(jnp.where(hots[k], before, 0.0), axis=-1, keepdims=True)
        slab = jnp.where(lane == k, tops[k][1].astype(F32), slab)
        slab = jnp.where(lane == TOP_K + k, es[k] / den, slab)
        slab = jnp.where(lane == 2 * TOP_K + k, pos, slab)
    slab_ref[...] = slab
    base[...] = base[...] + jnp.sum(member, axis=0, keepdims=True)
    cnt_ref[...] = base[...]


def _route(logits):
    t = logits.shape[0]
    tt = min(TT_ROUTE, t)
    tri = jnp.tril(jnp.ones((tt, tt), F32), -1).astype(BF16)
    return pl.pallas_call(
        _route_body,
        name="route",
        grid=(t // tt,),
        in_specs=[pl.BlockSpec((tt, LANES), lambda i: (i, 0)),
                  pl.BlockSpec((tt, tt), lambda i: (0, 0))],
        out_specs=[pl.BlockSpec((tt, LANES), lambda i: (i, 0)),
                   pl.BlockSpec((1, LANES), lambda i: (0, 0))],
        out_shape=[jax.ShapeDtypeStruct((t, LANES), F32), jax.ShapeDtypeStruct((1, LANES), F32)],
        scratch_shapes=[pltpu.VMEM((1, LANES), F32)],
        compiler_params=_cparams(("arbitrary",)),
    )(logits, tri)


def _row_copies(n_tok, issue_one):
    def body(t8, carry):
        for i in range(SUBLANES):
            issue_one(t8 * SUBLANES + i)
        return carry
    lax.fori_loop(0, n_tok // SUBLANES, body, 0)


def _row_tile(ref, row):
    return ref.at[pl.ds(pl.multiple_of(row * ROW_TILE, ROW_TILE), ROW_TILE)]


def _dispatch_body(pe_ref, nu_ref, dest_ref, x_ref, xs_ref, zbuf, sem, zsem):
    tt = x_ref.shape[0] // ROW_TILE
    blk_rows = MOE_BLOCK * ROW_TILE
    n_blocks = xs_ref.shape[0] // blk_rows

    @pl.when(pl.program_id(0) == 0)
    def _():
        zbuf[...] = jnp.zeros_like(zbuf)

        def zero_block(row):
            return pltpu.make_async_copy(
                zbuf, xs_ref.at[pl.ds(pl.multiple_of(row * ROW_TILE, blk_rows), blk_rows)], zsem)

        def each(act):
            for e in range(N_EXPERTS):
                @pl.when(pe_ref[e] >= MOE_BLOCK)
                def _():
                    act(zero_block(pe_ref[e] - MOE_BLOCK))

            def tail(i, carry):
                act(zero_block(i * MOE_BLOCK))
                return carry
            lax.fori_loop(nu_ref[0], n_blocks, tail, 0)

        each(lambda c: c.start())
        each(lambda c: c.wait())

    def issue(t):
        for k in range(TOP_K):
            d = dest_ref[0, 0, t * TOP_K + k]
            pltpu.make_async_copy(_row_tile(x_ref, t), _row_tile(xs_ref, d), sem).start(priority=k % 2)

    _row_copies(tt, issue)
    for _ in range(TOP_K):
        pltpu.make_async_copy(x_ref, xs_ref.at[pl.ds(0, tt * ROW_TILE)], sem).wait()


def _dispatch(pad_ends, n_used, x1r, dest3, n_rows):
    t = x1r.shape[0] // ROW_TILE
    tt = dest3.shape[2] // TOP_K
    grid_spec = pltpu.PrefetchScalarGridSpec(
        num_scalar_prefetch=2,
        grid=(t // tt,),
        in_specs=[pl.BlockSpec((1, 1, tt * TOP_K), lambda i, pe, nu: (i, 0, 0), memory_space=pltpu.SMEM),
                  pl.BlockSpec((tt * ROW_TILE, LANES), lambda i, pe, nu: (i, 0))],
        out_specs=pl.BlockSpec(memory_space=pl.ANY),
        scratch_shapes=[pltpu.VMEM((MOE_BLOCK * ROW_TILE, LANES), F32),
                        pltpu.SemaphoreType.DMA(()), pltpu.SemaphoreType.DMA(())],
    )
    return pl.pallas_call(
        _dispatch_body,
        name="dispatch",
        grid_spec=grid_spec,
        out_shape=jax.ShapeDtypeStruct((n_rows * ROW_TILE, LANES), F32),
        compiler_params=_cparams(("arbitrary",)),
    )(pad_ends, n_used, dest3, x1r)


def _ffn_body(b0_ref, nb_ref, nu_ref, g_ref, wgu_ref, bgu_ref, wd_ref, bd_ref, xs_ref, ys_ref,
              xbuf, ybuf, in_sems, out_sems, wgu_bf, wd_bf):
    e = pl.program_id(0)
    blk_rows = MOE_BLOCK * ROW_TILE
    n_blocks = xs_ref.shape[0] // blk_rows
    first = b0_ref[e]
    n_used = nu_ref[0]

    def rows(b):
        return pl.ds(pl.multiple_of(b * blk_rows, blk_rows), blk_rows)

    def in_copy(b):
        return pltpu.make_async_copy(xs_ref.at[rows(b)], xbuf.at[b % 2], in_sems.at[b % 2])

    def out_copy(b):
        return pltpu.make_async_copy(ybuf.at[b % 2], ys_ref.at[rows(b)], out_sems.at[b % 2])

    @pl.when(e == 0)
    def _():
        in_copy(0).start(priority=1)

    wgu_bf[...] = wgu_ref[...].astype(BF16)
    wd_bf[...] = wd_ref[...].astype(BF16)

    def block(j, carry):
        b = first + j
        in_copy(b).wait()

        @pl.when(b + 1 < n_used)
        def _():
            in_copy(b + 1).start(priority=1)

        h = _rms(_load_row_tiles(xbuf.at[b % 2], MOE_BLOCK), g_ref[...]).astype(BF16)
        gu = jnp.dot(h, wgu_bf[...], preferred_element_type=F32) + bgu_ref[...]
        gate = jnp.minimum(gu[:, :D_FF], SWIGLU_LIMIT)
        up = jnp.clip(gu[:, D_FF:], -SWIGLU_LIMIT, SWIGLU_LIMIT)
        act = (up + 1.0) * gate * jax.nn.sigmoid(SWIGLU_ALPHA * gate)
        y = jnp.dot(act.astype(BF16), wd_bf[...], preferred_element_type=F32) + bd_ref[...]

        @pl.when(b >= 2)
        def _():
            out_copy(b - 2).wait()

        _store_row_tiles(ybuf.at[b % 2], y)
        out_copy(b).start(priority=1)
        return carry

    lax.fori_loop(0, nb_ref[e], block, 0)

    @pl.when(e == pl.num_programs(0) - 1)
    def _():
        @pl.when(n_used >= 2)
        def _():
            out_copy(n_used - 2).wait()
        out_copy(n_used - 1).wait()

        ybuf[0] = jnp.zeros((blk_rows, LANES), F32)

        def tail(act):
            def body(i, carry):
                act(pltpu.make_async_copy(
                    ybuf.at[0], ys_ref.at[pl.ds(pl.multiple_of(i * blk_rows, blk_rows), blk_rows)],
                    out_sems.at[0]))
                return carry
            lax.fori_loop(nu_ref[0], n_blocks, body, 0)

        tail(lambda c: c.start())
        tail(lambda c: c.wait())


def _moe_ffn(blk_first, blk_count, n_used, xs, norm2_g, w_gu, b_gu, w_d, b_d):
    blk_rows = MOE_BLOCK * ROW_TILE
    by_e = lambda e, b0, nb, nu: (e, 0, 0)
    grid_spec = pltpu.PrefetchScalarGridSpec(
        num_scalar_prefetch=3,
        grid=(N_EXPERTS,),
        in_specs=[
            pl.BlockSpec((1, D_MODEL), lambda e, b0, nb, nu: (0, 0)),
            pl.BlockSpec((None, D_MODEL, 2 * D_FF), by_e),
            pl.BlockSpec((None, 1, 2 * D_FF), by_e),
            pl.BlockSpec((None, D_FF, D_MODEL), by_e),
            pl.BlockSpec((None, 1, D_MODEL), by_e),
            pl.BlockSpec(memory_space=pl.ANY),
        ],
        out_specs=pl.BlockSpec(memory_space=pl.ANY),
        scratch_shapes=[pltpu.VMEM((2, blk_rows, LANES), F32), pltpu.VMEM((2, blk_rows, LANES), F32),
                        pltpu.SemaphoreType.DMA((2,)), pltpu.SemaphoreType.DMA((2,)),
                        pltpu.VMEM((D_MODEL, 2 * D_FF), BF16), pltpu.VMEM((D_FF, D_MODEL), BF16)],
    )
    return pl.pallas_call(
        _ffn_body,
        name="moe_ffn",
        grid_spec=grid_spec,
        out_shape=jax.ShapeDtypeStruct(xs.shape, F32),
        compiler_params=_cparams(("arbitrary",)),
    )(blk_first, blk_count, n_used, norm2_g, w_gu, b_gu, w_d, b_d, xs)


def _combine_body(dest_ref, x_ref, slab_ref, ys_ref, o_ref, buf, sems):
    i = pl.program_id(0)
    n_tiles = pl.num_programs(0) - 1
    tt = o_ref.shape[0]

    def gathers(slot):
        def issue(t):
            for k in range(TOP_K):
                d = dest_ref[0, 0, t * TOP_K + k]
                pltpu.make_async_copy(_row_tile(ys_ref, d), _row_tile(buf.at[slot, k], t),
                                      sems.at[slot]).start(priority=k % 2)
        _row_copies(tt, issue)

    def finish(slot):
        for k in range(TOP_K):
            pltpu.make_async_copy(ys_ref.at[pl.ds(0, tt * ROW_TILE)], buf.at[slot, k], sems.at[slot]).wait()
        acc = _load_row_tiles(x_ref, tt)
        for k in range(TOP_K):
            acc = acc + slab_ref[:, TOP_K + k:TOP_K + k + 1] * _load_row_tiles(buf.at[slot, k], tt)
        o_ref[...] = acc

    for slot in range(2):
        @pl.when((i < n_tiles) & (i % 2 == slot))
        def _():
            gathers(slot)

        @pl.when((i > 0) & ((i - 1) % 2 == slot))
        def _():
            finish(slot)


def _combine(x1r, slab, dest3, ys):
    t = x1r.shape[0] // ROW_TILE
    tt = dest3.shape[2] // TOP_K
    n_tiles = t // tt
    ahead = lambda i: (jnp.minimum(i, n_tiles - 1), 0, 0)
    behind = lambda i: (jnp.maximum(i - 1, 0), 0)
    return pl.pallas_call(
        _combine_body,
        name="combine",
        grid=(n_tiles + 1,),
        in_specs=[pl.BlockSpec((1, 1, tt * TOP_K), ahead, memory_space=pltpu.SMEM),
                  pl.BlockSpec((tt * ROW_TILE, LANES), behind),
                  pl.BlockSpec((tt, LANES), behind),
                  pl.BlockSpec(memory_space=pl.ANY)],
        out_specs=pl.BlockSpec((tt, D_MODEL), behind),
        out_shape=jax.ShapeDtypeStruct((t, D_MODEL), F32),
        scratch_shapes=[pltpu.VMEM((2, TOP_K, tt * ROW_TILE, LANES), F32), pltpu.SemaphoreType.DMA((2,))],
        compiler_params=_cparams(("arbitrary",)),
    )(dest3, x1r, slab, ys)


def _moe(x1, logits, norm2_g, w_gate_up, b_gate_up, w_down, b_down):
    t = x1.shape[0] // ROW_TILE
    n_assign = t * TOP_K
    n_blocks = -(-n_assign // MOE_BLOCK) + N_EXPERTS
    n_rows = n_blocks * MOE_BLOCK

    slab, cnt = _route(logits)
    ids = slab[:, :TOP_K].astype(jnp.int32)
    pos = slab[:, 2 * TOP_K:3 * TOP_K].astype(jnp.int32)
    counts = cnt[0, :N_EXPERTS].astype(jnp.int32)
    padded = (counts + MOE_BLOCK - 1) // MOE_BLOCK * MOE_BLOCK
    pad_ends = jnp.cumsum(padded)
    pad_starts = pad_ends - padded
    dest = pad_starts[ids] + pos
    n_used = (pad_ends[-1] // MOE_BLOCK).astype(jnp.int32).reshape(1)

    tt = min(TT_MOVE, t)
    dest3 = dest.reshape(t // tt, 1, tt * TOP_K)
    xs = _dispatch(pad_ends.astype(jnp.int32), n_used, x1, dest3, n_rows)
    ys = _moe_ffn((pad_starts // MOE_BLOCK).astype(jnp.int32), (padded // MOE_BLOCK).astype(jnp.int32),
                  n_used, xs, norm2_g,
                  w_gate_up, b_gate_up.reshape(N_EXPERTS, 1, 2 * D_FF),
                  w_down, b_down.reshape(N_EXPERTS, 1, D_MODEL))
    return _combine(x1, slab, dest3, ys)


def _layer(x, norm1_g, w_in, ssm_a_re, ssm_a_im, ssm_log_dt, ssm_b_re, ssm_b_im, ssm_c_re,
           ssm_c_im, ssm_d, w_glu, w_ssm_proj, q_norm_g, k_norm_g, attn_sinks, w_attn_proj,
           w_out, norm2_g, w_router, b_router, w_gate_up, b_gate_up, w_down, b_down):
    n_b, seq, d = x.shape
    t = n_b * seq
    x2 = x.reshape(t, d)
    u, q, kd, vd, gs, ga = _in_proj(x2, norm1_g.reshape(1, d), w_in, q_norm_g, k_norm_g)

    abar_re, abar_im, bbt_re, bbt_im = _s5_prep(ssm_a_re, ssm_a_im, ssm_log_dt, ssm_b_re, ssm_b_im)
    b_mat, c_mat, ab = _s5_matrices(abar_re, abar_im, bbt_re, bbt_im, ssm_c_re, ssm_c_im)
    y = _s5(u.reshape(n_b, seq, SSM_WIDTH), b_mat, c_mat, ab, ssm_d.reshape(1, SSM_WIDTH))

    a = _attention(q.reshape(n_b, seq, ATTN_WIDTH), kd.reshape(n_b, seq, KV_DUP),
                   vd.reshape(n_b, seq, KV_DUP), attn_sinks)

    w_r_pad = jnp.zeros((d, LANES), BF16).at[:, :N_EXPERTS].set(w_router.astype(BF16))
    b_r_pad = jnp.zeros((1, LANES), F32).at[0, :N_EXPERTS].set(b_router)
    norm2 = norm2_g.reshape(1, d)
    x1, logits = _mix(y.reshape(t, SSM_WIDTH), a.reshape(t, ATTN_WIDTH), gs, ga, x2,
                      w_glu.astype(BF16), w_ssm_proj.astype(BF16), w_attn_proj.astype(BF16),
                      w_out.astype(BF16), norm2, w_r_pad, b_r_pad)
    out = _moe(x1, logits, norm2, w_gate_up, b_gate_up, w_down, b_down)
    return out.reshape(n_b, seq, d)


def kernel(x, norm1_g, w_in, ssm_a_re, ssm_a_im, ssm_log_dt, ssm_b_re, ssm_b_im, ssm_c_re, ssm_c_im, ssm_d, w_glu, w_ssm_proj, q_norm_g, k_norm_g, attn_sinks, w_attn_proj, w_out, norm2_g, w_router, b_router, w_gate_up, b_gate_up, w_down, b_down):
    for layer in range(norm1_g.shape[0]):
        x = _layer(
            x, norm1_g[layer], w_in[layer], ssm_a_re[layer], ssm_a_im[layer], ssm_log_dt[layer],
            ssm_b_re[layer], ssm_b_im[layer], ssm_c_re[layer], ssm_c_im[layer], ssm_d[layer],
            w_glu[layer], w_ssm_proj[layer], q_norm_g[layer], k_norm_g[layer], attn_sinks[layer],
            w_attn_proj[layer], w_out[layer], norm2_g[layer], w_router[layer], b_router[layer],
            w_gate_up[layer], b_gate_up[layer], w_down[layer], b_down[layer])
    return x
```

```python
import functools

import jax
import jax.numpy as jnp
from jax import lax
from jax.experimental import pallas as pl
from jax.experimental.pallas import tpu as pltpu

D_MODEL = 1024
NORM_EPS = 1e-5
SSM_WIDTH = 1024
SSM_GROUP_CH = 16
SSM_GROUPS = SSM_WIDTH // SSM_GROUP_CH
SSM_STATE = 64
HEAD_DIM = 64
N_Q_HEADS = 16
N_KV_HEADS = 4
Q_PER_KV = N_Q_HEADS // N_KV_HEADS
ATTN_WIDTH = N_Q_HEADS * HEAD_DIM
KV_WIDTH = N_KV_HEADS * HEAD_DIM
WINDOW = 128
BLOCK = 128
N_EXPERTS = 32
TOP_K = 4
D_FF = 1024
SWIGLU_LIMIT = 7.0
SWIGLU_ALPHA = 1.702
MOE_BLOCK = 512

LANES = 128
SUBLANES = 8
VMEM_LIMIT = 56 * 1024 * 1024

S5_GROUP_BLOCK = 8
S5_CH = S5_GROUP_BLOCK * SSM_GROUP_CH
S5_ST = S5_GROUP_BLOCK * SSM_STATE
S5_CHUNK = 512
S5_SEG = 64
assert S5_CHUNK // S5_SEG >= 4 and (S5_CHUNK // S5_SEG) % 2 == 0
S5_PITCH = S5_CHUNK + SUBLANES

TM_IN = 256
TM_MIX = 512
TT_ROUTE = 512
TT_MOVE = 256

F32 = jnp.float32
BF16 = jnp.bfloat16


def _cparams(sem):
    return pltpu.CompilerParams(dimension_semantics=sem, vmem_limit_bytes=VMEM_LIMIT)


def _rms(x, g):
    return x * lax.rsqrt(jnp.mean(x * x, axis=-1, keepdims=True) + NORM_EPS) * g


KV_DUP = N_KV_HEADS * LANES


def _in_proj_body(x_ref, g_ref, w_ref, qg_ref, kg_ref, u_ref, q_ref, kd_ref, vd_ref, gs_ref, ga_ref):
    h = _rms(x_ref[...], g_ref[...]).astype(BF16)
    off = [0]

    def proj(width):
        lo = off[0]
        off[0] = lo + width
        return jnp.dot(h, w_ref[:, lo:lo + width], preferred_element_type=F32)

    u_ref[...] = proj(SSM_WIDTH)

    q = proj(ATTN_WIDTH)
    lo_sel = lax.broadcasted_iota(jnp.int32, (q.shape[0], LANES), 1) < HEAD_DIM
    for j in range(ATTN_WIDTH // LANES):
        qq = q[:, j * LANES:(j + 1) * LANES]
        sq = qq * qq
        ms_lo = jnp.sum(jnp.where(lo_sel, sq, 0.0), axis=-1, keepdims=True) / HEAD_DIM
        ms_hi = jnp.sum(jnp.where(lo_sel, 0.0, sq), axis=-1, keepdims=True) / HEAD_DIM
        r = jnp.where(lo_sel, lax.rsqrt(ms_lo + NORM_EPS), lax.rsqrt(ms_hi + NORM_EPS))
        q_ref[:, j * LANES:(j + 1) * LANES] = (qq * r * qg_ref[...]).astype(q_ref.dtype)

    kd = proj(KV_DUP)
    for kv in range(N_KV_HEADS):
        grp = slice(kv * LANES, (kv + 1) * LANES)
        kd_ref[:, grp] = _rms(kd[:, grp], kg_ref[...]).astype(kd_ref.dtype)
    vd_ref[...] = proj(KV_DUP).astype(vd_ref.dtype)
    gs_ref[...] = proj(D_MODEL)
    ga_ref[...] = proj(D_MODEL)


def _in_proj_weight(w_in):
    d = w_in.shape[0]
    cuts = (SSM_WIDTH, SSM_WIDTH + ATTN_WIDTH, SSM_WIDTH + ATTN_WIDTH + KV_WIDTH,
            SSM_WIDTH + ATTN_WIDTH + 2 * KV_WIDTH)
    w_uq, w_k, w_v, w_g = (w_in[:, :cuts[1]], w_in[:, cuts[1]:cuts[2]], w_in[:, cuts[2]:cuts[3]],
                           w_in[:, cuts[3]:])

    def twice(w):
        w = w.reshape(d, N_KV_HEADS, 1, HEAD_DIM)
        return jnp.concatenate([w, w], axis=2).reshape(d, KV_DUP)

    return jnp.concatenate([w_uq, twice(w_k), twice(w_v), w_g], axis=1).astype(BF16)


def _in_proj(x2, norm_g, w_in, q_norm_g, k_norm_g):
    t = x2.shape[0]
    w = _in_proj_weight(w_in)
    qg = (jnp.tile(q_norm_g, LANES // HEAD_DIM) * HEAD_DIM ** -0.5).reshape(1, LANES)
    kg = jnp.tile(k_norm_g, LANES // HEAD_DIM).reshape(1, LANES)
    outs = ((SSM_WIDTH, F32), (ATTN_WIDTH, BF16), (KV_DUP, BF16), (KV_DUP, BF16),
            (D_MODEL, F32), (D_MODEL, F32))
    const = lambda a: pl.BlockSpec(a.shape, lambda i: (0, 0))
    return pl.pallas_call(
        _in_proj_body,
        name="in_proj",
        grid=(t // TM_IN,),
        in_specs=[pl.BlockSpec((TM_IN, D_MODEL), lambda i: (i, 0)), const(norm_g), const(w),
                  const(qg), const(kg)],
        out_specs=[pl.BlockSpec((TM_IN, n), lambda i: (i, 0)) for n, _ in outs],
        out_shape=[jax.ShapeDtypeStruct((t, n), dt) for n, dt in outs],
        compiler_params=_cparams(("parallel",)),
    )(x2, norm_g, w, qg, kg)


def _s5_prep_body(are_ref, aim_ref, ldt_ref, bre_ref, bim_ref,
                  abr_ref, abi_ref, bbr_ref, bbi_ref):
    a_re = are_ref[...]
    a_im = aim_ref[...]
    dt = jnp.exp(ldt_ref[...])
    mag = jnp.exp(a_re * dt)
    abar_re = mag * jnp.cos(a_im * dt)
    abar_im = mag * jnp.sin(a_im * dt)
    den = a_re * a_re + a_im * a_im
    q_re = ((abar_re - 1.0) * a_re + abar_im * a_im) / den
    q_im = (abar_im * a_re - (abar_re - 1.0) * a_im) / den
    abr_ref[...] = abar_re
    abi_ref[...] = abar_im
    for h in range(SSM_GROUP_CH):
        bbr_ref[h] = q_re * bre_ref[h] - q_im * bim_ref[h]
        bbi_ref[h] = q_re * bim_ref[h] + q_im * bre_ref[h]


def _s5_prep(a_re, a_im, log_dt, b_re, b_im):
    g, p, hc = SSM_GROUPS, SSM_STATE, SSM_GROUP_CH
    bt_re = jnp.transpose(b_re, (2, 0, 1))
    bt_im = jnp.transpose(b_im, (2, 0, 1))
    return pl.pallas_call(
        _s5_prep_body,
        name="s5_prep",
        out_shape=[
            jax.ShapeDtypeStruct((g, p), F32),
            jax.ShapeDtypeStruct((g, p), F32),
            jax.ShapeDtypeStruct((hc, g, p), F32),
            jax.ShapeDtypeStruct((hc, g, p), F32),
        ],
    )(a_re, a_im, log_dt.reshape(g, 1), bt_re, bt_im)


def _s5_matrices(abar_re, abar_im, bbt_re, bbt_im, c_re, c_im):
    nb, gb, hc, p = SSM_GROUPS // S5_GROUP_BLOCK, S5_GROUP_BLOCK, SSM_GROUP_CH, SSM_STATE
    eye = jnp.eye(gb, dtype=F32)

    def b_block(bbt):
        t = jnp.transpose(bbt, (1, 0, 2)).reshape(nb, gb, hc, p)
        return jnp.einsum('jghp,gk->jghkp', t, eye).reshape(nb, gb * hc, gb * p)

    def c_block(c):
        t = c.reshape(nb, gb, hc, p)
        return jnp.einsum('jghp,gk->jgpkh', t, eye).reshape(nb, gb * p, gb * hc)

    b_mat = jnp.concatenate([b_block(bbt_re), b_block(bbt_im)], axis=2).astype(BF16)
    c_mat = jnp.concatenate([c_block(c_re), c_block(-c_im)], axis=1).astype(BF16)
    ab = jnp.concatenate([abar_re.reshape(nb, 1, gb * p), abar_im.reshape(nb, 1, gb * p)], axis=2)
    ab = jnp.broadcast_to(ab, (nb, SUBLANES, 2 * gb * p))
    return b_mat, c_mat, ab


def _s5_body(u_ref, b_ref, c_ref, ab_ref, d_ref, o_ref, ubuf, utm, bu0, bu1, st0, st1, state, ytm):
    n_b = u_ref.shape[0]
    chunk = u_ref.shape[1]

    @pl.when(pl.program_id(1) == 0)
    def _():
        state[...] = jnp.zeros_like(state)

    for b in range(n_b):
        ubuf[b * S5_PITCH:b * S5_PITCH + chunk, :] = u_ref[b]

    def to_time_major(l8, carry):
        for i in range(SUBLANES):
            l = l8 * SUBLANES + i
            row = pl.multiple_of(l * n_b, SUBLANES)
            utm[pl.ds(row, n_b), :] = ubuf[pl.ds(l, n_b, stride=S5_PITCH), :]
        return carry

    lax.fori_loop(0, chunk // SUBLANES, to_time_major, 0)

    seg_rows = S5_SEG * n_b
    n_seg = chunk // S5_SEG
    a_re = ab_ref[:, :S5_ST]
    a_im = ab_ref[:, S5_ST:]

    bu = (bu0, bu1)
    st = (st0, st1)

    def seg(s):
        return pl.ds(pl.multiple_of(s * seg_rows, seg_rows), seg_rows)

    def b_proj(s, par):
        bu[par][...] = jnp.dot(utm[seg(s), :].astype(BF16), b_ref[...], preferred_element_type=F32)

    def c_proj(s, par):
        y = jnp.dot(st[par][...], c_ref[...], preferred_element_type=F32)
        ytm[seg(s), :] = jax.nn.gelu(y + d_ref[...] * utm[seg(s), :])

    def recurrence(par, carry):
        s_re, s_im = carry
        for i in range(0, S5_SEG, 2):
            both = []
            for ii in (i, i + 1):
                rows = slice(ii * n_b, (ii + 1) * n_b)
                n_re = a_re * s_re - a_im * s_im + bu[par][rows, :S5_ST]
                n_im = a_re * s_im + a_im * s_re + bu[par][rows, S5_ST:]
                both.append(jnp.concatenate([n_re, n_im], axis=1))
                s_re, s_im = n_re, n_im
            st[par][i * n_b:(i + 2) * n_b, :] = jnp.concatenate(both, axis=0).astype(BF16)
        return s_re, s_im

    def stage(s, par, carry):
        b_proj(s + 1, 1 - par)
        c_proj(s - 1, 1 - par)
        return recurrence(par, carry)

    def stage_pair(p, carry):
        s = 2 * p + 1
        carry = stage(s, 1, carry)
        return stage(s + 1, 0, carry)

    b_proj(0, 0)
    b_proj(1, 1)
    carry = recurrence(0, (state[:, :S5_ST], state[:, S5_ST:]))
    carry = lax.fori_loop(0, (n_seg - 2) // 2, stage_pair, carry)
    c_proj(n_seg - 2, 0)
    s_re, s_im = recurrence(1, carry)
    c_proj(n_seg - 1, 1)
    state[:, :S5_ST] = s_re
    state[:, S5_ST:] = s_im

    for b in range(n_b):
        o_ref[b] = ytm[pl.ds(b, chunk, stride=n_b), :].astype(o_ref.dtype)


def _s5(u3, b_mat, c_mat, ab, d_flat):
    n_b, seq, _ = u3.shape
    assert n_b == SUBLANES, "the scan keeps one sequence per sublane"
    chunk = min(S5_CHUNK, seq)
    nb = SSM_GROUPS // S5_GROUP_BLOCK
    return pl.pallas_call(
        _s5_body,
        name="s5_scan",
        grid=(nb, seq // chunk),
        in_specs=[
            pl.BlockSpec((n_b, chunk, S5_CH), lambda j, c: (0, c, j)),
            pl.BlockSpec((None, S5_CH, 2 * S5_ST), lambda j, c: (j, 0, 0)),
            pl.BlockSpec((None, 2 * S5_ST, S5_CH), lambda j, c: (j, 0, 0)),
            pl.BlockSpec((None, SUBLANES, 2 * S5_ST), lambda j, c: (j, 0, 0)),
            pl.BlockSpec((1, S5_CH), lambda j, c: (0, j)),
        ],
        out_specs=pl.BlockSpec((n_b, chunk, S5_CH), lambda j, c: (0, c, j)),
        out_shape=jax.ShapeDtypeStruct((n_b, seq, SSM_WIDTH), BF16),
        scratch_shapes=[
            pltpu.VMEM((n_b * S5_PITCH, S5_CH), F32),
            pltpu.VMEM((chunk * n_b, S5_CH), F32),
            pltpu.VMEM((S5_SEG * n_b, 2 * S5_ST), F32),
            pltpu.VMEM((S5_SEG * n_b, 2 * S5_ST), F32),
            pltpu.VMEM((S5_SEG * n_b, 2 * S5_ST), BF16),
            pltpu.VMEM((S5_SEG * n_b, 2 * S5_ST), BF16),
            pltpu.VMEM((n_b, 2 * S5_ST), F32),
            pltpu.VMEM((chunk * n_b, S5_CH), F32),
        ],
        compiler_params=_cparams(("parallel", "arbitrary")),
    )(u3, b_mat, c_mat, ab, d_flat)


def _attn_body(sinks_ref, q_ref, kp_ref, kc_ref, vp_ref, vc_ref, bias_ref, half_ref, o_ref):
    lo_mask = half_ref[0:1, :]
    hi_mask = half_ref[1:2, :]
    lo_sel = lax.broadcasted_iota(jnp.int32, (BLOCK, LANES), 1) < HEAD_DIM
    for kv in range(N_KV_HEADS):
        grp = slice(kv * LANES, (kv + 1) * LANES)
        kw = jnp.concatenate([kp_ref[:, grp], kc_ref[:, grp]], axis=0)
        vw = jnp.concatenate([vp_ref[:, grp], vc_ref[:, grp]], axis=0)
        qs = []
        for pair in range(Q_PER_KV // 2):
            lanes = slice((2 * kv + pair) * LANES, (2 * kv + pair + 1) * LANES)
            qp = q_ref[:, lanes]
            qs += [qp * lo_mask, qp * hi_mask]
        q_st = jnp.concatenate(qs, axis=0)
        s = lax.dot_general(q_st, kw, (((1,), (1,)), ((), ())), preferred_element_type=F32)
        s = s + bias_ref[kv]
        ps = []
        for g in range(Q_PER_KV):
            sg = s[g * BLOCK:(g + 1) * BLOCK]
            sink = sinks_ref[kv * Q_PER_KV + g]
            m = jnp.maximum(jnp.max(sg, axis=-1, keepdims=True), sink)
            p = jnp.exp(sg - m)
            den = jnp.sum(p, axis=-1, keepdims=True) + jnp.exp(sink - m)
            ps.append((p / den).astype(BF16))
        p_st = jnp.concatenate(ps, axis=0)
        o = jnp.dot(p_st, vw, preferred_element_type=F32)
        for pair in range(Q_PER_KV // 2):
            even = o[(2 * pair) * BLOCK:(2 * pair + 1) * BLOCK]
            odd = o[(2 * pair + 1) * BLOCK:(2 * pair + 2) * BLOCK]
            lanes = slice((2 * kv + pair) * LANES, (2 * kv + pair + 1) * LANES)
            o_ref[:, lanes] = jnp.where(lo_sel, even, odd).astype(o_ref.dtype)


def _attn_bias():
    h = jnp.arange(1, N_Q_HEADS + 1, dtype=F32)
    slopes = jnp.exp2(-8.0 * h / N_Q_HEADS)
    qi = jnp.arange(BLOCK)[:, None]
    kj = jnp.arange(2 * BLOCK)[None, :]
    dist = qi - kj + BLOCK
    band = (dist >= 0) & (dist < WINDOW)
    bias = -slopes[:, None, None] * dist.astype(F32)
    later = jnp.where(band[None], bias, -jnp.inf)
    first = jnp.where((band & (kj >= BLOCK))[None], bias, -jnp.inf)
    both = jnp.stack([first, later])
    return both.reshape(2, N_KV_HEADS, Q_PER_KV * BLOCK, 2 * BLOCK)


def _attention(q3, kd3, vd3, sinks):
    n_b, seq, _ = q3.shape
    n_blk = seq // BLOCK
    kv_dup = N_KV_HEADS * LANES
    prev = lambda b, i: (b, jnp.maximum(i - 1, 0), 0)
    cur = lambda b, i: (b, i, 0)
    lane = jnp.arange(LANES)
    half = jnp.stack([lane < HEAD_DIM, lane >= HEAD_DIM]).astype(BF16)
    return pl.pallas_call(
        _attn_body,
        name="swa_attention",
        grid=(n_b, n_blk),
        in_specs=[
            pl.BlockSpec(memory_space=pltpu.SMEM),
            pl.BlockSpec((None, BLOCK, ATTN_WIDTH), cur),
            pl.BlockSpec((None, BLOCK, kv_dup), prev),
            pl.BlockSpec((None, BLOCK, kv_dup), cur),
            pl.BlockSpec((None, BLOCK, kv_dup), prev),
            pl.BlockSpec((None, BLOCK, kv_dup), cur),
            pl.BlockSpec((None, N_KV_HEADS, Q_PER_KV * BLOCK, 2 * BLOCK),
                         lambda b, i: (jnp.minimum(i, 1), 0, 0, 0)),
            pl.BlockSpec((2, LANES), lambda b, i: (0, 0)),
        ],
        out_specs=pl.BlockSpec((None, BLOCK, ATTN_WIDTH), cur),
        out_shape=jax.ShapeDtypeStruct((n_b, seq, ATTN_WIDTH), BF16),
        compiler_params=_cparams(("parallel", "arbitrary")),
    )(sinks, q3, kd3, kd3, vd3, vd3, _attn_bias(), half)


ROW_TILE = D_MODEL // LANES
assert ROW_TILE == SUBLANES


def _store_row_tiles(ref, x):
    for i in range(x.shape[0] // SUBLANES):
        for j in range(ROW_TILE):
            ref[pl.ds(i * SUBLANES * ROW_TILE + j, SUBLANES, stride=ROW_TILE), :] = (
                x[i * SUBLANES:(i + 1) * SUBLANES, j * LANES:(j + 1) * LANES])


def _load_row_tiles(ref, n):
    cols = []
    for j in range(ROW_TILE):
        cols.append(jnp.concatenate(
            [ref[pl.ds(i * SUBLANES * ROW_TILE + j, SUBLANES, stride=ROW_TILE), :]
             for i in range(n // SUBLANES)], axis=0))
    return jnp.concatenate(cols, axis=1)


def _mix_body(y_ref, a_ref, gs_ref, ga_ref, x_ref, wglu_ref, wsp_ref, wap_ref, wo_ref,
              g2_ref, wr_ref, br_ref, x1_ref, lg_ref):
    z = jnp.dot(y_ref[...], wglu_ref[...], preferred_element_type=F32)
    glu = z[:, :SSM_WIDTH] * jax.nn.sigmoid(z[:, SSM_WIDTH:])
    y_ssm = jnp.dot(glu.astype(BF16), wsp_ref[...], preferred_element_type=F32)
    y_attn = jnp.dot(a_ref[...], wap_ref[...], preferred_element_type=F32)
    mixed = jax.nn.sigmoid(gs_ref[...]) * y_ssm + jax.nn.sigmoid(ga_ref[...]) * y_attn
    x1 = x_ref[...] + jnp.dot(mixed.astype(BF16), wo_ref[...], preferred_element_type=F32)
    _store_row_tiles(x1_ref, x1)
    h2 = _rms(x1, g2_ref[...]).astype(BF16)
    lg_ref[...] = lax.dot_general(wr_ref[...], h2, (((1,), (1,)), ((), ())),
                                  preferred_element_type=F32) + br_ref[...]


def _mix(y2, a2, gs, ga, x2, w_glu, w_sp, w_ap, w_o, norm2_g, w_r_t, b_r):
    t = x2.shape[0]
    tm = min(TM_MIX, t)
    b_r_wide = jnp.broadcast_to(b_r.reshape(N_EXPERTS, 1), (N_EXPERTS, tm))
    tok = lambda w: pl.BlockSpec((tm, w), lambda i: (i, 0))
    full = lambda a: pl.BlockSpec(a.shape, lambda i: (0, 0))
    return pl.pallas_call(
        _mix_body,
        name="mix",
        grid=(t // tm,),
        in_specs=[tok(SSM_WIDTH), tok(ATTN_WIDTH), tok(D_MODEL), tok(D_MODEL), tok(D_MODEL),
                  full(w_glu), full(w_sp), full(w_ap), full(w_o), full(norm2_g),
                  full(w_r_t), full(b_r_wide)],
        out_specs=[pl.BlockSpec((tm * ROW_TILE, LANES), lambda i: (i, 0)),
                   pl.BlockSpec((N_EXPERTS, tm), lambda i: (0, i))],
        out_shape=[jax.ShapeDtypeStruct((t * ROW_TILE, LANES), F32),
                   jax.ShapeDtypeStruct((N_EXPERTS, t), F32)],
        compiler_params=_cparams(("parallel",)),
    )(y2, a2, gs, ga, x2, w_glu, w_sp, w_ap, w_o, norm2_g, w_r_t, b_r_wide)


ROUTE_ROWS = 4 * TOP_K


def _route_body(lg_ref, tri_ref, out_ref, cnt_ref, base):
    @pl.when(pl.program_id(0) == 0)
    def _():
        base[...] = jnp.zeros_like(base)

    tt = lg_ref.shape[1]
    row = lax.broadcasted_iota(jnp.int32, (N_EXPERTS, tt), 0)
    l = lg_ref[...]
    tops, hots = [], []
    for _ in range(TOP_K):
        m = jnp.max(l, axis=0, keepdims=True)
        idx = jnp.min(jnp.where(l == m, row, N_EXPERTS), axis=0, keepdims=True)
        hot = row == idx
        tops.append((m, idx))
        hots.append(hot)
        l = jnp.where(hot, -jnp.inf, l)
    es = [jnp.exp(m - tops[0][0]) for m, _ in tops]
    den = es[0] + es[1] + es[2] + es[3]
    member = jnp.zeros((N_EXPERTS, tt), F32)
    for hot in hots:
        member = member + jnp.where(hot, 1.0, 0.0)
    before = jnp.dot(member.astype(BF16), tri_ref[...], preferred_element_type=F32) + base[:, :tt]
    rows = [idx.astype(F32) for _, idx in tops]
    rows += [e / den for e in es]
    rows += [jnp.sum(jnp.where(hot, before, 0.0), axis=0, keepdims=True) for hot in hots]
    rows += [jnp.zeros((1, tt), F32)] * (ROUTE_ROWS - len(rows))
    out_ref[...] = jnp.concatenate(rows, axis=0)
    base[...] = base[...] + jnp.sum(member, axis=1, keepdims=True)
    cnt_ref[...] = base[:, :LANES]


def _route(logits_t):
    t = logits_t.shape[1]
    tt = min(TT_ROUTE, t)
    tri = jnp.triu(jnp.ones((tt, tt), F32), 1).astype(BF16)
    return pl.pallas_call(
        _route_body,
        name="route",
        grid=(t // tt,),
        in_specs=[pl.BlockSpec((N_EXPERTS, tt), lambda i: (0, i)),
                  pl.BlockSpec((tt, tt), lambda i: (0, 0))],
        out_specs=[pl.BlockSpec((ROUTE_ROWS, tt), lambda i: (0, i)),
                   pl.BlockSpec((N_EXPERTS, LANES), lambda i: (0, 0))],
        out_shape=[jax.ShapeDtypeStruct((ROUTE_ROWS, t), F32),
                   jax.ShapeDtypeStruct((N_EXPERTS, LANES), F32)],
        scratch_shapes=[pltpu.VMEM((N_EXPERTS, tt), F32)],
        compiler_params=_cparams(("arbitrary",)),
    )(logits_t, tri)


def _row_copies(n_tok, issue_one):
    def body(t8, carry):
        for i in range(SUBLANES):
            issue_one(t8 * SUBLANES + i)
        return carry
    lax.fori_loop(0, n_tok // SUBLANES, body, 0)


def _row_tile(ref, row):
    return ref.at[pl.ds(pl.multiple_of(row * ROW_TILE, ROW_TILE), ROW_TILE)]


def _dispatch_body(pe_ref, nu_ref, dest_ref, x_ref, xs_ref, zbuf, sem, zsem):
    tt = x_ref.shape[0] // ROW_TILE
    blk_rows = MOE_BLOCK * ROW_TILE
    n_blocks = xs_ref.shape[0] // blk_rows

    @pl.when(pl.program_id(0) == 0)
    def _():
        zbuf[...] = jnp.zeros_like(zbuf)

        def zero_block(row):
            return pltpu.make_async_copy(
                zbuf, xs_ref.at[pl.ds(pl.multiple_of(row * ROW_TILE, blk_rows), blk_rows)], zsem)

        def each(act):
            for e in range(N_EXPERTS):
                @pl.when(pe_ref[e] >= MOE_BLOCK)
                def _():
                    act(zero_block(pe_ref[e] - MOE_BLOCK))

            def tail(i, carry):
                act(zero_block(i * MOE_BLOCK))
                return carry
            lax.fori_loop(nu_ref[0], n_blocks, tail, 0)

        each(lambda c: c.start())
        each(lambda c: c.wait())

    def issue(t):
        for k in range(TOP_K):
            d = dest_ref[0, 0, k * tt + t]
            pltpu.make_async_copy(_row_tile(x_ref, t), _row_tile(xs_ref, d), sem).start(priority=k % 2)

    _row_copies(tt, issue)
    for _ in range(TOP_K):
        pltpu.make_async_copy(x_ref, xs_ref.at[pl.ds(0, tt * ROW_TILE)], sem).wait()


def _dispatch(pad_ends, n_used, x1r, dest3, n_rows):
    t = x1r.shape[0] // ROW_TILE
    tt = dest3.shape[2] // TOP_K
    grid_spec = pltpu.PrefetchScalarGridSpec(
        num_scalar_prefetch=2,
        grid=(t // tt,),
        in_specs=[pl.BlockSpec((1, 1, tt * TOP_K), lambda i, pe, nu: (i, 0, 0), memory_space=pltpu.SMEM),
                  pl.BlockSpec((tt * ROW_TILE, LANES), lambda i, pe, nu: (i, 0))],
        out_specs=pl.BlockSpec(memory_space=pl.ANY),
        scratch_shapes=[pltpu.VMEM((MOE_BLOCK * ROW_TILE, LANES), F32),
                        pltpu.SemaphoreType.DMA(()), pltpu.SemaphoreType.DMA(())],
    )
    return pl.pallas_call(
        _dispatch_body,
        name="dispatch",
        grid_spec=grid_spec,
        out_shape=jax.ShapeDtypeStruct((n_rows * ROW_TILE, LANES), F32),
        compiler_params=_cparams(("arbitrary",)),
    )(pad_ends, n_used, dest3, x1r)


def _ffn_body(b0_ref, nb_ref, nu_ref, g_ref, wgu_ref, bgu_ref, wd_ref, bd_ref, xs_ref, ys_ref,
              xbuf, ybuf, in_sems, out_sems, wgu_bf, wd_bf):
    e = pl.program_id(0)
    blk_rows = MOE_BLOCK * ROW_TILE
    n_blocks = xs_ref.shape[0] // blk_rows
    first = b0_ref[e]
    n_used = nu_ref[0]

    def rows(b):
        return pl.ds(pl.multiple_of(b * blk_rows, blk_rows), blk_rows)

    def in_copy(b):
        return pltpu.make_async_copy(xs_ref.at[rows(b)], xbuf.at[b % 2], in_sems.at[b % 2])

    def out_copy(b):
        return pltpu.make_async_copy(ybuf.at[b % 2], ys_ref.at[rows(b)], out_sems.at[b % 2])

    @pl.when(e == 0)
    def _():
        in_copy(0).start(priority=1)

    wgu_bf[...] = wgu_ref[...].astype(BF16)
    wd_bf[...] = wd_ref[...].astype(BF16)

    def block(j, carry):
        b = first + j
        in_copy(b).wait()

        @pl.when(b + 1 < n_used)
        def _():
            in_copy(b + 1).start(priority=1)

        h = _rms(_load_row_tiles(xbuf.at[b % 2], MOE_BLOCK), g_ref[...]).astype(BF16)
        gu = jnp.dot(h, wgu_bf[...], preferred_element_type=F32) + bgu_ref[...]
        gate = jnp.minimum(gu[:, :D_FF], SWIGLU_LIMIT)
        up = jnp.clip(gu[:, D_FF:], -SWIGLU_LIMIT, SWIGLU_LIMIT)
        act = (up + 1.0) * gate * jax.nn.sigmoid(SWIGLU_ALPHA * gate)
        y = jnp.dot(act.astype(BF16), wd_bf[...], preferred_element_type=F32) + bd_ref[...]

        @pl.when(b >= 2)
        def _():
            out_copy(b - 2).wait()

        _store_row_tiles(ybuf.at[b % 2], y)
        out_copy(b).start(priority=1)
        return carry

    lax.fori_loop(0, nb_ref[e], block, 0)

    @pl.when(e == pl.num_programs(0) - 1)
    def _():
        @pl.when(n_used >= 2)
        def _():
            out_copy(n_used - 2).wait()
        out_copy(n_used - 1).wait()

        ybuf[0] = jnp.zeros((blk_rows, LANES), F32)

        def tail(act):
            def body(i, carry):
                act(pltpu.make_async_copy(
                    ybuf.at[0], ys_ref.at[pl.ds(pl.multiple_of(i * blk_rows, blk_rows), blk_rows)],
                    out_sems.at[0]))
                return carry
            lax.fori_loop(nu_ref[0], n_blocks, body, 0)

        tail(lambda c: c.start())
        tail(lambda c: c.wait())


def _moe_ffn(blk_first, blk_count, n_used, xs, norm2_g, w_gu, b_gu, w_d, b_d):
    blk_rows = MOE_BLOCK * ROW_TILE
    by_e = lambda e, b0, nb, nu: (e, 0, 0)
    grid_spec = pltpu.PrefetchScalarGridSpec(
        num_scalar_prefetch=3,
        grid=(N_EXPERTS,),
        in_specs=[
            pl.BlockSpec((1, D_MODEL), lambda e, b0, nb, nu: (0, 0)),
            pl.BlockSpec((None, D_MODEL, 2 * D_FF), by_e),
            pl.BlockSpec((None, 1, 2 * D_FF), by_e),
            pl.BlockSpec((None, D_FF, D_MODEL), by_e),
            pl.BlockSpec((None, 1, D_MODEL), by_e),
            pl.BlockSpec(memory_space=pl.ANY),
        ],
        out_specs=pl.BlockSpec(memory_space=pl.ANY),
        scratch_shapes=[pltpu.VMEM((2, blk_rows, LANES), F32), pltpu.VMEM((2, blk_rows, LANES), F32),
                        pltpu.SemaphoreType.DMA((2,)), pltpu.SemaphoreType.DMA((2,)),
                        pltpu.VMEM((D_MODEL, 2 * D_FF), BF16), pltpu.VMEM((D_FF, D_MODEL), BF16)],
    )
    return pl.pallas_call(
        _ffn_body,
        name="moe_ffn",
        grid_spec=grid_spec,
        out_shape=jax.ShapeDtypeStruct(xs.shape, F32),
        compiler_params=_cparams(("arbitrary",)),
    )(blk_first, blk_count, n_used, norm2_g, w_gu, b_gu, w_d, b_d, xs)


def _combine_body(dest_ref, x_ref, gate_ref, ys_ref, o_ref, buf, sems):
    i = pl.program_id(0)
    n_tiles = pl.num_programs(0) - 1
    tt = o_ref.shape[0]

    def gathers(slot):
        def issue(t):
            for k in range(TOP_K):
                d = dest_ref[0, 0, k * tt + t]
                pltpu.make_async_copy(_row_tile(ys_ref, d), _row_tile(buf.at[slot, k], t),
                                      sems.at[slot]).start(priority=k % 2)
        _row_copies(tt, issue)

    def finish(slot):
        for k in range(TOP_K):
            pltpu.make_async_copy(ys_ref.at[pl.ds(0, tt * ROW_TILE)], buf.at[slot, k], sems.at[slot]).wait()
        acc = _load_row_tiles(x_ref, tt)
        for k in range(TOP_K):
            acc = acc + gate_ref[:, k:k + 1] * _load_row_tiles(buf.at[slot, k], tt)
        o_ref[...] = acc

    for slot in range(2):
        @pl.when((i < n_tiles) & (i % 2 == slot))
        def _():
            gathers(slot)

        @pl.when((i > 0) & ((i - 1) % 2 == slot))
        def _():
            finish(slot)


def _combine(x1r, gates, dest3, ys):
    t = x1r.shape[0] // ROW_TILE
    tt = dest3.shape[2] // TOP_K
    n_tiles = t // tt
    ahead = lambda i: (jnp.minimum(i, n_tiles - 1), 0, 0)
    behind = lambda i: (jnp.maximum(i - 1, 0), 0)
    return pl.pallas_call(
        _combine_body,
        name="combine",
        grid=(n_tiles + 1,),
        in_specs=[pl.BlockSpec((1, 1, tt * TOP_K), ahead, memory_space=pltpu.SMEM),
                  pl.BlockSpec((tt * ROW_TILE, LANES), behind),
                  pl.BlockSpec((tt, LANES), behind),
                  pl.BlockSpec(memory_space=pl.ANY)],
        out_specs=pl.BlockSpec((tt, D_MODEL), behind),
        out_shape=jax.ShapeDtypeStruct((t, D_MODEL), F32),
        scratch_shapes=[pltpu.VMEM((2, TOP_K, tt * ROW_TILE, LANES), F32), pltpu.SemaphoreType.DMA((2,))],
        compiler_params=_cparams(("arbitrary",)),
    )(dest3, x1r, gates, ys)


def _moe(x1, logits, norm2_g, w_gate_up, b_gate_up, w_down, b_down):
    t = x1.shape[0] // ROW_TILE
    n_assign = t * TOP_K
    n_blocks = -(-n_assign // MOE_BLOCK) + N_EXPERTS
    n_rows = n_blocks * MOE_BLOCK

    routed, cnt = _route(logits)
    ids = routed[:TOP_K].astype(jnp.int32)
    pos = routed[2 * TOP_K:3 * TOP_K].astype(jnp.int32)
    gates = jnp.zeros((t, LANES), F32).at[:, :TOP_K].set(routed[TOP_K:2 * TOP_K].T)
    counts = cnt[:, 0].astype(jnp.int32)
    padded = (counts + MOE_BLOCK - 1) // MOE_BLOCK * MOE_BLOCK
    pad_ends = jnp.cumsum(padded)
    pad_starts = pad_ends - padded
    dest = pad_starts[ids] + pos
    n_used = (pad_ends[-1] // MOE_BLOCK).astype(jnp.int32).reshape(1)

    tt = min(TT_MOVE, t)
    dest3 = dest.reshape(TOP_K, t // tt, tt).transpose(1, 0, 2).reshape(t // tt, 1, TOP_K * tt)
    xs = _dispatch(pad_ends.astype(jnp.int32), n_used, x1, dest3, n_rows)
    ys = _moe_ffn((pad_starts // MOE_BLOCK).astype(jnp.int32), (padded // MOE_BLOCK).astype(jnp.int32),
                  n_used, xs, norm2_g,
                  w_gate_up, b_gate_up.reshape(N_EXPERTS, 1, 2 * D_FF),
                  w_down, b_down.reshape(N_EXPERTS, 1, D_MODEL))
    return _combine(x1, gates, dest3, ys)


def _layer(x, norm1_g, w_in, ssm_a_re, ssm_a_im, ssm_log_dt, ssm_b_re, ssm_b_im, ssm_c_re,
           ssm_c_im, ssm_d, w_glu, w_ssm_proj, q_norm_g, k_norm_g, attn_sinks, w_attn_proj,
           w_out, norm2_g, w_router, b_router, w_gate_up, b_gate_up, w_down, b_down):
    n_b, seq, d = x.shape
    t = n_b * seq
    x2 = x.reshape(t, d)
    u, q, kd, vd, gs, ga = _in_proj(x2, norm1_g.reshape(1, d), w_in, q_norm_g, k_norm_g)

    abar_re, abar_im, bbt_re, bbt_im = _s5_prep(ssm_a_re, ssm_a_im, ssm_log_dt, ssm_b_re, ssm_b_im)
    b_mat, c_mat, ab = _s5_matrices(abar_re, abar_im, bbt_re, bbt_im, ssm_c_re, ssm_c_im)
    y = _s5(u.reshape(n_b, seq, SSM_WIDTH), b_mat, c_mat, ab, ssm_d.reshape(1, SSM_WIDTH))

    a = _attention(q.reshape(n_b, seq, ATTN_WIDTH), kd.reshape(n_b, seq, KV_DUP),
                   vd.reshape(n_b, seq, KV_DUP), attn_sinks)

    norm2 = norm2_g.reshape(1, d)
    x1, logits = _mix(y.reshape(t, SSM_WIDTH), a.reshape(t, ATTN_WIDTH), gs, ga, x2,
                      w_glu.astype(BF16), w_ssm_proj.astype(BF16), w_attn_proj.astype(BF16),
                      w_out.astype(BF16), norm2, w_router.T.astype(BF16), b_router)
    out = _moe(x1, logits, norm2, w_gate_up, b_gate_up, w_down, b_down)
    return out.reshape(n_b, seq, d)


def kernel(x, norm1_g, w_in, ssm_a_re, ssm_a_im, ssm_log_dt, ssm_b_re, ssm_b_im, ssm_c_re, ssm_c_im, ssm_d, w_glu, w_ssm_proj, q_norm_g, k_norm_g, attn_sinks, w_attn_proj, w_out, norm2_g, w_router, b_router, w_gate_up, b_gate_up, w_down, b_down):
    for layer in range(norm1_g.shape[0]):
        x = _layer(
            x, norm1_g[layer], w_in[layer], ssm_a_re[layer], ssm_a_im[layer], ssm_log_dt[layer],
            ssm_b_re[layer], ssm_b_im[layer], ssm_c_re[layer], ssm_c_im[layer], ssm_d[layer],
            w_glu[layer], w_ssm_proj[layer], q_norm_g[layer], k_norm_g[layer], attn_sinks[layer],
            w_attn_proj[layer], w_out[layer], norm2_g[layer], w_router[layer], b_router[layer],
            w_gate_up[layer], b_gate_up[layer], w_down[layer], b_down[layer])
    return x
```

```python
import functools

import jax
import jax.numpy as jnp
from jax import lax
from jax.experimental import pallas as pl
from jax.experimental.pallas import tpu as pltpu

D_MODEL = 1024
NORM_EPS = 1e-5
SSM_WIDTH = 1024
SSM_GROUP_CH = 16
SSM_GROUPS = SSM_WIDTH // SSM_GROUP_CH
SSM_STATE = 64
HEAD_DIM = 64
N_Q_HEADS = 16
N_KV_HEADS = 4
Q_PER_KV = N_Q_HEADS // N_KV_HEADS
ATTN_WIDTH = N_Q_HEADS * HEAD_DIM
KV_WIDTH = N_KV_HEADS * HEAD_DIM
WINDOW = 128
BLOCK = 128
N_EXPERTS = 32
TOP_K = 4
D_FF = 1024
SWIGLU_LIMIT = 7.0
SWIGLU_ALPHA = 1.702
MOE_BLOCK = 512

LANES = 128
SUBLANES = 8
VMEM_LIMIT = 56 * 1024 * 1024

S5_GROUP_BLOCK = 8
S5_CH = S5_GROUP_BLOCK * SSM_GROUP_CH
S5_ST = S5_GROUP_BLOCK * SSM_STATE
S5_CHUNK = 512
S5_SEG = 64
assert S5_CHUNK // S5_SEG >= 4 and (S5_CHUNK // S5_SEG) % 2 == 0
S5_PITCH = S5_CHUNK + SUBLANES

TM_IN = 256
TM_MIX = 512
TT_ROUTE = 512
TT_MOVE = 256

F32 = jnp.float32
BF16 = jnp.bfloat16


def _cparams(sem):
    return pltpu.CompilerParams(dimension_semantics=sem, vmem_limit_bytes=VMEM_LIMIT)


def _rms(x, g):
    return x * lax.rsqrt(jnp.mean(x * x, axis=-1, keepdims=True) + NORM_EPS) * g


KV_DUP = N_KV_HEADS * LANES


def _in_proj_body(x_ref, g_ref, w_ref, qg_ref, kg_ref, u_ref, q_ref, kd_ref, vd_ref, gs_ref, ga_ref):
    h = _rms(x_ref[...], g_ref[...]).astype(BF16)
    off = [0]

    def proj(width):
        lo = off[0]
        off[0] = lo + width
        return jnp.dot(h, w_ref[:, lo:lo + width], preferred_element_type=F32)

    u_ref[...] = proj(SSM_WIDTH)

    q = proj(ATTN_WIDTH)
    lo_sel = lax.broadcasted_iota(jnp.int32, (q.shape[0], LANES), 1) < HEAD_DIM
    for j in range(ATTN_WIDTH // LANES):
        qq = q[:, j * LANES:(j + 1) * LANES]
        sq = qq * qq
        ms_lo = jnp.sum(jnp.where(lo_sel, sq, 0.0), axis=-1, keepdims=True) / HEAD_DIM
        ms_hi = jnp.sum(jnp.where(lo_sel, 0.0, sq), axis=-1, keepdims=True) / HEAD_DIM
        r = jnp.where(lo_sel, lax.rsqrt(ms_lo + NORM_EPS), lax.rsqrt(ms_hi + NORM_EPS))
        q_ref[:, j * LANES:(j + 1) * LANES] = (qq * r * qg_ref[...]).astype(q_ref.dtype)

    kd = proj(KV_DUP)
    for kv in range(N_KV_HEADS):
        grp = slice(kv * LANES, (kv + 1) * LANES)
        kd_ref[:, grp] = _rms(kd[:, grp], kg_ref[...]).astype(kd_ref.dtype)
    vd_ref[...] = proj(KV_DUP).astype(vd_ref.dtype)
    gs_ref[...] = proj(D_MODEL)
    ga_ref[...] = proj(D_MODEL)


def _in_proj_weight(w_in):
    d = w_in.shape[0]
    cuts = (SSM_WIDTH, SSM_WIDTH + ATTN_WIDTH, SSM_WIDTH + ATTN_WIDTH + KV_WIDTH,
            SSM_WIDTH + ATTN_WIDTH + 2 * KV_WIDTH)
    w_uq, w_k, w_v, w_g = (w_in[:, :cuts[1]], w_in[:, cuts[1]:cuts[2]], w_in[:, cuts[2]:cuts[3]],
                           w_in[:, cuts[3]:])

    def twice(w):
        w = w.reshape(d, N_KV_HEADS, 1, HEAD_DIM)
        return jnp.concatenate([w, w], axis=2).reshape(d, KV_DUP)

    return jnp.concatenate([w_uq, twice(w_k), twice(w_v), w_g], axis=1).astype(BF16)


def _in_proj(x2, norm_g, w_in, q_norm_g, k_norm_g):
    t = x2.shape[0]
    w = _in_proj_weight(w_in)
    qg = (jnp.tile(q_norm_g, LANES // HEAD_DIM) * HEAD_DIM ** -0.5).reshape(1, LANES)
    kg = jnp.tile(k_norm_g, LANES // HEAD_DIM).reshape(1, LANES)
    outs = ((SSM_WIDTH, F32), (ATTN_WIDTH, BF16), (KV_DUP, BF16), (KV_DUP, BF16),
            (D_MODEL, F32), (D_MODEL, F32))
    const = lambda a: pl.BlockSpec(a.shape, lambda i: (0, 0))
    return pl.pallas_call(
        _in_proj_body,
        name="in_proj",
        grid=(t // TM_IN,),
        in_specs=[pl.BlockSpec((TM_IN, D_MODEL), lambda i: (i, 0)), const(norm_g), const(w),
                  const(qg), const(kg)],
        out_specs=[pl.BlockSpec((TM_IN, n), lambda i: (i, 0)) for n, _ in outs],
        out_shape=[jax.ShapeDtypeStruct((t, n), dt) for n, dt in outs],
        compiler_params=_cparams(("parallel",)),
    )(x2, norm_g, w, qg, kg)


def _s5_prep_body(are_ref, aim_ref, ldt_ref, bre_ref, bim_ref,
                  abr_ref, abi_ref, bbr_ref, bbi_ref):
    a_re = are_ref[...]
    a_im = aim_ref[...]
    dt = jnp.exp(ldt_ref[...])
    mag = jnp.exp(a_re * dt)
    abar_re = mag * jnp.cos(a_im * dt)
    abar_im = mag * jnp.sin(a_im * dt)
    den = a_re * a_re + a_im * a_im
    q_re = ((abar_re - 1.0) * a_re + abar_im * a_im) / den
    q_im = (abar_im * a_re - (abar_re - 1.0) * a_im) / den
    abr_ref[...] = abar_re
    abi_ref[...] = abar_im
    for h in range(SSM_GROUP_CH):
        bbr_ref[h] = q_re * bre_ref[h] - q_im * bim_ref[h]
        bbi_ref[h] = q_re * bim_ref[h] + q_im * bre_ref[h]


def _s5_prep(a_re, a_im, log_dt, b_re, b_im):
    g, p, hc = SSM_GROUPS, SSM_STATE, SSM_GROUP_CH
    bt_re = jnp.transpose(b_re, (2, 0, 1))
    bt_im = jnp.transpose(b_im, (2, 0, 1))
    return pl.pallas_call(
        _s5_prep_body,
        name="s5_prep",
        out_shape=[
            jax.ShapeDtypeStruct((g, p), F32),
            jax.ShapeDtypeStruct((g, p), F32),
            jax.ShapeDtypeStruct((hc, g, p), F32),
            jax.ShapeDtypeStruct((hc, g, p), F32),
        ],
    )(a_re, a_im, log_dt.reshape(g, 1), bt_re, bt_im)


def _s5_matrices(abar_re, abar_im, bbt_re, bbt_im, c_re, c_im):
    nb, gb, hc, p = SSM_GROUPS // S5_GROUP_BLOCK, S5_GROUP_BLOCK, SSM_GROUP_CH, SSM_STATE
    eye = jnp.eye(gb, dtype=F32)

    def b_block(bbt):
        t = jnp.transpose(bbt, (1, 0, 2)).reshape(nb, gb, hc, p)
        return jnp.einsum('jghp,gk->jghkp', t, eye).reshape(nb, gb * hc, gb * p)

    def c_block(c):
        t = c.reshape(nb, gb, hc, p)
        return jnp.einsum('jghp,gk->jgpkh', t, eye).reshape(nb, gb * p, gb * hc)

    b_mat = jnp.concatenate([b_block(bbt_re), b_block(bbt_im)], axis=2).astype(BF16)
    c_mat = jnp.concatenate([c_block(c_re), c_block(-c_im)], axis=1).astype(BF16)
    ab = jnp.concatenate([abar_re.reshape(nb, 1, gb * p), abar_im.reshape(nb, 1, gb * p)], axis=2)
    ab = jnp.broadcast_to(ab, (nb, SUBLANES, 2 * gb * p))
    return b_mat, c_mat, ab


def _s5_body(u_ref, b_ref, c_ref, ab_ref, d_ref, o_ref, ubuf, utm, bu0, bu1, st0, st1, state, ytm):
    n_b = u_ref.shape[0]
    chunk = u_ref.shape[1]

    @pl.when(pl.program_id(1) == 0)
    def _():
        state[...] = jnp.zeros_like(state)

    for b in range(n_b):
        ubuf[b * S5_PITCH:b * S5_PITCH + chunk, :] = u_ref[b]

    def to_time_major(l8, carry):
        for i in range(SUBLANES):
            l = l8 * SUBLANES + i
            row = pl.multiple_of(l * n_b, SUBLANES)
            utm[pl.ds(row, n_b), :] = ubuf[pl.ds(l, n_b, stride=S5_PITCH), :]
        return carry

    lax.fori_loop(0, chunk // SUBLANES, to_time_major, 0)

    seg_rows = S5_SEG * n_b
    n_seg = chunk // S5_SEG
    a_re = ab_ref[:, :S5_ST]
    a_im = ab_ref[:, S5_ST:]

    bu = (bu0, bu1)
    st = (st0, st1)

    def seg(s):
        return pl.ds(pl.multiple_of(s * seg_rows, seg_rows), seg_rows)

    def b_proj(s, par):
        bu[par][...] = jnp.dot(utm[seg(s), :].astype(BF16), b_ref[...], preferred_element_type=F32)

    def c_proj(s, par):
        y = jnp.dot(st[par][...], c_ref[...], preferred_element_type=F32)
        ytm[seg(s), :] = jax.nn.gelu(y + d_ref[...] * utm[seg(s), :])

    def recurrence(par, carry):
        s_re, s_im = carry
        for i in range(0, S5_SEG, 2):
            both = []
            for ii in (i, i + 1):
                rows = slice(ii * n_b, (ii + 1) * n_b)
                n_re = a_re * s_re - a_im * s_im + bu[par][rows, :S5_ST]
                n_im = a_re * s_im + a_im * s_re + bu[par][rows, S5_ST:]
                both.append(jnp.concatenate([n_re, n_im], axis=1))
                s_re, s_im = n_re, n_im
            st[par][i * n_b:(i + 2) * n_b, :] = jnp.concatenate(both, axis=0).astype(BF16)
        return s_re, s_im

    def stage(s, par, carry):
        b_proj(s + 1, 1 - par)
        c_proj(s - 1, 1 - par)
        return recurrence(par, carry)

    def stage_pair(p, carry):
        s = 2 * p + 1
        carry = stage(s, 1, carry)
        return stage(s + 1, 0, carry)

    b_proj(0, 0)
    b_proj(1, 1)
    carry = recurrence(0, (state[:, :S5_ST], state[:, S5_ST:]))
    carry = lax.fori_loop(0, (n_seg - 2) // 2, stage_pair, carry)
    c_proj(n_seg - 2, 0)
    s_re, s_im = recurrence(1, carry)
    c_proj(n_seg - 1, 1)
    state[:, :S5_ST] = s_re
    state[:, S5_ST:] = s_im

    for b in range(n_b):
        o_ref[b] = ytm[pl.ds(b, chunk, stride=n_b), :].astype(o_ref.dtype)


def _s5(u3, b_mat, c_mat, ab, d_flat):
    n_b, seq, _ = u3.shape
    assert n_b == SUBLANES, "the scan keeps one sequence per sublane"
    chunk = min(S5_CHUNK, seq)
    nb = SSM_GROUPS // S5_GROUP_BLOCK
    return pl.pallas_call(
        _s5_body,
        name="s5_scan",
        grid=(nb, seq // chunk),
        in_specs=[
            pl.BlockSpec((n_b, chunk, S5_CH), lambda j, c: (0, c, j)),
            pl.BlockSpec((None, S5_CH, 2 * S5_ST), lambda j, c: (j, 0, 0)),
            pl.BlockSpec((None, 2 * S5_ST, S5_CH), lambda j, c: (j, 0, 0)),
            pl.BlockSpec((None, SUBLANES, 2 * S5_ST), lambda j, c: (j, 0, 0)),
            pl.BlockSpec((1, S5_CH), lambda j, c: (0, j)),
        ],
        out_specs=pl.BlockSpec((n_b, chunk, S5_CH), lambda j, c: (0, c, j)),
        out_shape=jax.ShapeDtypeStruct((n_b, seq, SSM_WIDTH), BF16),
        scratch_shapes=[
            pltpu.VMEM((n_b * S5_PITCH, S5_CH), F32),
            pltpu.VMEM((chunk * n_b, S5_CH), F32),
            pltpu.VMEM((S5_SEG * n_b, 2 * S5_ST), F32),
            pltpu.VMEM((S5_SEG * n_b, 2 * S5_ST), F32),
            pltpu.VMEM((S5_SEG * n_b, 2 * S5_ST), BF16),
            pltpu.VMEM((S5_SEG * n_b, 2 * S5_ST), BF16),
            pltpu.VMEM((n_b, 2 * S5_ST), F32),
            pltpu.VMEM((chunk * n_b, S5_CH), F32),
        ],
        compiler_params=_cparams(("parallel", "arbitrary")),
    )(u3, b_mat, c_mat, ab, d_flat)


def _attn_body(sinks_ref, q_ref, kp_ref, kc_ref, vp_ref, vc_ref, bias_ref, half_ref, o_ref):
    lo_mask = half_ref[0:1, :]
    hi_mask = half_ref[1:2, :]
    lo_sel = lax.broadcasted_iota(jnp.int32, (BLOCK, LANES), 1) < HEAD_DIM
    for kv in range(N_KV_HEADS):
        grp = slice(kv * LANES, (kv + 1) * LANES)
        kw = jnp.concatenate([kp_ref[:, grp], kc_ref[:, grp]], axis=0)
        vw = jnp.concatenate([vp_ref[:, grp], vc_ref[:, grp]], axis=0)
        qs = []
        for pair in range(Q_PER_KV // 2):
            lanes = slice((2 * kv + pair) * LANES, (2 * kv + pair + 1) * LANES)
            qp = q_ref[:, lanes]
            qs += [qp * lo_mask, qp * hi_mask]
        q_st = jnp.concatenate(qs, axis=0)
        s = lax.dot_general(q_st, kw, (((1,), (1,)), ((), ())), preferred_element_type=F32)
        s = s + bias_ref[kv]
        ps = []
        for g in range(Q_PER_KV):
            sg = s[g * BLOCK:(g + 1) * BLOCK]
            sink = sinks_ref[kv * Q_PER_KV + g]
            m = jnp.maximum(jnp.max(sg, axis=-1, keepdims=True), sink)
            p = jnp.exp(sg - m)
            den = jnp.sum(p, axis=-1, keepdims=True) + jnp.exp(sink - m)
            ps.append((p / den).astype(BF16))
        p_st = jnp.concatenate(ps, axis=0)
        o = jnp.dot(p_st, vw, preferred_element_type=F32)
        for pair in range(Q_PER_KV // 2):
            even = o[(2 * pair) * BLOCK:(2 * pair + 1) * BLOCK]
            odd = o[(2 * pair + 1) * BLOCK:(2 * pair + 2) * BLOCK]
            lanes = slice((2 * kv + pair) * LANES, (2 * kv + pair + 1) * LANES)
            o_ref[:, lanes] = jnp.where(lo_sel, even, odd).astype(o_ref.dtype)


def _attn_bias():
    h = jnp.arange(1, N_Q_HEADS + 1, dtype=F32)
    slopes = jnp.exp2(-8.0 * h / N_Q_HEADS)
    qi = jnp.arange(BLOCK)[:, None]
    kj = jnp.arange(2 * BLOCK)[None, :]
    dist = qi - kj + BLOCK
    band = (dist >= 0) & (dist < WINDOW)
    bias = -slopes[:, None, None] * dist.astype(F32)
    later = jnp.where(band[None], bias, -jnp.inf)
    first = jnp.where((band & (kj >= BLOCK))[None], bias, -jnp.inf)
    both = jnp.stack([first, later])
    return both.reshape(2, N_KV_HEADS, Q_PER_KV * BLOCK, 2 * BLOCK)


def _attention(q3, kd3, vd3, sinks):
    n_b, seq, _ = q3.shape
    n_blk = seq // BLOCK
    kv_dup = N_KV_HEADS * LANES
    prev = lambda b, i: (b, jnp.maximum(i - 1, 0), 0)
    cur = lambda b, i: (b, i, 0)
    lane = jnp.arange(LANES)
    half = jnp.stack([lane < HEAD_DIM, lane >= HEAD_DIM]).astype(BF16)
    return pl.pallas_call(
        _attn_body,
        name="swa_attention",
        grid=(n_b, n_blk),
        in_specs=[
            pl.BlockSpec(memory_space=pltpu.SMEM),
            pl.BlockSpec((None, BLOCK, ATTN_WIDTH), cur),
            pl.BlockSpec((None, BLOCK, kv_dup), prev),
            pl.BlockSpec((None, BLOCK, kv_dup), cur),
            pl.BlockSpec((None, BLOCK, kv_dup), prev),
            pl.BlockSpec((None, BLOCK, kv_dup), cur),
            pl.BlockSpec((None, N_KV_HEADS, Q_PER_KV * BLOCK, 2 * BLOCK),
                         lambda b, i: (jnp.minimum(i, 1), 0, 0, 0)),
            pl.BlockSpec((2, LANES), lambda b, i: (0, 0)),
        ],
        out_specs=pl.BlockSpec((None, BLOCK, ATTN_WIDTH), cur),
        out_shape=jax.ShapeDtypeStruct((n_b, seq, ATTN_WIDTH), BF16),
        compiler_params=_cparams(("parallel", "arbitrary")),
    )(sinks, q3, kd3, kd3, vd3, vd3, _attn_bias(), half)


ROW_TILE = D_MODEL // LANES
assert ROW_TILE == SUBLANES


def _store_row_tiles(ref, x):
    for i in range(x.shape[0] // SUBLANES):
        for j in range(ROW_TILE):
            ref[pl.ds(i * SUBLANES * ROW_TILE + j, SUBLANES, stride=ROW_TILE), :] = (
                x[i * SUBLANES:(i + 1) * SUBLANES, j * LANES:(j + 1) * LANES])


def _load_row_tiles(ref, n):
    cols = []
    for j in range(ROW_TILE):
        cols.append(jnp.concatenate(
            [ref[pl.ds(i * SUBLANES * ROW_TILE + j, SUBLANES, stride=ROW_TILE), :]
             for i in range(n // SUBLANES)], axis=0))
    return jnp.concatenate(cols, axis=1)


def _mix_body(y_ref, a_ref, gs_ref, ga_ref, x_ref, wglu_ref, wsp_ref, wap_ref, wo_ref,
              g2_ref, wr_ref, br_ref, x1_ref, lg_ref):
    z = jnp.dot(y_ref[...], wglu_ref[...], preferred_element_type=F32)
    glu = z[:, :SSM_WIDTH] * jax.nn.sigmoid(z[:, SSM_WIDTH:])
    y_ssm = jnp.dot(glu.astype(BF16), wsp_ref[...], preferred_element_type=F32)
    y_attn = jnp.dot(a_ref[...], wap_ref[...], preferred_element_type=F32)
    mixed = jax.nn.sigmoid(gs_ref[...]) * y_ssm + jax.nn.sigmoid(ga_ref[...]) * y_attn
    x1 = x_ref[...] + jnp.dot(mixed.astype(BF16), wo_ref[...], preferred_element_type=F32)
    _store_row_tiles(x1_ref, x1)
    h2 = _rms(x1, g2_ref[...]).astype(BF16)
    lg_ref[...] = lax.dot_general(wr_ref[...], h2, (((1,), (1,)), ((), ())),
                                  preferred_element_type=F32) + br_ref[...]


def _mix(y2, a2, gs, ga, x2, w_glu, w_sp, w_ap, w_o, norm2_g, w_r_t, b_r):
    t = x2.shape[0]
    tm = min(TM_MIX, t)
    b_r_wide = jnp.broadcast_to(b_r.reshape(N_EXPERTS, 1), (N_EXPERTS, tm))
    tok = lambda w: pl.BlockSpec((tm, w), lambda i: (i, 0))
    full = lambda a: pl.BlockSpec(a.shape, lambda i: (0, 0))
    return pl.pallas_call(
        _mix_body,
        name="mix",
        grid=(t // tm,),
        in_specs=[tok(SSM_WIDTH), tok(ATTN_WIDTH), tok(D_MODEL), tok(D_MODEL), tok(D_MODEL),
                  full(w_glu), full(w_sp), full(w_ap), full(w_o), full(norm2_g),
                  full(w_r_t), full(b_r_wide)],
        out_specs=[pl.BlockSpec((tm * ROW_TILE, LANES), lambda i: (i, 0)),
                   pl.BlockSpec((N_EXPERTS, tm), lambda i: (0, i))],
        out_shape=[jax.ShapeDtypeStruct((t * ROW_TILE, LANES), F32),
                   jax.ShapeDtypeStruct((N_EXPERTS, t), F32)],
        compiler_params=_cparams(("parallel",)),
    )(y2, a2, gs, ga, x2, w_glu, w_sp, w_ap, w_o, norm2_g, w_r_t, b_r_wide)


ROUTE_ROWS = 4 * TOP_K


def _route_body(lg_ref, tri_ref, out_ref, cnt_ref, base):
    @pl.when(pl.program_id(0) == 0)
    def _():
        base[...] = jnp.zeros_like(base)

    tt = lg_ref.shape[1]
    row = lax.broadcasted_iota(jnp.int32, (N_EXPERTS, tt), 0)
    l = lg_ref[...]
    tops, hots = [], []
    for _ in range(TOP_K):
        m = jnp.max(l, axis=0, keepdims=True)
        idx = jnp.min(jnp.where(l == m, row, N_EXPERTS), axis=0, keepdims=True)
        hot = row == idx
        tops.append((m, idx))
        hots.append(hot)
        l = jnp.where(hot, -jnp.inf, l)
    es = [jnp.exp(m - tops[0][0]) for m, _ in tops]
    den = es[0] + es[1] + es[2] + es[3]
    member = jnp.zeros((N_EXPERTS, tt), F32)
    for hot in hots:
        member = member + jnp.where(hot, 1.0, 0.0)
    before = jnp.dot(member.astype(BF16), tri_ref[...], preferred_element_type=F32) + base[:, :tt]
    rows = [idx.astype(F32) for _, idx in tops]
    rows += [e / den for e in es]
    rows += [jnp.sum(jnp.where(hot, before, 0.0), axis=0, keepdims=True) for hot in hots]
    rows += [jnp.zeros((1, tt), F32)] * (ROUTE_ROWS - len(rows))
    out_ref[...] = jnp.concatenate(rows, axis=0)
    base[...] = base[...] + jnp.sum(member, axis=1, keepdims=True)
    cnt_ref[...] = base[:, :LANES]


def _route(logits_t):
    t = logits_t.shape[1]
    tt = min(TT_ROUTE, t)
    tri = jnp.triu(jnp.ones((tt, tt), F32), 1).astype(BF16)
    return pl.pallas_call(
        _route_body,
        name="route",
        grid=(t // tt,),
        in_specs=[pl.BlockSpec((N_EXPERTS, tt), lambda i: (0, i)),
                  pl.BlockSpec((tt, tt), lambda i: (0, 0))],
        out_specs=[pl.BlockSpec((ROUTE_ROWS, tt), lambda i: (0, i)),
                   pl.BlockSpec((N_EXPERTS, LANES), lambda i: (0, 0))],
        out_shape=[jax.ShapeDtypeStruct((ROUTE_ROWS, t), F32),
                   jax.ShapeDtypeStruct((N_EXPERTS, LANES), F32)],
        scratch_shapes=[pltpu.VMEM((N_EXPERTS, tt), F32)],
        compiler_params=_cparams(("arbitrary",)),
    )(logits_t, tri)


def _row_copies(n_tok, issue_one):
    def body(t8, carry):
        for i in range(SUBLANES):
            issue_one(t8 * SUBLANES + i)
        return carry
    lax.fori_loop(0, n_tok // SUBLANES, body, 0)


def _row_tile(ref, row):
    return ref.at[pl.ds(pl.multiple_of(row * ROW_TILE, ROW_TILE), ROW_TILE)]


def _dispatch_body(pe_ref, nu_ref, dest_ref, x_ref, xs_ref, zbuf, sem, zsem):
    tt = x_ref.shape[0] // ROW_TILE
    blk_rows = MOE_BLOCK * ROW_TILE
    n_blocks = xs_ref.shape[0] // blk_rows

    @pl.when(pl.program_id(0) == 0)
    def _():
        zbuf[...] = jnp.zeros_like(zbuf)

        def zero_block(row):
            return pltpu.make_async_copy(
                zbuf, xs_ref.at[pl.ds(pl.multiple_of(row * ROW_TILE, blk_rows), blk_rows)], zsem)

        def each(act):
            for e in range(N_EXPERTS):
                @pl.when(pe_ref[e] >= MOE_BLOCK)
                def _():
                    act(zero_block(pe_ref[e] - MOE_BLOCK))

            def tail(i, carry):
                act(zero_block(i * MOE_BLOCK))
                return carry
            lax.fori_loop(nu_ref[0], n_blocks, tail, 0)

        each(lambda c: c.start())
        each(lambda c: c.wait())

    def issue(t):
        for k in range(TOP_K):
            d = dest_ref[0, 0, k * tt + t]
            pltpu.make_async_copy(_row_tile(x_ref, t), _row_tile(xs_ref, d), sem).start(priority=k % 2)

    _row_copies(tt, issue)
    for _ in range(TOP_K):
        pltpu.make_async_copy(x_ref, xs_ref.at[pl.ds(0, tt * ROW_TILE)], sem).wait()


def _dispatch(pad_ends, n_used, x1r, dest3, n_rows):
    t = x1r.shape[0] // ROW_TILE
    tt = dest3.shape[2] // TOP_K
    grid_spec = pltpu.PrefetchScalarGridSpec(
        num_scalar_prefetch=2,
        grid=(t // tt,),
        in_specs=[pl.BlockSpec((1, 1, tt * TOP_K), lambda i, pe, nu: (i, 0, 0), memory_space=pltpu.SMEM),
                  pl.BlockSpec((tt * ROW_TILE, LANES), lambda i, pe, nu: (i, 0))],
        out_specs=pl.BlockSpec(memory_space=pl.ANY),
        scratch_shapes=[pltpu.VMEM((MOE_BLOCK * ROW_TILE, LANES), F32),
                        pltpu.SemaphoreType.DMA(()), pltpu.SemaphoreType.DMA(())],
    )
    return pl.pallas_call(
        _dispatch_body,
        name="dispatch",
        grid_spec=grid_spec,
        out_shape=jax.ShapeDtypeStruct((n_rows * ROW_TILE, LANES), F32),
        compiler_params=_cparams(("arbitrary",)),
    )(pad_ends, n_used, dest3, x1r)


def _ffn_body(b0_ref, nb_ref, nu_ref, g_ref, wgu_ref, bgu_ref, wd_ref, bd_ref, xs_ref, ys_ref,
              xbuf, ybuf, in_sems, out_sems, wgu_bf, wd_bf):
    e = pl.program_id(0)
    blk_rows = MOE_BLOCK * ROW_TILE
    n_blocks = xs_ref.shape[0] // blk_rows
    first = b0_ref[e]
    n_used = nu_ref[0]

    def rows(b):
        return pl.ds(pl.multiple_of(b * blk_rows, blk_rows), blk_rows)

    def in_copy(b):
        return pltpu.make_async_copy(xs_ref.at[rows(b)], xbuf.at[b % 2], in_sems.at[b % 2])

    def out_copy(b):
        return pltpu.make_async_copy(ybuf.at[b % 2], ys_ref.at[rows(b)], out_sems.at[b % 2])

    @pl.when(e == 0)
    def _():
        in_copy(0).start(priority=1)

    wgu_bf[...] = wgu_ref[...].astype(BF16)
    wd_bf[...] = wd_ref[...].astype(BF16)

    def block(j, carry):
        b = first + j
        in_copy(b).wait()

        @pl.when(b + 1 < n_used)
        def _():
            in_copy(b + 1).start(priority=1)

        h = _rms(_load_row_tiles(xbuf.at[b % 2], MOE_BLOCK), g_ref[...]).astype(BF16)
        gu = jnp.dot(h, wgu_bf[...], preferred_element_type=F32) + bgu_ref[...]
        gate = jnp.minimum(gu[:, :D_FF], SWIGLU_LIMIT)
        up = jnp.clip(gu[:, D_FF:], -SWIGLU_LIMIT, SWIGLU_LIMIT)
        act = (up + 1.0) * gate * jax.nn.sigmoid(SWIGLU_ALPHA * gate)
        y = jnp.dot(act.astype(BF16), wd_bf[...], preferred_element_type=F32) + bd_ref[...]

        @pl.when(b >= 2)
        def _():
            out_copy(b - 2).wait()

        _store_row_tiles(ybuf.at[b % 2], y)
        out_copy(b).start(priority=1)
        return carry

    lax.fori_loop(0, nb_ref[e], block, 0)

    @pl.when(e == pl.num_programs(0) - 1)
    def _():
        @pl.when(n_used >= 2)
        def _():
            out_copy(n_used - 2).wait()
        out_copy(n_used - 1).wait()

        ybuf[0] = jnp.zeros((blk_rows, LANES), F32)

        def tail(act):
            def body(i, carry):
                act(pltpu.make_async_copy(
                    ybuf.at[0], ys_ref.at[pl.ds(pl.multiple_of(i * blk_rows, blk_rows), blk_rows)],
                    out_sems.at[0]))
                return carry
            lax.fori_loop(nu_ref[0], n_blocks, body, 0)

        tail(lambda c: c.start())
        tail(lambda c: c.wait())


def _moe_ffn(blk_first, blk_count, n_used, xs, norm2_g, w_gu, b_gu, w_d, b_d):
    blk_rows = MOE_BLOCK * ROW_TILE
    by_e = lambda e, b0, nb, nu: (e, 0, 0)
    grid_spec = pltpu.PrefetchScalarGridSpec(
        num_scalar_prefetch=3,
        grid=(N_EXPERTS,),
        in_specs=[
            pl.BlockSpec((1, D_MODEL), lambda e, b0, nb, nu: (0, 0)),
            pl.BlockSpec((None, D_MODEL, 2 * D_FF), by_e),
            pl.BlockSpec((None, 1, 2 * D_FF), by_e),
            pl.BlockSpec((None, D_FF, D_MODEL), by_e),
            pl.BlockSpec((None, 1, D_MODEL), by_e),
            pl.BlockSpec(memory_space=pl.ANY),
        ],
        out_specs=pl.BlockSpec(memory_space=pl.ANY),
        scratch_shapes=[pltpu.VMEM((2, blk_rows, LANES), F32), pltpu.VMEM((2, blk_rows, LANES), F32),
                        pltpu.SemaphoreType.DMA((2,)), pltpu.SemaphoreType.DMA((2,)),
                        pltpu.VMEM((D_MODEL, 2 * D_FF), BF16), pltpu.VMEM((D_FF, D_MODEL), BF16)],
    )
    return pl.pallas_call(
        _ffn_body,
        name="moe_ffn",
        grid_spec=grid_spec,
        out_shape=jax.ShapeDtypeStruct(xs.shape, F32),
        compiler_params=_cparams(("arbitrary",)),
    )(blk_first, blk_count, n_used, norm2_g, w_gu, b_gu, w_d, b_d, xs)


def _combine_body(dest_ref, x_ref, gate_ref, ys_ref, o_ref, buf, sems):
    i = pl.program_id(0)
    n_tiles = pl.num_programs(0) - 1
    tt = o_ref.shape[0]

    def gathers(slot):
        def issue(t):
            for k in range(TOP_K):
                d = dest_ref[0, 0, k * tt + t]
                pltpu.make_async_copy(_row_tile(ys_ref, d), _row_tile(buf.at[slot, k], t),
                                      sems.at[slot]).start(priority=k % 2)
        _row_copies(tt, issue)

    def finish(slot):
        for k in range(TOP_K):
            pltpu.make_async_copy(ys_ref.at[pl.ds(0, tt * ROW_TILE)], buf.at[slot, k], sems.at[slot]).wait()
        acc = _load_row_tiles(x_ref, tt)
        for k in range(TOP_K):
            acc = acc + gate_ref[:, k:k + 1] * _load_row_tiles(buf.at[slot, k], tt)
        o_ref[...] = acc

    for slot in range(2):
        @pl.when((i < n_tiles) & (i % 2 == slot))
        def _():
            gathers(slot)

        @pl.when((i > 0) & ((i - 1) % 2 == slot))
        def _():
            finish(slot)


def _combine(x1r, gates, dest3, ys):
    t = x1r.shape[0] // ROW_TILE
    tt = dest3.shape[2] // TOP_K
    n_tiles = t // tt
    ahead = lambda i: (jnp.minimum(i, n_tiles - 1), 0, 0)
    behind = lambda i: (jnp.maximum(i - 1, 0), 0)
    return pl.pallas_call(
        _combine_body,
        name="combine",
        grid=(n_tiles + 1,),
        in_specs=[pl.BlockSpec((1, 1, tt * TOP_K), ahead, memory_space=pltpu.SMEM),
                  pl.BlockSpec((tt * ROW_TILE, LANES), behind),
                  pl.BlockSpec((tt, LANES), behind),
                  pl.BlockSpec(memory_space=pl.ANY)],
        out_specs=pl.BlockSpec((tt, D_MODEL), behind),
        out_shape=jax.ShapeDtypeStruct((t, D_MODEL), F32),
        scratch_shapes=[pltpu.VMEM((2, TOP_K, tt * ROW_TILE, LANES), F32), pltpu.SemaphoreType.DMA((2,))],
        compiler_params=_cparams(("arbitrary",)),
    )(dest3, x1r, gates, ys)


def _moe(x1, logits, norm2_g, w_gate_up, b_gate_up, w_down, b_down):
    t = x1.shape[0] // ROW_TILE
    n_assign = t * TOP_K
    n_blocks = -(-n_assign // MOE_BLOCK) + N_EXPERTS
    n_rows = n_blocks * MOE_BLOCK

    routed, cnt = _route(logits)
    ids = routed[:TOP_K].astype(jnp.int32)
    pos = routed[2 * TOP_K:3 * TOP_K].astype(jnp.int32)
    gates = jnp.zeros((t, LANES), F32).at[:, :TOP_K].set(routed[TOP_K:2 * TOP_K].T)
    counts = cnt[:, 0].astype(jnp.int32)
    padded = (counts + MOE_BLOCK - 1) // MOE_BLOCK * MOE_BLOCK
    pad_ends = jnp.cumsum(padded)
    pad_starts = pad_ends - padded
    experts = jnp.arange(N_EXPERTS, dtype=jnp.int32)[:, None, None]
    dest = jnp.sum(jnp.where(ids[None] == experts, pad_starts[:, None, None], 0), axis=0) + pos
    n_used = (pad_ends[-1] // MOE_BLOCK).astype(jnp.int32).reshape(1)

    tt = min(TT_MOVE, t)
    dest3 = dest.reshape(TOP_K, t // tt, tt).transpose(1, 0, 2).reshape(t // tt, 1, TOP_K * tt)
    xs = _dispatch(pad_ends.astype(jnp.int32), n_used, x1, dest3, n_rows)
    ys = _moe_ffn((pad_starts // MOE_BLOCK).astype(jnp.int32), (padded // MOE_BLOCK).astype(jnp.int32),
                  n_used, xs, norm2_g,
                  w_gate_up, b_gate_up.reshape(N_EXPERTS, 1, 2 * D_FF),
                  w_down, b_down.reshape(N_EXPERTS, 1, D_MODEL))
    return _combine(x1, gates, dest3, ys)


def _layer(x, norm1_g, w_in, ssm_a_re, ssm_a_im, ssm_log_dt, ssm_b_re, ssm_b_im, ssm_c_re,
           ssm_c_im, ssm_d, w_glu, w_ssm_proj, q_norm_g, k_norm_g, attn_sinks, w_attn_proj,
           w_out, norm2_g, w_router, b_router, w_gate_up, b_gate_up, w_down, b_down):
    n_b, seq, d = x.shape
    t = n_b * seq
    x2 = x.reshape(t, d)
    u, q, kd, vd, gs, ga = _in_proj(x2, norm1_g.reshape(1, d), w_in, q_norm_g, k_norm_g)

    abar_re, abar_im, bbt_re, bbt_im = _s5_prep(ssm_a_re, ssm_a_im, ssm_log_dt, ssm_b_re, ssm_b_im)
    b_mat, c_mat, ab = _s5_matrices(abar_re, abar_im, bbt_re, bbt_im, ssm_c_re, ssm_c_im)
    y = _s5(u.reshape(n_b, seq, SSM_WIDTH), b_mat, c_mat, ab, ssm_d.reshape(1, SSM_WIDTH))

    a = _attention(q.reshape(n_b, seq, ATTN_WIDTH), kd.reshape(n_b, seq, KV_DUP),
                   vd.reshape(n_b, seq, KV_DUP), attn_sinks)

    norm2 = norm2_g.reshape(1, d)
    x1, logits = _mix(y.reshape(t, SSM_WIDTH), a.reshape(t, ATTN_WIDTH), gs, ga, x2,
                      w_glu.astype(BF16), w_ssm_proj.astype(BF16), w_attn_proj.astype(BF16),
                      w_out.astype(BF16), norm2, w_router.T.astype(BF16), b_router)
    out = _moe(x1, logits, norm2, w_gate_up, b_gate_up, w_down, b_down)
    return out.reshape(n_b, seq, d)


def kernel(x, norm1_g, w_in, ssm_a_re, ssm_a_im, ssm_log_dt, ssm_b_re, ssm_b_im, ssm_c_re, ssm_c_im, ssm_d, w_glu, w_ssm_proj, q_norm_g, k_norm_g, attn_sinks, w_attn_proj, w_out, norm2_g, w_router, b_router, w_gate_up, b_gate_up, w_down, b_down):
    for layer in range(norm1_g.shape[0]):
        x = _layer(
            x, norm1_g[layer], w_in[layer], ssm_a_re[layer], ssm_a_im[layer], ssm_log_dt[layer],
            ssm_b_re[layer], ssm_b_im[layer], ssm_c_re[layer], ssm_c_im[layer], ssm_d[layer],
            w_glu[layer], w_ssm_proj[layer], q_norm_g[layer], k_norm_g[layer], attn_sinks[layer],
            w_attn_proj[layer], w_out[layer], norm2_g[layer], w_router[layer], b_router[layer],
            w_gate_up[layer], b_gate_up[layer], w_down[layer], b_down[layer])
    return x
```

```python
import functools

import jax
import jax.numpy as jnp
from jax import lax
from jax.experimental import pallas as pl
from jax.experimental.pallas import tpu as pltpu

D_MODEL = 1024
NORM_EPS = 1e-5
SSM_WIDTH = 1024
SSM_GROUP_CH = 16
SSM_GROUPS = SSM_WIDTH // SSM_GROUP_CH
SSM_STATE = 64
HEAD_DIM = 64
N_Q_HEADS = 16
N_KV_HEADS = 4
Q_PER_KV = N_Q_HEADS // N_KV_HEADS
ATTN_WIDTH = N_Q_HEADS * HEAD_DIM
KV_WIDTH = N_KV_HEADS * HEAD_DIM
WINDOW = 128
BLOCK = 128
N_EXPERTS = 32
TOP_K = 4
D_FF = 1024
SWIGLU_LIMIT = 7.0
SWIGLU_ALPHA = 1.702
MOE_BLOCK = 512

LANES = 128
SUBLANES = 8
VMEM_LIMIT = 56 * 1024 * 1024

S5_GROUP_BLOCK = 8
S5_CH = S5_GROUP_BLOCK * SSM_GROUP_CH
S5_ST = S5_GROUP_BLOCK * SSM_STATE
S5_CHUNK = 512
S5_SEG = 64
assert S5_CHUNK // S5_SEG >= 4 and (S5_CHUNK // S5_SEG) % 2 == 0
S5_PITCH = S5_CHUNK + SUBLANES

TM_IN = 256
TM_MIX = 512
TT_ROUTE = 512
TT_MOVE = 512
assert TT_MOVE == TT_ROUTE

F32 = jnp.float32
BF16 = jnp.bfloat16


def _cparams(sem):
    return pltpu.CompilerParams(dimension_semantics=sem, vmem_limit_bytes=VMEM_LIMIT)


def _rms(x, g):
    return x * lax.rsqrt(jnp.mean(x * x, axis=-1, keepdims=True) + NORM_EPS) * g


KV_DUP = N_KV_HEADS * LANES


def _in_proj_body(x_ref, g_ref, w_ref, qg_ref, kg_ref, u_ref, q_ref, kd_ref, vd_ref, gs_ref, ga_ref):
    h = _rms(x_ref[...], g_ref[...]).astype(BF16)
    off = [0]

    def proj(width):
        lo = off[0]
        off[0] = lo + width
        return jnp.dot(h, w_ref[:, lo:lo + width], preferred_element_type=F32)

    u_ref[...] = proj(SSM_WIDTH)

    q = proj(ATTN_WIDTH)
    lo_sel = lax.broadcasted_iota(jnp.int32, (q.shape[0], LANES), 1) < HEAD_DIM
    for j in range(ATTN_WIDTH // LANES):
        qq = q[:, j * LANES:(j + 1) * LANES]
        sq = qq * qq
        ms_lo = jnp.sum(jnp.where(lo_sel, sq, 0.0), axis=-1, keepdims=True) / HEAD_DIM
        ms_hi = jnp.sum(jnp.where(lo_sel, 0.0, sq), axis=-1, keepdims=True) / HEAD_DIM
        r = jnp.where(lo_sel, lax.rsqrt(ms_lo + NORM_EPS), lax.rsqrt(ms_hi + NORM_EPS))
        q_ref[:, j * LANES:(j + 1) * LANES] = (qq * r * qg_ref[...]).astype(q_ref.dtype)

    kd = proj(KV_DUP)
    for kv in range(N_KV_HEADS):
        grp = slice(kv * LANES, (kv + 1) * LANES)
        kd_ref[:, grp] = _rms(kd[:, grp], kg_ref[...]).astype(kd_ref.dtype)
    vd_ref[...] = proj(KV_DUP).astype(vd_ref.dtype)
    gs_ref[...] = proj(D_MODEL)
    ga_ref[...] = proj(D_MODEL)


def _in_proj_weight(w_in):
    d = w_in.shape[0]
    cuts = (SSM_WIDTH, SSM_WIDTH + ATTN_WIDTH, SSM_WIDTH + ATTN_WIDTH + KV_WIDTH,
            SSM_WIDTH + ATTN_WIDTH + 2 * KV_WIDTH)
    w_uq, w_k, w_v, w_g = (w_in[:, :cuts[1]], w_in[:, cuts[1]:cuts[2]], w_in[:, cuts[2]:cuts[3]],
                           w_in[:, cuts[3]:])

    def twice(w):
        w = w.reshape(d, N_KV_HEADS, 1, HEAD_DIM)
        return jnp.concatenate([w, w], axis=2).reshape(d, KV_DUP)

    return jnp.concatenate([w_uq, twice(w_k), twice(w_v), w_g], axis=1).astype(BF16)


def _in_proj(x2, norm_g, w_in, q_norm_g, k_norm_g):
    t = x2.shape[0]
    w = _in_proj_weight(w_in)
    qg = (jnp.tile(q_norm_g, LANES // HEAD_DIM) * HEAD_DIM ** -0.5).reshape(1, LANES)
    kg = jnp.tile(k_norm_g, LANES // HEAD_DIM).reshape(1, LANES)
    outs = ((SSM_WIDTH, F32), (ATTN_WIDTH, BF16), (KV_DUP, BF16), (KV_DUP, BF16),
            (D_MODEL, F32), (D_MODEL, F32))
    const = lambda a: pl.BlockSpec(a.shape, lambda i: (0, 0))
    return pl.pallas_call(
        _in_proj_body,
        name="in_proj",
        grid=(t // TM_IN,),
        in_specs=[pl.BlockSpec((TM_IN, D_MODEL), lambda i: (i, 0)), const(norm_g), const(w),
                  const(qg), const(kg)],
        out_specs=[pl.BlockSpec((TM_IN, n), lambda i: (i, 0)) for n, _ in outs],
        out_shape=[jax.ShapeDtypeStruct((t, n), dt) for n, dt in outs],
        compiler_params=_cparams(("parallel",)),
    )(x2, norm_g, w, qg, kg)


def _s5_prep_body(are_ref, aim_ref, ldt_ref, bre_ref, bim_ref,
                  abr_ref, abi_ref, bbr_ref, bbi_ref):
    a_re = are_ref[...]
    a_im = aim_ref[...]
    dt = jnp.exp(ldt_ref[...])
    mag = jnp.exp(a_re * dt)
    abar_re = mag * jnp.cos(a_im * dt)
    abar_im = mag * jnp.sin(a_im * dt)
    den = a_re * a_re + a_im * a_im
    q_re = ((abar_re - 1.0) * a_re + abar_im * a_im) / den
    q_im = (abar_im * a_re - (abar_re - 1.0) * a_im) / den
    abr_ref[...] = abar_re
    abi_ref[...] = abar_im
    for h in range(SSM_GROUP_CH):
        bbr_ref[h] = q_re * bre_ref[h] - q_im * bim_ref[h]
        bbi_ref[h] = q_re * bim_ref[h] + q_im * bre_ref[h]


def _s5_prep(a_re, a_im, log_dt, b_re, b_im):
    g, p, hc = SSM_GROUPS, SSM_STATE, SSM_GROUP_CH
    bt_re = jnp.transpose(b_re, (2, 0, 1))
    bt_im = jnp.transpose(b_im, (2, 0, 1))
    return pl.pallas_call(
        _s5_prep_body,
        name="s5_prep",
        out_shape=[
            jax.ShapeDtypeStruct((g, p), F32),
            jax.ShapeDtypeStruct((g, p), F32),
            jax.ShapeDtypeStruct((hc, g, p), F32),
            jax.ShapeDtypeStruct((hc, g, p), F32),
        ],
    )(a_re, a_im, log_dt.reshape(g, 1), bt_re, bt_im)


def _s5_matrices(abar_re, abar_im, bbt_re, bbt_im, c_re, c_im):
    nb, gb, hc, p = SSM_GROUPS // S5_GROUP_BLOCK, S5_GROUP_BLOCK, SSM_GROUP_CH, SSM_STATE
    eye = jnp.eye(gb, dtype=F32)

    def b_block(bbt):
        t = jnp.transpose(bbt, (1, 0, 2)).reshape(nb, gb, hc, p)
        return jnp.einsum('jghp,gk->jghkp', t, eye).reshape(nb, gb * hc, gb * p)

    def c_block(c):
        t = c.reshape(nb, gb, hc, p)
        return jnp.einsum('jghp,gk->jgpkh', t, eye).reshape(nb, gb * p, gb * hc)

    b_mat = jnp.concatenate([b_block(bbt_re), b_block(bbt_im)], axis=2).astype(BF16)
    c_mat = jnp.concatenate([c_block(c_re), c_block(-c_im)], axis=1).astype(BF16)
    ab = jnp.concatenate([abar_re.reshape(nb, 1, gb * p), abar_im.reshape(nb, 1, gb * p)], axis=2)
    ab = jnp.broadcast_to(ab, (nb, SUBLANES, 2 * gb * p))
    return b_mat, c_mat, ab


def _s5_body(u_ref, b_ref, c_ref, ab_ref, d_ref, o_ref, ubuf, utm, bu0, bu1, st0, st1, state, ytm):
    n_b = u_ref.shape[0]
    chunk = u_ref.shape[1]

    @pl.when(pl.program_id(1) == 0)
    def _():
        state[...] = jnp.zeros_like(state)

    for b in range(n_b):
        ubuf[b * S5_PITCH:b * S5_PITCH + chunk, :] = u_ref[b]

    def to_time_major(l8, carry):
        for i in range(SUBLANES):
            l = l8 * SUBLANES + i
            row = pl.multiple_of(l * n_b, SUBLANES)
            utm[pl.ds(row, n_b), :] = ubuf[pl.ds(l, n_b, stride=S5_PITCH), :]
        return carry

    lax.fori_loop(0, chunk // SUBLANES, to_time_major, 0)

    seg_rows = S5_SEG * n_b
    n_seg = chunk // S5_SEG
    a_re = ab_ref[:, :S5_ST]
    a_im = ab_ref[:, S5_ST:]

    bu = (bu0, bu1)
    st = (st0, st1)

    def seg(s):
        return pl.ds(pl.multiple_of(s * seg_rows, seg_rows), seg_rows)

    def b_proj(s, par):
        bu[par][...] = jnp.dot(utm[seg(s), :].astype(BF16), b_ref[...], preferred_element_type=F32)

    def c_proj(s, par):
        y = jnp.dot(st[par][...], c_ref[...], preferred_element_type=F32)
        ytm[seg(s), :] = jax.nn.gelu(y + d_ref[...] * utm[seg(s), :])

    def recurrence(par, carry):
        s_re, s_im = carry
        for i in range(0, S5_SEG, 2):
            both = []
            for ii in (i, i + 1):
                rows = slice(ii * n_b, (ii + 1) * n_b)
                n_re = a_re * s_re - a_im * s_im + bu[par][rows, :S5_ST]
                n_im = a_re * s_im + a_im * s_re + bu[par][rows, S5_ST:]
                both.append(jnp.concatenate([n_re, n_im], axis=1))
                s_re, s_im = n_re, n_im
            st[par][i * n_b:(i + 2) * n_b, :] = jnp.concatenate(both, axis=0).astype(BF16)
        return s_re, s_im

    def stage(s, par, carry):
        b_proj(s + 1, 1 - par)
        c_proj(s - 1, 1 - par)
        return recurrence(par, carry)

    def stage_pair(p, carry):
        s = 2 * p + 1
        carry = stage(s, 1, carry)
        return stage(s + 1, 0, carry)

    b_proj(0, 0)
    b_proj(1, 1)
    carry = recurrence(0, (state[:, :S5_ST], state[:, S5_ST:]))
    carry = lax.fori_loop(0, (n_seg - 2) // 2, stage_pair, carry)
    c_proj(n_seg - 2, 0)
    s_re, s_im = recurrence(1, carry)
    c_proj(n_seg - 1, 1)
    state[:, :S5_ST] = s_re
    state[:, S5_ST:] = s_im

    for b in range(n_b):
        o_ref[b] = ytm[pl.ds(b, chunk, stride=n_b), :].astype(o_ref.dtype)


def _s5(u3, b_mat, c_mat, ab, d_flat):
    n_b, seq, _ = u3.shape
    assert n_b == SUBLANES, "the scan keeps one sequence per sublane"
    chunk = min(S5_CHUNK, seq)
    nb = SSM_GROUPS // S5_GROUP_BLOCK
    return pl.pallas_call(
        _s5_body,
        name="s5_scan",
        grid=(nb, seq // chunk),
        in_specs=[
            pl.BlockSpec((n_b, chunk, S5_CH), lambda j, c: (0, c, j)),
            pl.BlockSpec((None, S5_CH, 2 * S5_ST), lambda j, c: (j, 0, 0)),
            pl.BlockSpec((None, 2 * S5_ST, S5_CH), lambda j, c: (j, 0, 0)),
            pl.BlockSpec((None, SUBLANES, 2 * S5_ST), lambda j, c: (j, 0, 0)),
            pl.BlockSpec((1, S5_CH), lambda j, c: (0, j)),
        ],
        out_specs=pl.BlockSpec((n_b, chunk, S5_CH), lambda j, c: (0, c, j)),
        out_shape=jax.ShapeDtypeStruct((n_b, seq, SSM_WIDTH), BF16),
        scratch_shapes=[
            pltpu.VMEM((n_b * S5_PITCH, S5_CH), F32),
            pltpu.VMEM((chunk * n_b, S5_CH), F32),
            pltpu.VMEM((S5_SEG * n_b, 2 * S5_ST), F32),
            pltpu.VMEM((S5_SEG * n_b, 2 * S5_ST), F32),
            pltpu.VMEM((S5_SEG * n_b, 2 * S5_ST), BF16),
            pltpu.VMEM((S5_SEG * n_b, 2 * S5_ST), BF16),
            pltpu.VMEM((n_b, 2 * S5_ST), F32),
            pltpu.VMEM((chunk * n_b, S5_CH), F32),
        ],
        compiler_params=_cparams(("parallel", "arbitrary")),
    )(u3, b_mat, c_mat, ab, d_flat)


def _attn_body(sinks_ref, q_ref, kp_ref, kc_ref, vp_ref, vc_ref, bias_ref, half_ref, o_ref):
    lo_mask = half_ref[0:1, :]
    hi_mask = half_ref[1:2, :]
    lo_sel = lax.broadcasted_iota(jnp.int32, (BLOCK, LANES), 1) < HEAD_DIM
    for kv in range(N_KV_HEADS):
        grp = slice(kv * LANES, (kv + 1) * LANES)
        kw = jnp.concatenate([kp_ref[:, grp], kc_ref[:, grp]], axis=0)
        vw = jnp.concatenate([vp_ref[:, grp], vc_ref[:, grp]], axis=0)
        qs = []
        for pair in range(Q_PER_KV // 2):
            lanes = slice((2 * kv + pair) * LANES, (2 * kv + pair + 1) * LANES)
            qp = q_ref[:, lanes]
            qs += [qp * lo_mask, qp * hi_mask]
        q_st = jnp.concatenate(qs, axis=0)
        s = lax.dot_general(q_st, kw, (((1,), (1,)), ((), ())), preferred_element_type=F32)
        s = s + bias_ref[kv]
        ps = []
        for g in range(Q_PER_KV):
            sg = s[g * BLOCK:(g + 1) * BLOCK]
            sink = sinks_ref[kv * Q_PER_KV + g]
            m = jnp.maximum(jnp.max(sg, axis=-1, keepdims=True), sink)
            p = jnp.exp(sg - m)
            den = jnp.sum(p, axis=-1, keepdims=True) + jnp.exp(sink - m)
            ps.append((p / den).astype(BF16))
        p_st = jnp.concatenate(ps, axis=0)
        o = jnp.dot(p_st, vw, preferred_element_type=F32)
        for pair in range(Q_PER_KV // 2):
            even = o[(2 * pair) * BLOCK:(2 * pair + 1) * BLOCK]
            odd = o[(2 * pair + 1) * BLOCK:(2 * pair + 2) * BLOCK]
            lanes = slice((2 * kv + pair) * LANES, (2 * kv + pair + 1) * LANES)
            o_ref[:, lanes] = jnp.where(lo_sel, even, odd).astype(o_ref.dtype)


def _attn_bias():
    h = jnp.arange(1, N_Q_HEADS + 1, dtype=F32)
    slopes = jnp.exp2(-8.0 * h / N_Q_HEADS)
    qi = jnp.arange(BLOCK)[:, None]
    kj = jnp.arange(2 * BLOCK)[None, :]
    dist = qi - kj + BLOCK
    band = (dist >= 0) & (dist < WINDOW)
    bias = -slopes[:, None, None] * dist.astype(F32)
    later = jnp.where(band[None], bias, -jnp.inf)
    first = jnp.where((band & (kj >= BLOCK))[None], bias, -jnp.inf)
    both = jnp.stack([first, later])
    return both.reshape(2, N_KV_HEADS, Q_PER_KV * BLOCK, 2 * BLOCK)


def _attention(q3, kd3, vd3, sinks):
    n_b, seq, _ = q3.shape
    n_blk = seq // BLOCK
    kv_dup = N_KV_HEADS * LANES
    prev = lambda b, i: (b, jnp.maximum(i - 1, 0), 0)
    cur = lambda b, i: (b, i, 0)
    lane = jnp.arange(LANES)
    half = jnp.stack([lane < HEAD_DIM, lane >= HEAD_DIM]).astype(BF16)
    return pl.pallas_call(
        _attn_body,
        name="swa_attention",
        grid=(n_b, n_blk),
        in_specs=[
            pl.BlockSpec(memory_space=pltpu.SMEM),
            pl.BlockSpec((None, BLOCK, ATTN_WIDTH), cur),
            pl.BlockSpec((None, BLOCK, kv_dup), prev),
            pl.BlockSpec((None, BLOCK, kv_dup), cur),
            pl.BlockSpec((None, BLOCK, kv_dup), prev),
            pl.BlockSpec((None, BLOCK, kv_dup), cur),
            pl.BlockSpec((None, N_KV_HEADS, Q_PER_KV * BLOCK, 2 * BLOCK),
                         lambda b, i: (jnp.minimum(i, 1), 0, 0, 0)),
            pl.BlockSpec((2, LANES), lambda b, i: (0, 0)),
        ],
        out_specs=pl.BlockSpec((None, BLOCK, ATTN_WIDTH), cur),
        out_shape=jax.ShapeDtypeStruct((n_b, seq, ATTN_WIDTH), BF16),
        compiler_params=_cparams(("parallel", "arbitrary")),
    )(sinks, q3, kd3, kd3, vd3, vd3, _attn_bias(), half)


ROW_TILE = D_MODEL // LANES
assert ROW_TILE == SUBLANES


def _store_row_tiles(ref, x):
    for i in range(x.shape[0] // SUBLANES):
        for j in range(ROW_TILE):
            ref[pl.ds(i * SUBLANES * ROW_TILE + j, SUBLANES, stride=ROW_TILE), :] = (
                x[i * SUBLANES:(i + 1) * SUBLANES, j * LANES:(j + 1) * LANES])


def _load_row_tiles(ref, n):
    cols = []
    for j in range(ROW_TILE):
        cols.append(jnp.concatenate(
            [ref[pl.ds(i * SUBLANES * ROW_TILE + j, SUBLANES, stride=ROW_TILE), :]
             for i in range(n // SUBLANES)], axis=0))
    return jnp.concatenate(cols, axis=1)


def _mix_body(y_ref, a_ref, gs_ref, ga_ref, x_ref, wglu_ref, wsp_ref, wap_ref, wo_ref,
              g2_ref, wr_ref, br_ref, x1_ref, lg_ref):
    z = jnp.dot(y_ref[...], wglu_ref[...], preferred_element_type=F32)
    glu = z[:, :SSM_WIDTH] * jax.nn.sigmoid(z[:, SSM_WIDTH:])
    y_ssm = jnp.dot(glu.astype(BF16), wsp_ref[...], preferred_element_type=F32)
    y_attn = jnp.dot(a_ref[...], wap_ref[...], preferred_element_type=F32)
    mixed = jax.nn.sigmoid(gs_ref[...]) * y_ssm + jax.nn.sigmoid(ga_ref[...]) * y_attn
    x1 = x_ref[...] + jnp.dot(mixed.astype(BF16), wo_ref[...], preferred_element_type=F32)
    _store_row_tiles(x1_ref, x1)
    h2 = _rms(x1, g2_ref[...]).astype(BF16)
    lg_ref[...] = lax.dot_general(wr_ref[...], h2, (((1,), (1,)), ((), ())),
                                  preferred_element_type=F32) + br_ref[...]


def _mix(y2, a2, gs, ga, x2, w_glu, w_sp, w_ap, w_o, norm2_g, w_r_t, b_r):
    t = x2.shape[0]
    tm = min(TM_MIX, t)
    b_r_wide = jnp.broadcast_to(b_r.reshape(N_EXPERTS, 1), (N_EXPERTS, tm))
    tok = lambda w: pl.BlockSpec((tm, w), lambda i: (i, 0))
    full = lambda a: pl.BlockSpec(a.shape, lambda i: (0, 0))
    return pl.pallas_call(
        _mix_body,
        name="mix",
        grid=(t // tm,),
        in_specs=[tok(SSM_WIDTH), tok(ATTN_WIDTH), tok(D_MODEL), tok(D_MODEL), tok(D_MODEL),
                  full(w_glu), full(w_sp), full(w_ap), full(w_o), full(norm2_g),
                  full(w_r_t), full(b_r_wide)],
        out_specs=[pl.BlockSpec((tm * ROW_TILE, LANES), lambda i: (i, 0)),
                   pl.BlockSpec((N_EXPERTS, tm), lambda i: (0, i))],
        out_shape=[jax.ShapeDtypeStruct((t * ROW_TILE, LANES), F32),
                   jax.ShapeDtypeStruct((N_EXPERTS, t), F32)],
        compiler_params=_cparams(("parallel",)),
    )(y2, a2, gs, ga, x2, w_glu, w_sp, w_ap, w_o, norm2_g, w_r_t, b_r_wide)


ROUTE_ROWS = 4 * TOP_K


def _route_body(lg_ref, tri_ref, out_ref, cnt_ref, start_ref, base):
    @pl.when(pl.program_id(0) == 0)
    def _():
        base[...] = jnp.zeros_like(base)

    start_ref[...] = base[:, :LANES]

    tt = lg_ref.shape[1]
    row = lax.broadcasted_iota(jnp.int32, (N_EXPERTS, tt), 0)
    l = lg_ref[...]
    tops, hots = [], []
    for _ in range(TOP_K):
        m = jnp.max(l, axis=0, keepdims=True)
        idx = jnp.min(jnp.where(l == m, row, N_EXPERTS), axis=0, keepdims=True)
        hot = row == idx
        tops.append((m, idx))
        hots.append(hot)
        l = jnp.where(hot, -jnp.inf, l)
    es = [jnp.exp(m - tops[0][0]) for m, _ in tops]
    den = es[0] + es[1] + es[2] + es[3]
    member = jnp.zeros((N_EXPERTS, tt), F32)
    for hot in hots:
        member = member + jnp.where(hot, 1.0, 0.0)
    before = jnp.dot(member.astype(BF16), tri_ref[...], preferred_element_type=F32) + base[:, :tt]
    rows = [idx.astype(F32) for _, idx in tops]
    rows += [e / den for e in es]
    rows += [jnp.sum(jnp.where(hot, before, 0.0), axis=0, keepdims=True) for hot in hots]
    rows += [jnp.zeros((1, tt), F32)] * (ROUTE_ROWS - len(rows))
    out_ref[...] = jnp.concatenate(rows, axis=0)
    base[...] = base[...] + jnp.sum(member, axis=1, keepdims=True)
    cnt_ref[...] = base[:, :LANES]


def _route(logits_t):
    t = logits_t.shape[1]
    tt = min(TT_ROUTE, t)
    tri = jnp.triu(jnp.ones((tt, tt), F32), 1).astype(BF16)
    return pl.pallas_call(
        _route_body,
        name="route",
        grid=(t // tt,),
        in_specs=[pl.BlockSpec((N_EXPERTS, tt), lambda i: (0, i)),
                  pl.BlockSpec((tt, tt), lambda i: (0, 0))],
        out_specs=[pl.BlockSpec((ROUTE_ROWS, tt), lambda i: (0, i)),
                   pl.BlockSpec((N_EXPERTS, LANES), lambda i: (0, 0)),
                   pl.BlockSpec((None, N_EXPERTS, LANES), lambda i: (i, 0, 0))],
        out_shape=[jax.ShapeDtypeStruct((ROUTE_ROWS, t), F32),
                   jax.ShapeDtypeStruct((N_EXPERTS, LANES), F32),
                   jax.ShapeDtypeStruct((t // tt, N_EXPERTS, LANES), F32)],
        scratch_shapes=[pltpu.VMEM((N_EXPERTS, tt), F32)],
        compiler_params=_cparams(("arbitrary",)),
    )(logits_t, tri)


def _row_copies(n_tok, issue_one):
    def body(t8, carry):
        for i in range(SUBLANES):
            issue_one(t8 * SUBLANES + i)
        return carry
    lax.fori_loop(0, n_tok // SUBLANES, body, 0)


def _row_tile(ref, row):
    return ref.at[pl.ds(pl.multiple_of(row * ROW_TILE, ROW_TILE), ROW_TILE)]


SEG_BITS = 10
SEG_CNT, SEG_SRC, SEG_DST = 0, N_EXPERTS, 2 * N_EXPERTS


def _runs(seg_ref, staged, rows, sem, act):
    for e in range(N_EXPERTS):
        n = seg_ref[0, 0, SEG_CNT + e]
        src = seg_ref[0, 0, SEG_SRC + e]
        dst = seg_ref[0, 0, SEG_DST + e]
        for b in reversed(range(SEG_BITS)):
            size = (1 << b) * ROW_TILE
            done = (n >> (b + 1)) << (b + 1)

            @pl.when((n & (1 << b)) != 0)
            def _():
                act(staged.at[pl.ds(pl.multiple_of((src + done) * ROW_TILE, ROW_TILE), size)],
                    rows.at[pl.ds(pl.multiple_of((dst + done) * ROW_TILE, ROW_TILE), size)], sem)


def _dispatch_body(pe_ref, nu_ref, q_ref, seg_ref, x_ref, xs_ref, zbuf, stage, sem, zsem):
    tt = x_ref.shape[0] // ROW_TILE
    blk_rows = MOE_BLOCK * ROW_TILE
    n_blocks = xs_ref.shape[0] // blk_rows

    @pl.when(pl.program_id(0) == 0)
    def _():
        zbuf[...] = jnp.zeros_like(zbuf)

        def zero_block(row):
            return pltpu.make_async_copy(
                zbuf, xs_ref.at[pl.ds(pl.multiple_of(row * ROW_TILE, blk_rows), blk_rows)], zsem)

        def each(act):
            for e in range(N_EXPERTS):
                @pl.when(pe_ref[e] >= MOE_BLOCK)
                def _():
                    act(zero_block(pe_ref[e] - MOE_BLOCK))

            def tail(i, carry):
                act(zero_block(i * MOE_BLOCK))
                return carry
            lax.fori_loop(nu_ref[0], n_blocks, tail, 0)

        each(lambda c: c.start())
        each(lambda c: c.wait())

    def place(t):
        row = x_ref[pl.ds(pl.multiple_of(t * ROW_TILE, ROW_TILE), ROW_TILE), :]
        for k in range(TOP_K):
            q = q_ref[0, 0, k * tt + t]
            stage[pl.ds(pl.multiple_of(q * ROW_TILE, ROW_TILE), ROW_TILE), :] = row

    _row_copies(tt, place)

    _runs(seg_ref, stage, xs_ref, sem,
          lambda src, dst, s: pltpu.make_async_copy(src, dst, s).start())
    for _ in range(TOP_K):
        pltpu.make_async_copy(x_ref, xs_ref.at[pl.ds(0, tt * ROW_TILE)], sem).wait()


def _dispatch(pad_ends, n_used, x1r, q3, seg3, n_rows):
    t = x1r.shape[0] // ROW_TILE
    tt = q3.shape[2] // TOP_K
    smem = lambda a: pl.BlockSpec((1, 1, a.shape[2]), lambda i, pe, nu: (i, 0, 0), memory_space=pltpu.SMEM)
    grid_spec = pltpu.PrefetchScalarGridSpec(
        num_scalar_prefetch=2,
        grid=(t // tt,),
        in_specs=[smem(q3), smem(seg3),
                  pl.BlockSpec((tt * ROW_TILE, LANES), lambda i, pe, nu: (i, 0))],
        out_specs=pl.BlockSpec(memory_space=pl.ANY),
        scratch_shapes=[pltpu.VMEM((MOE_BLOCK * ROW_TILE, LANES), F32),
                        pltpu.VMEM((TOP_K * tt * ROW_TILE, LANES), F32),
                        pltpu.SemaphoreType.DMA(()), pltpu.SemaphoreType.DMA(())],
    )
    return pl.pallas_call(
        _dispatch_body,
        name="dispatch",
        grid_spec=grid_spec,
        out_shape=jax.ShapeDtypeStruct((n_rows * ROW_TILE, LANES), F32),
        compiler_params=_cparams(("arbitrary",)),
    )(pad_ends, n_used, q3, seg3, x1r)


def _ffn_body(b0_ref, nb_ref, nu_ref, g_ref, wgu_ref, bgu_ref, wd_ref, bd_ref, xs_ref, ys_ref,
              xbuf, ybuf, in_sems, out_sems, wgu_bf, wd_bf):
    e = pl.program_id(0)
    blk_rows = MOE_BLOCK * ROW_TILE
    n_blocks = xs_ref.shape[0] // blk_rows
    first = b0_ref[e]
    n_used = nu_ref[0]

    def rows(b):
        return pl.ds(pl.multiple_of(b * blk_rows, blk_rows), blk_rows)

    def in_copy(b):
        return pltpu.make_async_copy(xs_ref.at[rows(b)], xbuf.at[b % 2], in_sems.at[b % 2])

    def out_copy(b):
        return pltpu.make_async_copy(ybuf.at[b % 2], ys_ref.at[rows(b)], out_sems.at[b % 2])

    @pl.when(e == 0)
    def _():
        in_copy(0).start(priority=1)

    wgu_bf[...] = wgu_ref[...].astype(BF16)
    wd_bf[...] = wd_ref[...].astype(BF16)

    def block(j, carry):
        b = first + j
        in_copy(b).wait()

        @pl.when(b + 1 < n_used)
        def _():
            in_copy(b + 1).start(priority=1)

        h = _rms(_load_row_tiles(xbuf.at[b % 2], MOE_BLOCK), g_ref[...]).astype(BF16)
        gu = jnp.dot(h, wgu_bf[...], preferred_element_type=F32) + bgu_ref[...]
        gate = jnp.minimum(gu[:, :D_FF], SWIGLU_LIMIT)
        up = jnp.clip(gu[:, D_FF:], -SWIGLU_LIMIT, SWIGLU_LIMIT)
        act = (up + 1.0) * gate * jax.nn.sigmoid(SWIGLU_ALPHA * gate)
        y = jnp.dot(act.astype(BF16), wd_bf[...], preferred_element_type=F32) + bd_ref[...]

        @pl.when(b >= 2)
        def _():
            out_copy(b - 2).wait()

        _store_row_tiles(ybuf.at[b % 2], y)
        out_copy(b).start(priority=1)
        return carry

    lax.fori_loop(0, nb_ref[e], block, 0)

    @pl.when(e == pl.num_programs(0) - 1)
    def _():
        @pl.when(n_used >= 2)
        def _():
            out_copy(n_used - 2).wait()
        out_copy(n_used - 1).wait()

        ybuf[0] = jnp.zeros((blk_rows, LANES), F32)

        def tail(act):
            def body(i, carry):
                act(pltpu.make_async_copy(
                    ybuf.at[0], ys_ref.at[pl.ds(pl.multiple_of(i * blk_rows, blk_rows), blk_rows)],
                    out_sems.at[0]))
                return carry
            lax.fori_loop(nu_ref[0], n_blocks, body, 0)

        tail(lambda c: c.start())
        tail(lambda c: c.wait())


def _moe_ffn(blk_first, blk_count, n_used, xs, norm2_g, w_gu, b_gu, w_d, b_d):
    blk_rows = MOE_BLOCK * ROW_TILE
    by_e = lambda e, b0, nb, nu: (e, 0, 0)
    grid_spec = pltpu.PrefetchScalarGridSpec(
        num_scalar_prefetch=3,
        grid=(N_EXPERTS,),
        in_specs=[
            pl.BlockSpec((1, D_MODEL), lambda e, b0, nb, nu: (0, 0)),
            pl.BlockSpec((None, D_MODEL, 2 * D_FF), by_e),
            pl.BlockSpec((None, 1, 2 * D_FF), by_e),
            pl.BlockSpec((None, D_FF, D_MODEL), by_e),
            pl.BlockSpec((None, 1, D_MODEL), by_e),
            pl.BlockSpec(memory_space=pl.ANY),
        ],
        out_specs=pl.BlockSpec(memory_space=pl.ANY),
        scratch_shapes=[pltpu.VMEM((2, blk_rows, LANES), F32), pltpu.VMEM((2, blk_rows, LANES), F32),
                        pltpu.SemaphoreType.DMA((2,)), pltpu.SemaphoreType.DMA((2,)),
                        pltpu.VMEM((D_MODEL, 2 * D_FF), BF16), pltpu.VMEM((D_FF, D_MODEL), BF16)],
    )
    return pl.pallas_call(
        _ffn_body,
        name="moe_ffn",
        grid_spec=grid_spec,
        out_shape=jax.ShapeDtypeStruct(xs.shape, F32),
        compiler_params=_cparams(("arbitrary",)),
    )(blk_first, blk_count, n_used, norm2_g, w_gu, b_gu, w_d, b_d, xs)


def _combine_body(dest_ref, x_ref, gate_ref, ys_ref, o_ref, buf, sems):
    i = pl.program_id(0)
    n_tiles = pl.num_programs(0) - 1
    tt = o_ref.shape[0]

    def gathers(slot):
        def issue(t):
            for k in range(TOP_K):
                d = dest_ref[0, 0, k * tt + t]
                pltpu.make_async_copy(_row_tile(ys_ref, d), _row_tile(buf.at[slot, k], t),
                                      sems.at[slot]).start(priority=k % 2)
        _row_copies(tt, issue)

    def finish(slot):
        for k in range(TOP_K):
            pltpu.make_async_copy(ys_ref.at[pl.ds(0, tt * ROW_TILE)], buf.at[slot, k], sems.at[slot]).wait()
        acc = _load_row_tiles(x_ref, tt)
        for k in range(TOP_K):
            acc = acc + gate_ref[:, k:k + 1] * _load_row_tiles(buf.at[slot, k], tt)
        o_ref[...] = acc

    for slot in range(2):
        @pl.when((i < n_tiles) & (i % 2 == slot))
        def _():
            gathers(slot)

        @pl.when((i > 0) & ((i - 1) % 2 == slot))
        def _():
            finish(slot)


def _combine(x1r, gates, dest3, ys):
    t = x1r.shape[0] // ROW_TILE
    tt = dest3.shape[2] // TOP_K
    n_tiles = t // tt
    ahead = lambda i: (jnp.minimum(i, n_tiles - 1), 0, 0)
    behind = lambda i: (jnp.maximum(i - 1, 0), 0)
    return pl.pallas_call(
        _combine_body,
        name="combine",
        grid=(n_tiles + 1,),
        in_specs=[pl.BlockSpec((1, 1, tt * TOP_K), ahead, memory_space=pltpu.SMEM),
                  pl.BlockSpec((tt * ROW_TILE, LANES), behind),
                  pl.BlockSpec((tt, LANES), behind),
                  pl.BlockSpec(memory_space=pl.ANY)],
        out_specs=pl.BlockSpec((tt, D_MODEL), behind),
        out_shape=jax.ShapeDtypeStruct((t, D_MODEL), F32),
        scratch_shapes=[pltpu.VMEM((2, TOP_K, tt * ROW_TILE, LANES), F32), pltpu.SemaphoreType.DMA((2,))],
        compiler_params=_cparams(("arbitrary",)),
    )(dest3, x1r, gates, ys)


def _moe(x1, logits, norm2_g, w_gate_up, b_gate_up, w_down, b_down):
    t = x1.shape[0] // ROW_TILE
    n_assign = t * TOP_K
    n_blocks = -(-n_assign // MOE_BLOCK) + N_EXPERTS
    n_rows = n_blocks * MOE_BLOCK

    routed, cnt, tile_start = _route(logits)
    ids = routed[:TOP_K].astype(jnp.int32)
    pos = routed[2 * TOP_K:3 * TOP_K].astype(jnp.int32)
    gates = jnp.zeros((t, LANES), F32).at[:, :TOP_K].set(routed[TOP_K:2 * TOP_K].T)
    counts = cnt[:, 0].astype(jnp.int32)
    padded = (counts + MOE_BLOCK - 1) // MOE_BLOCK * MOE_BLOCK
    pad_ends = jnp.cumsum(padded)
    pad_starts = pad_ends - padded
    experts = jnp.arange(N_EXPERTS, dtype=jnp.int32)[:, None, None]
    dest = jnp.sum(jnp.where(ids[None] == experts, pad_starts[:, None, None], 0), axis=0) + pos
    n_used = (pad_ends[-1] // MOE_BLOCK).astype(jnp.int32).reshape(1)

    tt = min(TT_MOVE, t)
    n_tiles = t // tt
    by_tile = lambda a: a.reshape(TOP_K, n_tiles, tt).transpose(1, 0, 2).reshape(n_tiles, 1, TOP_K * tt)
    dest3 = by_tile(dest)

    tile_start = tile_start[:, :, 0].astype(jnp.int32)
    tile_cnt = jnp.concatenate([tile_start[1:], counts[None]], axis=0) - tile_start
    tile_off = jnp.cumsum(tile_cnt, axis=1) - tile_cnt
    seg_dst = pad_starts[None, :] + tile_start
    seg3 = jnp.concatenate([tile_cnt, tile_off, seg_dst, jnp.zeros_like(tile_cnt)], axis=1)
    seg3 = seg3.reshape(n_tiles, 1, 4 * N_EXPERTS)
    shift = jnp.repeat((tile_off - tile_start).T, tt, axis=1)
    q = jnp.sum(jnp.where(ids[None] == experts, shift[:, None, :], 0), axis=0) + pos
    xs = _dispatch(pad_ends.astype(jnp.int32), n_used, x1, by_tile(q), seg3, n_rows)
    ys = _moe_ffn((pad_starts // MOE_BLOCK).astype(jnp.int32), (padded // MOE_BLOCK).astype(jnp.int32),
                  n_used, xs, norm2_g,
                  w_gate_up, b_gate_up.reshape(N_EXPERTS, 1, 2 * D_FF),
                  w_down, b_down.reshape(N_EXPERTS, 1, D_MODEL))
    return _combine(x1, gates, dest3, ys)


def _layer(x, norm1_g, w_in, ssm_a_re, ssm_a_im, ssm_log_dt, ssm_b_re, ssm_b_im, ssm_c_re,
           ssm_c_im, ssm_d, w_glu, w_ssm_proj, q_norm_g, k_norm_g, attn_sinks, w_attn_proj,
           w_out, norm2_g, w_router, b_router, w_gate_up, b_gate_up, w_down, b_down):
    n_b, seq, d = x.shape
    t = n_b * seq
    x2 = x.reshape(t, d)
    u, q, kd, vd, gs, ga = _in_proj(x2, norm1_g.reshape(1, d), w_in, q_norm_g, k_norm_g)

    abar_re, abar_im, bbt_re, bbt_im = _s5_prep(ssm_a_re, ssm_a_im, ssm_log_dt, ssm_b_re, ssm_b_im)
    b_mat, c_mat, ab = _s5_matrices(abar_re, abar_im, bbt_re, bbt_im, ssm_c_re, ssm_c_im)
    y = _s5(u.reshape(n_b, seq, SSM_WIDTH), b_mat, c_mat, ab, ssm_d.reshape(1, SSM_WIDTH))

    a = _attention(q.reshape(n_b, seq, ATTN_WIDTH), kd.reshape(n_b, seq, KV_DUP),
                   vd.reshape(n_b, seq, KV_DUP), attn_sinks)

    norm2 = norm2_g.reshape(1, d)
    x1, logits = _mix(y.reshape(t, SSM_WIDTH), a.reshape(t, ATTN_WIDTH), gs, ga, x2,
                      w_glu.astype(BF16), w_ssm_proj.astype(BF16), w_attn_proj.astype(BF16),
                      w_out.astype(BF16), norm2, w_router.T.astype(BF16), b_router)
    out = _moe(x1, logits, norm2, w_gate_up, b_gate_up, w_down, b_down)
    return out.reshape(n_b, seq, d)


def kernel(x, norm1_g, w_in, ssm_a_re, ssm_a_im, ssm_log_dt, ssm_b_re, ssm_b_im, ssm_c_re, ssm_c_im, ssm_d, w_glu, w_ssm_proj, q_norm_g, k_norm_g, attn_sinks, w_attn_proj, w_out, norm2_g, w_router, b_router, w_gate_up, b_gate_up, w_down, b_down):
    for layer in range(norm1_g.shape[0]):
        x = _layer(
            x, norm1_g[layer], w_in[layer], ssm_a_re[layer], ssm_a_im[layer], ssm_log_dt[layer],
            ssm_b_re[layer], ssm_b_im[layer], ssm_c_re[layer], ssm_c_im[layer], ssm_d[layer],
            w_glu[layer], w_ssm_proj[layer], q_norm_g[layer], k_norm_g[layer], attn_sinks[layer],
            w_attn_proj[layer], w_out[layer], norm2_g[layer], w_router[layer], b_router[layer],
            w_gate_up[layer], b_gate_up[layer], w_down[layer], b_down[layer])
    return x
```

```python
import functools

import jax
import jax.numpy as jnp
from jax import lax
from jax.experimental import pallas as pl
from jax.experimental.pallas import tpu as pltpu

D_MODEL = 1024
NORM_EPS = 1e-5
SSM_WIDTH = 1024
SSM_GROUP_CH = 16
SSM_GROUPS = SSM_WIDTH // SSM_GROUP_CH
SSM_STATE = 64
HEAD_DIM = 64
N_Q_HEADS = 16
N_KV_HEADS = 4
Q_PER_KV = N_Q_HEADS // N_KV_HEADS
ATTN_WIDTH = N_Q_HEADS * HEAD_DIM
KV_WIDTH = N_KV_HEADS * HEAD_DIM
WINDOW = 128
BLOCK = 128
N_EXPERTS = 32
TOP_K = 4
D_FF = 1024
SWIGLU_LIMIT = 7.0
SWIGLU_ALPHA = 1.702
MOE_BLOCK = 512

LANES = 128
SUBLANES = 8
VMEM_LIMIT = 56 * 1024 * 1024

S5_GROUP_BLOCK = 8
S5_CH = S5_GROUP_BLOCK * SSM_GROUP_CH
S5_ST = S5_GROUP_BLOCK * SSM_STATE
S5_CHUNK = 512
S5_SEG = 64
assert S5_CHUNK // S5_SEG >= 4 and (S5_CHUNK // S5_SEG) % 2 == 0
S5_PITCH = S5_CHUNK + SUBLANES

TM_IN = 256
TM_MIX = 512
TT_ROUTE = 512
TT_MOVE = 512
assert TT_MOVE == TT_ROUTE

F32 = jnp.float32
BF16 = jnp.bfloat16


def _cparams(sem):
    return pltpu.CompilerParams(dimension_semantics=sem, vmem_limit_bytes=VMEM_LIMIT)


def _rms(x, g):
    return x * lax.rsqrt(jnp.mean(x * x, axis=-1, keepdims=True) + NORM_EPS) * g


KV_DUP = N_KV_HEADS * LANES


def _in_proj_body(x_ref, g_ref, w_ref, qg_ref, kg_ref, u_ref, q_ref, kd_ref, vd_ref, gs_ref, ga_ref):
    h = _rms(x_ref[...], g_ref[...]).astype(BF16)
    off = [0]

    def proj(width):
        lo = off[0]
        off[0] = lo + width
        return jnp.dot(h, w_ref[:, lo:lo + width], preferred_element_type=F32)

    u_ref[...] = proj(SSM_WIDTH)

    q = proj(ATTN_WIDTH)
    lo_sel = lax.broadcasted_iota(jnp.int32, (q.shape[0], LANES), 1) < HEAD_DIM
    for j in range(ATTN_WIDTH // LANES):
        qq = q[:, j * LANES:(j + 1) * LANES]
        sq = qq * qq
        ms_lo = jnp.sum(jnp.where(lo_sel, sq, 0.0), axis=-1, keepdims=True) / HEAD_DIM
        ms_hi = jnp.sum(jnp.where(lo_sel, 0.0, sq), axis=-1, keepdims=True) / HEAD_DIM
        r = jnp.where(lo_sel, lax.rsqrt(ms_lo + NORM_EPS), lax.rsqrt(ms_hi + NORM_EPS))
        q_ref[:, j * LANES:(j + 1) * LANES] = (qq * r * qg_ref[...]).astype(q_ref.dtype)

    kd = proj(KV_DUP)
    for kv in range(N_KV_HEADS):
        grp = slice(kv * LANES, (kv + 1) * LANES)
        kd_ref[:, grp] = _rms(kd[:, grp], kg_ref[...]).astype(kd_ref.dtype)
    vd_ref[...] = proj(KV_DUP).astype(vd_ref.dtype)
    gs_ref[...] = proj(D_MODEL)
    ga_ref[...] = proj(D_MODEL)


def _in_proj_weight(w_in):
    d = w_in.shape[0]
    cuts = (SSM_WIDTH, SSM_WIDTH + ATTN_WIDTH, SSM_WIDTH + ATTN_WIDTH + KV_WIDTH,
            SSM_WIDTH + ATTN_WIDTH + 2 * KV_WIDTH)
    w_uq, w_k, w_v, w_g = (w_in[:, :cuts[1]], w_in[:, cuts[1]:cuts[2]], w_in[:, cuts[2]:cuts[3]],
                           w_in[:, cuts[3]:])

    def twice(w):
        w = w.reshape(d, N_KV_HEADS, 1, HEAD_DIM)
        return jnp.concatenate([w, w], axis=2).reshape(d, KV_DUP)

    return jnp.concatenate([w_uq, twice(w_k), twice(w_v), w_g], axis=1).astype(BF16)


def _in_proj(x2, norm_g, w_in, q_norm_g, k_norm_g):
    t = x2.shape[0]
    w = _in_proj_weight(w_in)
    qg = (jnp.tile(q_norm_g, LANES // HEAD_DIM) * HEAD_DIM ** -0.5).reshape(1, LANES)
    kg = jnp.tile(k_norm_g, LANES // HEAD_DIM).reshape(1, LANES)
    outs = ((SSM_WIDTH, F32), (ATTN_WIDTH, BF16), (KV_DUP, BF16), (KV_DUP, BF16),
            (D_MODEL, F32), (D_MODEL, F32))
    const = lambda a: pl.BlockSpec(a.shape, lambda i: (0, 0))
    return pl.pallas_call(
        _in_proj_body,
        name="in_proj",
        grid=(t // TM_IN,),
        in_specs=[pl.BlockSpec((TM_IN, D_MODEL), lambda i: (i, 0)), const(norm_g), const(w),
                  const(qg), const(kg)],
        out_specs=[pl.BlockSpec((TM_IN, n), lambda i: (i, 0)) for n, _ in outs],
        out_shape=[jax.ShapeDtypeStruct((t, n), dt) for n, dt in outs],
        compiler_params=_cparams(("parallel",)),
    )(x2, norm_g, w, qg, kg)


def _s5_prep_body(are_ref, aim_ref, ldt_ref, bre_ref, bim_ref,
                  abr_ref, abi_ref, bbr_ref, bbi_ref):
    a_re = are_ref[...]
    a_im = aim_ref[...]
    dt = jnp.exp(ldt_ref[...])
    mag = jnp.exp(a_re * dt)
    abar_re = mag * jnp.cos(a_im * dt)
    abar_im = mag * jnp.sin(a_im * dt)
    den = a_re * a_re + a_im * a_im
    q_re = ((abar_re - 1.0) * a_re + abar_im * a_im) / den
    q_im = (abar_im * a_re - (abar_re - 1.0) * a_im) / den
    abr_ref[...] = abar_re
    abi_ref[...] = abar_im
    for h in range(SSM_GROUP_CH):
        bbr_ref[h] = q_re * bre_ref[h] - q_im * bim_ref[h]
        bbi_ref[h] = q_re * bim_ref[h] + q_im * bre_ref[h]


def _s5_prep(a_re, a_im, log_dt, b_re, b_im):
    g, p, hc = SSM_GROUPS, SSM_STATE, SSM_GROUP_CH
    bt_re = jnp.transpose(b_re, (2, 0, 1))
    bt_im = jnp.transpose(b_im, (2, 0, 1))
    return pl.pallas_call(
        _s5_prep_body,
        name="s5_prep",
        out_shape=[
            jax.ShapeDtypeStruct((g, p), F32),
            jax.ShapeDtypeStruct((g, p), F32),
            jax.ShapeDtypeStruct((hc, g, p), F32),
            jax.ShapeDtypeStruct((hc, g, p), F32),
        ],
    )(a_re, a_im, log_dt.reshape(g, 1), bt_re, bt_im)


def _s5_matrices(abar_re, abar_im, bbt_re, bbt_im, c_re, c_im):
    nb, gb, hc, p = SSM_GROUPS // S5_GROUP_BLOCK, S5_GROUP_BLOCK, SSM_GROUP_CH, SSM_STATE
    eye = jnp.eye(gb, dtype=F32)

    def b_block(bbt):
        t = jnp.transpose(bbt, (1, 0, 2)).reshape(nb, gb, hc, p)
        return jnp.einsum('jghp,gk->jghkp', t, eye).reshape(nb, gb * hc, gb * p)

    def c_block(c):
        t = c.reshape(nb, gb, hc, p)
        return jnp.einsum('jghp,gk->jgpkh', t, eye).reshape(nb, gb * p, gb * hc)

    b_mat = jnp.concatenate([b_block(bbt_re), b_block(bbt_im)], axis=2).astype(BF16)
    c_mat = jnp.concatenate([c_block(c_re), c_block(-c_im)], axis=1).astype(BF16)
    ab = jnp.concatenate([abar_re.reshape(nb, 1, gb * p), abar_im.reshape(nb, 1, gb * p)], axis=2)
    ab = jnp.broadcast_to(ab, (nb, SUBLANES, 2 * gb * p))
    return b_mat, c_mat, ab


def _s5_body(u_ref, b_ref, c_ref, ab_ref, d_ref, o_ref, ubuf, utm, bu0, bu1, st0, st1, state, ytm):
    n_b = u_ref.shape[0]
    chunk = u_ref.shape[1]

    @pl.when(pl.program_id(1) == 0)
    def _():
        state[...] = jnp.zeros_like(state)

    for b in range(n_b):
        ubuf[b * S5_PITCH:b * S5_PITCH + chunk, :] = u_ref[b]

    def to_time_major(l8, carry):
        for i in range(SUBLANES):
            l = l8 * SUBLANES + i
            row = pl.multiple_of(l * n_b, SUBLANES)
            utm[pl.ds(row, n_b), :] = ubuf[pl.ds(l, n_b, stride=S5_PITCH), :]
        return carry

    lax.fori_loop(0, chunk // SUBLANES, to_time_major, 0)

    seg_rows = S5_SEG * n_b
    n_seg = chunk // S5_SEG
    a_re = ab_ref[:, :S5_ST]
    a_im = ab_ref[:, S5_ST:]

    bu = (bu0, bu1)
    st = (st0, st1)

    def seg(s):
        return pl.ds(pl.multiple_of(s * seg_rows, seg_rows), seg_rows)

    def b_proj(s, par):
        bu[par][...] = jnp.dot(utm[seg(s), :].astype(BF16), b_ref[...], preferred_element_type=F32)

    def c_proj(s, par):
        y = jnp.dot(st[par][...], c_ref[...], preferred_element_type=F32)
        ytm[seg(s), :] = jax.nn.gelu(y + d_ref[...] * utm[seg(s), :])

    def recurrence(par, carry):
        s_re, s_im = carry
        for i in range(0, S5_SEG, 2):
            both = []
            for ii in (i, i + 1):
                rows = slice(ii * n_b, (ii + 1) * n_b)
                n_re = a_re * s_re - a_im * s_im + bu[par][rows, :S5_ST]
                n_im = a_re * s_im + a_im * s_re + bu[par][rows, S5_ST:]
                both.append(jnp.concatenate([n_re, n_im], axis=1))
                s_re, s_im = n_re, n_im
            st[par][i * n_b:(i + 2) * n_b, :] = jnp.concatenate(both, axis=0).astype(BF16)
        return s_re, s_im

    def stage(s, par, carry):
        b_proj(s + 1, 1 - par)
        c_proj(s - 1, 1 - par)
        return recurrence(par, carry)

    def stage_pair(p, carry):
        s = 2 * p + 1
        carry = stage(s, 1, carry)
        return stage(s + 1, 0, carry)

    b_proj(0, 0)
    b_proj(1, 1)
    carry = recurrence(0, (state[:, :S5_ST], state[:, S5_ST:]))
    carry = lax.fori_loop(0, (n_seg - 2) // 2, stage_pair, carry)
    c_proj(n_seg - 2, 0)
    s_re, s_im = recurrence(1, carry)
    c_proj(n_seg - 1, 1)
    state[:, :S5_ST] = s_re
    state[:, S5_ST:] = s_im

    for b in range(n_b):
        o_ref[b] = ytm[pl.ds(b, chunk, stride=n_b), :].astype(o_ref.dtype)


def _s5(u3, b_mat, c_mat, ab, d_flat):
    n_b, seq, _ = u3.shape
    assert n_b == SUBLANES, "the scan keeps one sequence per sublane"
    chunk = min(S5_CHUNK, seq)
    nb = SSM_GROUPS // S5_GROUP_BLOCK
    return pl.pallas_call(
        _s5_body,
        name="s5_scan",
        grid=(nb, seq // chunk),
        in_specs=[
            pl.BlockSpec((n_b, chunk, S5_CH), lambda j, c: (0, c, j)),
            pl.BlockSpec((None, S5_CH, 2 * S5_ST), lambda j, c: (j, 0, 0)),
            pl.BlockSpec((None, 2 * S5_ST, S5_CH), lambda j, c: (j, 0, 0)),
            pl.BlockSpec((None, SUBLANES, 2 * S5_ST), lambda j, c: (j, 0, 0)),
            pl.BlockSpec((1, S5_CH), lambda j, c: (0, j)),
        ],
        out_specs=pl.BlockSpec((n_b, chunk, S5_CH), lambda j, c: (0, c, j)),
        out_shape=jax.ShapeDtypeStruct((n_b, seq, SSM_WIDTH), BF16),
        scratch_shapes=[
            pltpu.VMEM((n_b * S5_PITCH, S5_CH), F32),
            pltpu.VMEM((chunk * n_b, S5_CH), F32),
            pltpu.VMEM((S5_SEG * n_b, 2 * S5_ST), F32),
            pltpu.VMEM((S5_SEG * n_b, 2 * S5_ST), F32),
            pltpu.VMEM((S5_SEG * n_b, 2 * S5_ST), BF16),
            pltpu.VMEM((S5_SEG * n_b, 2 * S5_ST), BF16),
            pltpu.VMEM((n_b, 2 * S5_ST), F32),
            pltpu.VMEM((chunk * n_b, S5_CH), F32),
        ],
        compiler_params=_cparams(("parallel", "arbitrary")),
    )(u3, b_mat, c_mat, ab, d_flat)


def _attn_body(sinks_ref, q_ref, kp_ref, kc_ref, vp_ref, vc_ref, bias_ref, half_ref, o_ref):
    lo_mask = half_ref[0:1, :]
    hi_mask = half_ref[1:2, :]
    lo_sel = lax.broadcasted_iota(jnp.int32, (BLOCK, LANES), 1) < HEAD_DIM
    for kv in range(N_KV_HEADS):
        grp = slice(kv * LANES, (kv + 1) * LANES)
        kw = jnp.concatenate([kp_ref[:, grp], kc_ref[:, grp]], axis=0)
        vw = jnp.concatenate([vp_ref[:, grp], vc_ref[:, grp]], axis=0)
        qs = []
        for pair in range(Q_PER_KV // 2):
            lanes = slice((2 * kv + pair) * LANES, (2 * kv + pair + 1) * LANES)
            qp = q_ref[:, lanes]
            qs += [qp * lo_mask, qp * hi_mask]
        q_st = jnp.concatenate(qs, axis=0)
        s = lax.dot_general(q_st, kw, (((1,), (1,)), ((), ())), preferred_element_type=F32)
        s = s + bias_ref[kv]
        ps = []
        for g in range(Q_PER_KV):
            sg = s[g * BLOCK:(g + 1) * BLOCK]
            sink = sinks_ref[kv * Q_PER_KV + g]
            m = jnp.maximum(jnp.max(sg, axis=-1, keepdims=True), sink)
            p = jnp.exp(sg - m)
            den = jnp.sum(p, axis=-1, keepdims=True) + jnp.exp(sink - m)
            ps.append((p / den).astype(BF16))
        p_st = jnp.concatenate(ps, axis=0)
        o = jnp.dot(p_st, vw, preferred_element_type=F32)
        for pair in range(Q_PER_KV // 2):
            even = o[(2 * pair) * BLOCK:(2 * pair + 1) * BLOCK]
            odd = o[(2 * pair + 1) * BLOCK:(2 * pair + 2) * BLOCK]
            lanes = slice((2 * kv + pair) * LANES, (2 * kv + pair + 1) * LANES)
            o_ref[:, lanes] = jnp.where(lo_sel, even, odd).astype(o_ref.dtype)


def _attn_bias():
    h = jnp.arange(1, N_Q_HEADS + 1, dtype=F32)
    slopes = jnp.exp2(-8.0 * h / N_Q_HEADS)
    qi = jnp.arange(BLOCK)[:, None]
    kj = jnp.arange(2 * BLOCK)[None, :]
    dist = qi - kj + BLOCK
    band = (dist >= 0) & (dist < WINDOW)
    bias = -slopes[:, None, None] * dist.astype(F32)
    later = jnp.where(band[None], bias, -jnp.inf)
    first = jnp.where((band & (kj >= BLOCK))[None], bias, -jnp.inf)
    both = jnp.stack([first, later])
    return both.reshape(2, N_KV_HEADS, Q_PER_KV * BLOCK, 2 * BLOCK)


def _attention(q3, kd3, vd3, sinks):
    n_b, seq, _ = q3.shape
    n_blk = seq // BLOCK
    kv_dup = N_KV_HEADS * LANES
    prev = lambda b, i: (b, jnp.maximum(i - 1, 0), 0)
    cur = lambda b, i: (b, i, 0)
    lane = jnp.arange(LANES)
    half = jnp.stack([lane < HEAD_DIM, lane >= HEAD_DIM]).astype(BF16)
    return pl.pallas_call(
        _attn_body,
        name="swa_attention",
        grid=(n_b, n_blk),
        in_specs=[
            pl.BlockSpec(memory_space=pltpu.SMEM),
            pl.BlockSpec((None, BLOCK, ATTN_WIDTH), cur),
            pl.BlockSpec((None, BLOCK, kv_dup), prev),
            pl.BlockSpec((None, BLOCK, kv_dup), cur),
            pl.BlockSpec((None, BLOCK, kv_dup), prev),
            pl.BlockSpec((None, BLOCK, kv_dup), cur),
            pl.BlockSpec((None, N_KV_HEADS, Q_PER_KV * BLOCK, 2 * BLOCK),
                         lambda b, i: (jnp.minimum(i, 1), 0, 0, 0)),
            pl.BlockSpec((2, LANES), lambda b, i: (0, 0)),
        ],
        out_specs=pl.BlockSpec((None, BLOCK, ATTN_WIDTH), cur),
        out_shape=jax.ShapeDtypeStruct((n_b, seq, ATTN_WIDTH), BF16),
        compiler_params=_cparams(("parallel", "arbitrary")),
    )(sinks, q3, kd3, kd3, vd3, vd3, _attn_bias(), half)


ROW_TILE = D_MODEL // LANES
assert ROW_TILE == SUBLANES


def _store_row_tiles(ref, x):
    for i in range(x.shape[0] // SUBLANES):
        for j in range(ROW_TILE):
            ref[pl.ds(i * SUBLANES * ROW_TILE + j, SUBLANES, stride=ROW_TILE), :] = (
                x[i * SUBLANES:(i + 1) * SUBLANES, j * LANES:(j + 1) * LANES])


def _load_row_tiles(ref, n):
    cols = []
    for j in range(ROW_TILE):
        cols.append(jnp.concatenate(
            [ref[pl.ds(i * SUBLANES * ROW_TILE + j, SUBLANES, stride=ROW_TILE), :]
             for i in range(n // SUBLANES)], axis=0))
    return jnp.concatenate(cols, axis=1)


def _mix_body(y_ref, a_ref, gs_ref, ga_ref, x_ref, wglu_ref, wsp_ref, wap_ref, wo_ref,
              g2_ref, wr_ref, br_ref, x1_ref, lg_ref):
    z = jnp.dot(y_ref[...], wglu_ref[...], preferred_element_type=F32)
    glu = z[:, :SSM_WIDTH] * jax.nn.sigmoid(z[:, SSM_WIDTH:])
    y_ssm = jnp.dot(glu.astype(BF16), wsp_ref[...], preferred_element_type=F32)
    y_attn = jnp.dot(a_ref[...], wap_ref[...], preferred_element_type=F32)
    mixed = jax.nn.sigmoid(gs_ref[...]) * y_ssm + jax.nn.sigmoid(ga_ref[...]) * y_attn
    x1 = x_ref[...] + jnp.dot(mixed.astype(BF16), wo_ref[...], preferred_element_type=F32)
    _store_row_tiles(x1_ref, x1)
    h2 = _rms(x1, g2_ref[...]).astype(BF16)
    lg_ref[...] = lax.dot_general(wr_ref[...], h2, (((1,), (1,)), ((), ())),
                                  preferred_element_type=F32) + br_ref[...]


def _mix(y2, a2, gs, ga, x2, w_glu, w_sp, w_ap, w_o, norm2_g, w_r_t, b_r):
    t = x2.shape[0]
    tm = min(TM_MIX, t)
    b_r_wide = jnp.broadcast_to(b_r.reshape(N_EXPERTS, 1), (N_EXPERTS, tm))
    tok = lambda w: pl.BlockSpec((tm, w), lambda i: (i, 0))
    full = lambda a: pl.BlockSpec(a.shape, lambda i: (0, 0))
    return pl.pallas_call(
        _mix_body,
        name="mix",
        grid=(t // tm,),
        in_specs=[tok(SSM_WIDTH), tok(ATTN_WIDTH), tok(D_MODEL), tok(D_MODEL), tok(D_MODEL),
                  full(w_glu), full(w_sp), full(w_ap), full(w_o), full(norm2_g),
                  full(w_r_t), full(b_r_wide)],
        out_specs=[pl.BlockSpec((tm * ROW_TILE, LANES), lambda i: (i, 0)),
                   pl.BlockSpec((N_EXPERTS, tm), lambda i: (0, i))],
        out_shape=[jax.ShapeDtypeStruct((t * ROW_TILE, LANES), F32),
                   jax.ShapeDtypeStruct((N_EXPERTS, t), F32)],
        compiler_params=_cparams(("parallel",)),
    )(y2, a2, gs, ga, x2, w_glu, w_sp, w_ap, w_o, norm2_g, w_r_t, b_r_wide)


ROUTE_ROWS = 4 * TOP_K


def _route_body(lg_ref, tri_ref, out_ref, cnt_ref, start_ref, base):
    @pl.when(pl.program_id(0) == 0)
    def _():
        base[...] = jnp.zeros_like(base)

    start_ref[...] = base[:, :LANES]

    tt = lg_ref.shape[1]
    row = lax.broadcasted_iota(jnp.int32, (N_EXPERTS, tt), 0)
    l = lg_ref[...]
    tops, hots = [], []
    for _ in range(TOP_K):
        m = jnp.max(l, axis=0, keepdims=True)
        idx = jnp.min(jnp.where(l == m, row, N_EXPERTS), axis=0, keepdims=True)
        hot = row == idx
        tops.append((m, idx))
        hots.append(hot)
        l = jnp.where(hot, -jnp.inf, l)
    es = [jnp.exp(m - tops[0][0]) for m, _ in tops]
    den = es[0] + es[1] + es[2] + es[3]
    member = jnp.zeros((N_EXPERTS, tt), F32)
    for hot in hots:
        member = member + jnp.where(hot, 1.0, 0.0)
    before = jnp.dot(member.astype(BF16), tri_ref[...], preferred_element_type=F32) + base[:, :tt]
    rows = [idx.astype(F32) for _, idx in tops]
    rows += [e / den for e in es]
    rows += [jnp.sum(jnp.where(hot, before, 0.0), axis=0, keepdims=True) for hot in hots]
    rows += [jnp.zeros((1, tt), F32)] * (ROUTE_ROWS - len(rows))
    out_ref[...] = jnp.concatenate(rows, axis=0)
    base[...] = base[...] + jnp.sum(member, axis=1, keepdims=True)
    cnt_ref[...] = base[:, :LANES]


def _route(logits_t):
    t = logits_t.shape[1]
    tt = min(TT_ROUTE, t)
    tri = jnp.triu(jnp.ones((tt, tt), F32), 1).astype(BF16)
    return pl.pallas_call(
        _route_body,
        name="route",
        grid=(t // tt,),
        in_specs=[pl.BlockSpec((N_EXPERTS, tt), lambda i: (0, i)),
                  pl.BlockSpec((tt, tt), lambda i: (0, 0))],
        out_specs=[pl.BlockSpec((ROUTE_ROWS, tt), lambda i: (0, i)),
                   pl.BlockSpec((N_EXPERTS, LANES), lambda i: (0, 0)),
                   pl.BlockSpec((None, N_EXPERTS, LANES), lambda i: (i, 0, 0))],
        out_shape=[jax.ShapeDtypeStruct((ROUTE_ROWS, t), F32),
                   jax.ShapeDtypeStruct((N_EXPERTS, LANES), F32),
                   jax.ShapeDtypeStruct((t // tt, N_EXPERTS, LANES), F32)],
        scratch_shapes=[pltpu.VMEM((N_EXPERTS, tt), F32)],
        compiler_params=_cparams(("arbitrary",)),
    )(logits_t, tri)


def _row_copies(n_tok, issue_one):
    def body(t8, carry):
        for i in range(SUBLANES):
            issue_one(t8 * SUBLANES + i)
        return carry
    lax.fori_loop(0, n_tok // SUBLANES, body, 0)


def _row_tile(ref, row):
    return ref.at[pl.ds(pl.multiple_of(row * ROW_TILE, ROW_TILE), ROW_TILE)]


SEG_BITS = 10
SEG_CNT, SEG_SRC, SEG_DST = 0, N_EXPERTS, 2 * N_EXPERTS


def _runs(seg_ref, staged, rows, sem, act):
    for e in range(N_EXPERTS):
        n = seg_ref[0, 0, SEG_CNT + e]
        src = seg_ref[0, 0, SEG_SRC + e]
        dst = seg_ref[0, 0, SEG_DST + e]
        for b in reversed(range(SEG_BITS)):
            size = (1 << b) * ROW_TILE
            done = (n >> (b + 1)) << (b + 1)

            @pl.when((n & (1 << b)) != 0)
            def _():
                act(staged.at[pl.ds(pl.multiple_of((src + done) * ROW_TILE, ROW_TILE), size)],
                    rows.at[pl.ds(pl.multiple_of((dst + done) * ROW_TILE, ROW_TILE), size)], sem)


def _dispatch_body(pe_ref, nu_ref, q_ref, seg_ref, x_ref, xs_ref, zbuf, stage, sems, zsem):
    tt = x_ref.shape[0] // ROW_TILE
    blk_rows = MOE_BLOCK * ROW_TILE
    n_blocks = xs_ref.shape[0] // blk_rows

    @pl.when(pl.program_id(0) == 0)
    def _():
        zbuf[...] = jnp.zeros_like(zbuf)

        def zero_block(row):
            return pltpu.make_async_copy(
                zbuf, xs_ref.at[pl.ds(pl.multiple_of(row * ROW_TILE, blk_rows), blk_rows)], zsem)

        def each(act):
            for e in range(N_EXPERTS):
                @pl.when(pe_ref[e] >= MOE_BLOCK)
                def _():
                    act(zero_block(pe_ref[e] - MOE_BLOCK))

            def tail(i, carry):
                act(zero_block(i * MOE_BLOCK))
                return carry
            lax.fori_loop(nu_ref[0], n_blocks, tail, 0)

        each(lambda c: c.start())
        each(lambda c: c.wait())

    i = pl.program_id(0)
    slot = i % 2

    def wait_tile(s):
        for _ in range(TOP_K):
            pltpu.make_async_copy(x_ref, xs_ref.at[pl.ds(0, tt * ROW_TILE)], sems.at[s]).wait()

    def place(t):
        row = x_ref[pl.ds(pl.multiple_of(t * ROW_TILE, ROW_TILE), ROW_TILE), :]
        for k in range(TOP_K):
            q = q_ref[0, 0, k * tt + t]
            stage[slot, pl.ds(pl.multiple_of(q * ROW_TILE, ROW_TILE), ROW_TILE), :] = row

    _row_copies(tt, place)

    _runs(seg_ref, stage.at[slot], xs_ref, sems.at[slot],
          lambda src, dst, s: pltpu.make_async_copy(src, dst, s).start())

    @pl.when(i > 0)
    def _():
        wait_tile(1 - slot)

    @pl.when(i == pl.num_programs(0) - 1)
    def _():
        wait_tile(slot)


def _dispatch(pad_ends, n_used, x1r, q3, seg3, n_rows):
    t = x1r.shape[0] // ROW_TILE
    tt = q3.shape[2] // TOP_K
    smem = lambda a: pl.BlockSpec((1, 1, a.shape[2]), lambda i, pe, nu: (i, 0, 0), memory_space=pltpu.SMEM)
    grid_spec = pltpu.PrefetchScalarGridSpec(
        num_scalar_prefetch=2,
        grid=(t // tt,),
        in_specs=[smem(q3), smem(seg3),
                  pl.BlockSpec((tt * ROW_TILE, LANES), lambda i, pe, nu: (i, 0))],
        out_specs=pl.BlockSpec(memory_space=pl.ANY),
        scratch_shapes=[pltpu.VMEM((MOE_BLOCK * ROW_TILE, LANES), F32),
                        pltpu.VMEM((2, TOP_K * tt * ROW_TILE, LANES), F32),
                        pltpu.SemaphoreType.DMA((2,)), pltpu.SemaphoreType.DMA(())],
    )
    return pl.pallas_call(
        _dispatch_body,
        name="dispatch",
        grid_spec=grid_spec,
        out_shape=jax.ShapeDtypeStruct((n_rows * ROW_TILE, LANES), F32),
        compiler_params=_cparams(("arbitrary",)),
    )(pad_ends, n_used, q3, seg3, x1r)


def _ffn_body(b0_ref, nb_ref, nu_ref, g_ref, wgu_ref, bgu_ref, wd_ref, bd_ref, xs_ref, ys_ref,
              xbuf, ybuf, in_sems, out_sems, wgu_bf, wd_bf):
    e = pl.program_id(0)
    blk_rows = MOE_BLOCK * ROW_TILE
    n_blocks = xs_ref.shape[0] // blk_rows
    first = b0_ref[e]
    n_used = nu_ref[0]

    def rows(b):
        return pl.ds(pl.multiple_of(b * blk_rows, blk_rows), blk_rows)

    def in_copy(b):
        return pltpu.make_async_copy(xs_ref.at[rows(b)], xbuf.at[b % 2], in_sems.at[b % 2])

    def out_copy(b):
        return pltpu.make_async_copy(ybuf.at[b % 2], ys_ref.at[rows(b)], out_sems.at[b % 2])

    @pl.when(e == 0)
    def _():
        in_copy(0).start(priority=1)

    wgu_bf[...] = wgu_ref[...].astype(BF16)
    wd_bf[...] = wd_ref[...].astype(BF16)

    def block(j, carry):
        b = first + j
        in_copy(b).wait()

        @pl.when(b + 1 < n_used)
        def _():
            in_copy(b + 1).start(priority=1)

        h = _rms(_load_row_tiles(xbuf.at[b % 2], MOE_BLOCK), g_ref[...]).astype(BF16)
        gu = jnp.dot(h, wgu_bf[...], preferred_element_type=F32) + bgu_ref[...]
        gate = jnp.minimum(gu[:, :D_FF], SWIGLU_LIMIT)
        up = jnp.clip(gu[:, D_FF:], -SWIGLU_LIMIT, SWIGLU_LIMIT)
        act = (up + 1.0) * gate * jax.nn.sigmoid(SWIGLU_ALPHA * gate)
        y = jnp.dot(act.astype(BF16), wd_bf[...], preferred_element_type=F32) + bd_ref[...]

        @pl.when(b >= 2)
        def _():
            out_copy(b - 2).wait()

        _store_row_tiles(ybuf.at[b % 2], y)
        out_copy(b).start(priority=1)
        return carry

    lax.fori_loop(0, nb_ref[e], block, 0)

    @pl.when(e == pl.num_programs(0) - 1)
    def _():
        @pl.when(n_used >= 2)
        def _():
            out_copy(n_used - 2).wait()
        out_copy(n_used - 1).wait()

        ybuf[0] = jnp.zeros((blk_rows, LANES), F32)

        def tail(act):
            def body(i, carry):
                act(pltpu.make_async_copy(
                    ybuf.at[0], ys_ref.at[pl.ds(pl.multiple_of(i * blk_rows, blk_rows), blk_rows)],
                    out_sems.at[0]))
                return carry
            lax.fori_loop(nu_ref[0], n_blocks, body, 0)

        tail(lambda c: c.start())
        tail(lambda c: c.wait())


def _moe_ffn(blk_first, blk_count, n_used, xs, norm2_g, w_gu, b_gu, w_d, b_d):
    blk_rows = MOE_BLOCK * ROW_TILE
    by_e = lambda e, b0, nb, nu: (e, 0, 0)
    grid_spec = pltpu.PrefetchScalarGridSpec(
        num_scalar_prefetch=3,
        grid=(N_EXPERTS,),
        in_specs=[
            pl.BlockSpec((1, D_MODEL), lambda e, b0, nb, nu: (0, 0)),
            pl.BlockSpec((None, D_MODEL, 2 * D_FF), by_e),
            pl.BlockSpec((None, 1, 2 * D_FF), by_e),
            pl.BlockSpec((None, D_FF, D_MODEL), by_e),
            pl.BlockSpec((None, 1, D_MODEL), by_e),
            pl.BlockSpec(memory_space=pl.ANY),
        ],
        out_specs=pl.BlockSpec(memory_space=pl.ANY),
        scratch_shapes=[pltpu.VMEM((2, blk_rows, LANES), F32), pltpu.VMEM((2, blk_rows, LANES), F32),
                        pltpu.SemaphoreType.DMA((2,)), pltpu.SemaphoreType.DMA((2,)),
                        pltpu.VMEM((D_MODEL, 2 * D_FF), BF16), pltpu.VMEM((D_FF, D_MODEL), BF16)],
    )
    return pl.pallas_call(
        _ffn_body,
        name="moe_ffn",
        grid_spec=grid_spec,
        out_shape=jax.ShapeDtypeStruct(xs.shape, F32),
        compiler_params=_cparams(("arbitrary",)),
    )(blk_first, blk_count, n_used, norm2_g, w_gu, b_gu, w_d, b_d, xs)


def _combine_body(seg_ref, q_ref, gate_ref, x_ref, ys_ref, o_ref, stage, acc_rt, sems):
    i = pl.program_id(0)
    n_tiles = pl.num_programs(0) - 1
    tt = o_ref.shape[0]
    slot = i % 2

    @pl.when(i < n_tiles)
    def _():
        _runs(seg_ref, stage.at[slot], ys_ref, sems.at[slot],
              lambda staged, rows, s: pltpu.make_async_copy(rows, staged, s).start())

    @pl.when(i > 0)
    def _():
        for _ in range(TOP_K):
            pltpu.make_async_copy(ys_ref.at[pl.ds(0, tt * ROW_TILE)], acc_rt, sems.at[1 - slot]).wait()

        def token(t):
            row = pl.ds(pl.multiple_of(t * ROW_TILE, ROW_TILE), ROW_TILE)
            acc = x_ref[row, :]
            for k in range(TOP_K):
                q = q_ref[0, 0, k * tt + t]
                y = stage[1 - slot, pl.ds(pl.multiple_of(q * ROW_TILE, ROW_TILE), ROW_TILE), :]
                acc = acc + gate_ref[0, 0, k * tt + t] * y
            acc_rt[row, :] = acc

        _row_copies(tt, token)
        o_ref[...] = _load_row_tiles(acc_rt, tt)


def _combine(x1r, gates3, q3, seg3, ys):
    t = x1r.shape[0] // ROW_TILE
    tt = q3.shape[2] // TOP_K
    n_tiles = t // tt
    ahead = lambda i: (jnp.minimum(i, n_tiles - 1), 0, 0)
    behind3 = lambda i: (jnp.maximum(i - 1, 0), 0, 0)
    behind = lambda i: (jnp.maximum(i - 1, 0), 0)
    smem = lambda a, idx: pl.BlockSpec((1, 1, a.shape[2]), idx, memory_space=pltpu.SMEM)
    return pl.pallas_call(
        _combine_body,
        name="combine",
        grid=(n_tiles + 1,),
        in_specs=[smem(seg3, ahead), smem(q3, behind3), smem(gates3, behind3),
                  pl.BlockSpec((tt * ROW_TILE, LANES), behind),
                  pl.BlockSpec(memory_space=pl.ANY)],
        out_specs=pl.BlockSpec((tt, D_MODEL), behind),
        out_shape=jax.ShapeDtypeStruct((t, D_MODEL), F32),
        scratch_shapes=[pltpu.VMEM((2, TOP_K * tt * ROW_TILE, LANES), F32),
                        pltpu.VMEM((tt * ROW_TILE, LANES), F32),
                        pltpu.SemaphoreType.DMA((2,))],
        compiler_params=_cparams(("arbitrary",)),
    )(seg3, q3, gates3, x1r, ys)


def _moe(x1, logits, norm2_g, w_gate_up, b_gate_up, w_down, b_down):
    t = x1.shape[0] // ROW_TILE
    n_assign = t * TOP_K
    n_blocks = -(-n_assign // MOE_BLOCK) + N_EXPERTS
    n_rows = n_blocks * MOE_BLOCK

    routed, cnt, tile_start = _route(logits)
    ids = routed[:TOP_K].astype(jnp.int32)
    pos = routed[2 * TOP_K:3 * TOP_K].astype(jnp.int32)
    gates = routed[TOP_K:2 * TOP_K]
    counts = cnt[:, 0].astype(jnp.int32)
    padded = (counts + MOE_BLOCK - 1) // MOE_BLOCK * MOE_BLOCK
    pad_ends = jnp.cumsum(padded)
    pad_starts = pad_ends - padded
    n_used = (pad_ends[-1] // MOE_BLOCK).astype(jnp.int32).reshape(1)

    tt = min(TT_MOVE, t)
    n_tiles = t // tt
    by_tile = lambda a: a.reshape(TOP_K, n_tiles, tt).transpose(1, 0, 2).reshape(n_tiles, 1, TOP_K * tt)

    tile_start = tile_start[:, :, 0].astype(jnp.int32)
    tile_cnt = jnp.concatenate([tile_start[1:], counts[None]], axis=0) - tile_start
    tile_off = jnp.cumsum(tile_cnt, axis=1) - tile_cnt
    seg_dst = pad_starts[None, :] + tile_start
    seg3 = jnp.concatenate([tile_cnt, tile_off, seg_dst, jnp.zeros_like(tile_cnt)], axis=1)
    seg3 = seg3.reshape(n_tiles, 1, 4 * N_EXPERTS)
    experts = jnp.arange(N_EXPERTS, dtype=jnp.int32)[:, None, None]
    shift = jnp.repeat((tile_off - tile_start).T, tt, axis=1)
    q3 = by_tile(jnp.sum(jnp.where(ids[None] == experts, shift[:, None, :], 0), axis=0) + pos)
    xs = _dispatch(pad_ends.astype(jnp.int32), n_used, x1, q3, seg3, n_rows)
    ys = _moe_ffn((pad_starts // MOE_BLOCK).astype(jnp.int32), (padded // MOE_BLOCK).astype(jnp.int32),
                  n_used, xs, norm2_g,
                  w_gate_up, b_gate_up.reshape(N_EXPERTS, 1, 2 * D_FF),
                  w_down, b_down.reshape(N_EXPERTS, 1, D_MODEL))
    return _combine(x1, by_tile(gates), q3, seg3, ys)


def _layer(x, norm1_g, w_in, ssm_a_re, ssm_a_im, ssm_log_dt, ssm_b_re, ssm_b_im, ssm_c_re,
           ssm_c_im, ssm_d, w_glu, w_ssm_proj, q_norm_g, k_norm_g, attn_sinks, w_attn_proj,
           w_out, norm2_g, w_router, b_router, w_gate_up, b_gate_up, w_down, b_down):
    n_b, seq, d = x.shape
    t = n_b * seq
    x2 = x.reshape(t, d)
    u, q, kd, vd, gs, ga = _in_proj(x2, norm1_g.reshape(1, d), w_in, q_norm_g, k_norm_g)

    abar_re, abar_im, bbt_re, bbt_im = _s5_prep(ssm_a_re, ssm_a_im, ssm_log_dt, ssm_b_re, ssm_b_im)
    b_mat, c_mat, ab = _s5_matrices(abar_re, abar_im, bbt_re, bbt_im, ssm_c_re, ssm_c_im)
    y = _s5(u.reshape(n_b, seq, SSM_WIDTH), b_mat, c_mat, ab, ssm_d.reshape(1, SSM_WIDTH))

    a = _attention(q.reshape(n_b, seq, ATTN_WIDTH), kd.reshape(n_b, seq, KV_DUP),
                   vd.reshape(n_b, seq, KV_DUP), attn_sinks)

    norm2 = norm2_g.reshape(1, d)
    x1, logits = _mix(y.reshape(t, SSM_WIDTH), a.reshape(t, ATTN_WIDTH), gs, ga, x2,
                      w_glu.astype(BF16), w_ssm_proj.astype(BF16), w_attn_proj.astype(BF16),
                      w_out.astype(BF16), norm2, w_router.T.astype(BF16), b_router)
    out = _moe(x1, logits, norm2, w_gate_up, b_gate_up, w_down, b_down)
    return out.reshape(n_b, seq, d)


def kernel(x, norm1_g, w_in, ssm_a_re, ssm_a_im, ssm_log_dt, ssm_b_re, ssm_b_im, ssm_c_re, ssm_c_im, ssm_d, w_glu, w_ssm_proj, q_norm_g, k_norm_g, attn_sinks, w_attn_proj, w_out, norm2_g, w_router, b_router, w_gate_up, b_gate_up, w_down, b_down):
    for layer in range(norm1_g.shape[0]):
        x = _layer(
            x, norm1_g[layer], w_in[layer], ssm_a_re[layer], ssm_a_im[layer], ssm_log_dt[layer],
            ssm_b_re[layer], ssm_b_im[layer], ssm_c_re[layer], ssm_c_im[layer], ssm_d[layer],
            w_glu[layer], w_ssm_proj[layer], q_norm_g[layer], k_norm_g[layer], attn_sinks[layer],
            w_attn_proj[layer], w_out[layer], norm2_g[layer], w_router[layer], b_router[layer],
            w_gate_up[layer], b_gate_up[layer], w_down[layer], b_down[layer])
    return x
```

```python
import functools

import jax
import jax.numpy as jnp
from jax import lax
from jax.experimental import pallas as pl
from jax.experimental.pallas import tpu as pltpu

D_MODEL = 1024
NORM_EPS = 1e-5
SSM_WIDTH = 1024
SSM_GROUP_CH = 16
SSM_GROUPS = SSM_WIDTH // SSM_GROUP_CH
SSM_STATE = 64
HEAD_DIM = 64
N_Q_HEADS = 16
N_KV_HEADS = 4
Q_PER_KV = N_Q_HEADS // N_KV_HEADS
ATTN_WIDTH = N_Q_HEADS * HEAD_DIM
KV_WIDTH = N_KV_HEADS * HEAD_DIM
WINDOW = 128
BLOCK = 128
N_EXPERTS = 32
TOP_K = 4
D_FF = 1024
SWIGLU_LIMIT = 7.0
SWIGLU_ALPHA = 1.702
MOE_BLOCK = 512

LANES = 128
SUBLANES = 8
VMEM_LIMIT = 56 * 1024 * 1024

S5_GROUP_BLOCK = 8
S5_CH = S5_GROUP_BLOCK * SSM_GROUP_CH
S5_ST = S5_GROUP_BLOCK * SSM_STATE
S5_CHUNK = 512
S5_SEG = 64
assert S5_CHUNK // S5_SEG >= 4 and (S5_CHUNK // S5_SEG) % 2 == 0
S5_PITCH = S5_CHUNK + SUBLANES

TM_IN = 256
TM_MIX = 512
TT_ROUTE = 512
TT_MOVE = 512
assert TT_MOVE == TT_ROUTE

F32 = jnp.float32
BF16 = jnp.bfloat16


def _cparams(sem):
    return pltpu.CompilerParams(dimension_semantics=sem, vmem_limit_bytes=VMEM_LIMIT)


def _rms(x, g):
    return x * lax.rsqrt(jnp.mean(x * x, axis=-1, keepdims=True) + NORM_EPS) * g


KV_DUP = N_KV_HEADS * LANES


def _in_proj_body(x_ref, g_ref, w_ref, qg_ref, kg_ref, u_ref, q_ref, kd_ref, vd_ref, gs_ref, ga_ref):
    h = _rms(x_ref[...], g_ref[...]).astype(BF16)
    off = [0]

    def proj(width):
        lo = off[0]
        off[0] = lo + width
        return jnp.dot(h, w_ref[:, lo:lo + width], preferred_element_type=F32)

    u_ref[...] = proj(SSM_WIDTH)

    q = proj(ATTN_WIDTH)
    lo_sel = lax.broadcasted_iota(jnp.int32, (q.shape[0], LANES), 1) < HEAD_DIM
    for j in range(ATTN_WIDTH // LANES):
        qq = q[:, j * LANES:(j + 1) * LANES]
        sq = qq * qq
        ms_lo = jnp.sum(jnp.where(lo_sel, sq, 0.0), axis=-1, keepdims=True) / HEAD_DIM
        ms_hi = jnp.sum(jnp.where(lo_sel, 0.0, sq), axis=-1, keepdims=True) / HEAD_DIM
        r = jnp.where(lo_sel, lax.rsqrt(ms_lo + NORM_EPS), lax.rsqrt(ms_hi + NORM_EPS))
        q_ref[:, j * LANES:(j + 1) * LANES] = (qq * r * qg_ref[...]).astype(q_ref.dtype)

    kd = proj(KV_DUP)
    for kv in range(N_KV_HEADS):
        grp = slice(kv * LANES, (kv + 1) * LANES)
        kd_ref[:, grp] = _rms(kd[:, grp], kg_ref[...]).astype(kd_ref.dtype)
    vd_ref[...] = proj(KV_DUP).astype(vd_ref.dtype)
    gs_ref[...] = proj(D_MODEL)
    ga_ref[...] = proj(D_MODEL)


def _in_proj_weight(w_in):
    d = w_in.shape[0]
    cuts = (SSM_WIDTH, SSM_WIDTH + ATTN_WIDTH, SSM_WIDTH + ATTN_WIDTH + KV_WIDTH,
            SSM_WIDTH + ATTN_WIDTH + 2 * KV_WIDTH)
    w_uq, w_k, w_v, w_g = (w_in[:, :cuts[1]], w_in[:, cuts[1]:cuts[2]], w_in[:, cuts[2]:cuts[3]],
                           w_in[:, cuts[3]:])

    def twice(w):
        w = w.reshape(d, N_KV_HEADS, 1, HEAD_DIM)
        return jnp.concatenate([w, w], axis=2).reshape(d, KV_DUP)

    return jnp.concatenate([w_uq, twice(w_k), twice(w_v), w_g], axis=1).astype(BF16)


def _in_proj(x2, norm_g, w_in, q_norm_g, k_norm_g):
    t = x2.shape[0]
    w = _in_proj_weight(w_in)
    qg = (jnp.tile(q_norm_g, LANES // HEAD_DIM) * HEAD_DIM ** -0.5).reshape(1, LANES)
    kg = jnp.tile(k_norm_g, LANES // HEAD_DIM).reshape(1, LANES)
    outs = ((SSM_WIDTH, F32), (ATTN_WIDTH, BF16), (KV_DUP, BF16), (KV_DUP, BF16),
            (D_MODEL, F32), (D_MODEL, F32))
    const = lambda a: pl.BlockSpec(a.shape, lambda i: (0, 0))
    return pl.pallas_call(
        _in_proj_body,
        name="in_proj",
        grid=(t // TM_IN,),
        in_specs=[pl.BlockSpec((TM_IN, D_MODEL), lambda i: (i, 0)), const(norm_g), const(w),
                  const(qg), const(kg)],
        out_specs=[pl.BlockSpec((TM_IN, n), lambda i: (i, 0)) for n, _ in outs],
        out_shape=[jax.ShapeDtypeStruct((t, n), dt) for n, dt in outs],
        compiler_params=_cparams(("parallel",)),
    )(x2, norm_g, w, qg, kg)


def _s5_prep_body(are_ref, aim_ref, ldt_ref, bre_ref, bim_ref,
                  abr_ref, abi_ref, bbr_ref, bbi_ref):
    a_re = are_ref[...]
    a_im = aim_ref[...]
    dt = jnp.exp(ldt_ref[...])
    mag = jnp.exp(a_re * dt)
    abar_re = mag * jnp.cos(a_im * dt)
    abar_im = mag * jnp.sin(a_im * dt)
    den = a_re * a_re + a_im * a_im
    q_re = ((abar_re - 1.0) * a_re + abar_im * a_im) / den
    q_im = (abar_im * a_re - (abar_re - 1.0) * a_im) / den
    abr_ref[...] = abar_re
    abi_ref[...] = abar_im
    for h in range(SSM_GROUP_CH):
        bbr_ref[h] = q_re * bre_ref[h] - q_im * bim_ref[h]
        bbi_ref[h] = q_re * bim_ref[h] + q_im * bre_ref[h]


def _s5_prep(a_re, a_im, log_dt, b_re, b_im):
    g, p, hc = SSM_GROUPS, SSM_STATE, SSM_GROUP_CH
    bt_re = jnp.transpose(b_re, (2, 0, 1))
    bt_im = jnp.transpose(b_im, (2, 0, 1))
    return pl.pallas_call(
        _s5_prep_body,
        name="s5_prep",
        out_shape=[
            jax.ShapeDtypeStruct((g, p), F32),
            jax.ShapeDtypeStruct((g, p), F32),
            jax.ShapeDtypeStruct((hc, g, p), F32),
            jax.ShapeDtypeStruct((hc, g, p), F32),
        ],
    )(a_re, a_im, log_dt.reshape(g, 1), bt_re, bt_im)


def _s5_matrices(abar_re, abar_im, bbt_re, bbt_im, c_re, c_im):
    nb, gb, hc, p = SSM_GROUPS // S5_GROUP_BLOCK, S5_GROUP_BLOCK, SSM_GROUP_CH, SSM_STATE
    eye = jnp.eye(gb, dtype=F32)

    def b_block(bbt):
        t = jnp.transpose(bbt, (1, 0, 2)).reshape(nb, gb, hc, p)
        return jnp.einsum('jghp,gk->jghkp', t, eye).reshape(nb, gb * hc, gb * p)

    def c_block(c):
        t = c.reshape(nb, gb, hc, p)
        return jnp.einsum('jghp,gk->jgpkh', t, eye).reshape(nb, gb * p, gb * hc)

    b_mat = jnp.concatenate([b_block(bbt_re), b_block(bbt_im)], axis=2).astype(BF16)
    c_mat = jnp.concatenate([c_block(c_re), c_block(-c_im)], axis=1).astype(BF16)
    ab = jnp.concatenate([abar_re.reshape(nb, 1, gb * p), abar_im.reshape(nb, 1, gb * p)], axis=2)
    ab = jnp.broadcast_to(ab, (nb, SUBLANES, 2 * gb * p))
    return b_mat, c_mat, ab


def _s5_body(u_ref, b_ref, c_ref, ab_ref, d_ref, o_ref, ubuf, utm, bu0, bu1, st0, st1, state, ytm):
    n_b = u_ref.shape[0]
    chunk = u_ref.shape[1]

    @pl.when(pl.program_id(1) == 0)
    def _():
        state[...] = jnp.zeros_like(state)

    for b in range(n_b):
        ubuf[b * S5_PITCH:b * S5_PITCH + chunk, :] = u_ref[b]

    def to_time_major(l8, carry):
        for i in range(SUBLANES):
            l = l8 * SUBLANES + i
            row = pl.multiple_of(l * n_b, SUBLANES)
            utm[pl.ds(row, n_b), :] = ubuf[pl.ds(l, n_b, stride=S5_PITCH), :]
        return carry

    lax.fori_loop(0, chunk // SUBLANES, to_time_major, 0)

    seg_rows = S5_SEG * n_b
    n_seg = chunk // S5_SEG
    a_re = ab_ref[:, :S5_ST]
    a_im = ab_ref[:, S5_ST:]

    bu = (bu0, bu1)
    st = (st0, st1)

    def seg(s):
        return pl.ds(pl.multiple_of(s * seg_rows, seg_rows), seg_rows)

    def b_proj(s, par):
        bu[par][...] = jnp.dot(utm[seg(s), :].astype(BF16), b_ref[...], preferred_element_type=F32)

    def c_proj(s, par):
        y = jnp.dot(st[par][...], c_ref[...], preferred_element_type=F32)
        ytm[seg(s), :] = jax.nn.gelu(y + d_ref[...] * utm[seg(s), :])

    def recurrence(par, carry):
        s_re, s_im = carry
        for i in range(0, S5_SEG, 2):
            both = []
            for ii in (i, i + 1):
                rows = slice(ii * n_b, (ii + 1) * n_b)
                n_re = a_re * s_re - a_im * s_im + bu[par][rows, :S5_ST]
                n_im = a_re * s_im + a_im * s_re + bu[par][rows, S5_ST:]
                both.append(jnp.concatenate([n_re, n_im], axis=1))
                s_re, s_im = n_re, n_im
            st[par][i * n_b:(i + 2) * n_b, :] = jnp.concatenate(both, axis=0).astype(BF16)
        return s_re, s_im

    def stage(s, par, carry):
        b_proj(s + 1, 1 - par)
        c_proj(s - 1, 1 - par)
        return recurrence(par, carry)

    def stage_pair(p, carry):
        s = 2 * p + 1
        carry = stage(s, 1, carry)
        return stage(s + 1, 0, carry)

    b_proj(0, 0)
    b_proj(1, 1)
    carry = recurrence(0, (state[:, :S5_ST], state[:, S5_ST:]))
    carry = lax.fori_loop(0, (n_seg - 2) // 2, stage_pair, carry)
    c_proj(n_seg - 2, 0)
    s_re, s_im = recurrence(1, carry)
    c_proj(n_seg - 1, 1)
    state[:, :S5_ST] = s_re
    state[:, S5_ST:] = s_im

    for b in range(n_b):
        o_ref[b] = ytm[pl.ds(b, chunk, stride=n_b), :].astype(o_ref.dtype)


def _s5(u3, b_mat, c_mat, ab, d_flat):
    n_b, seq, _ = u3.shape
    assert n_b == SUBLANES, "the scan keeps one sequence per sublane"
    chunk = min(S5_CHUNK, seq)
    nb = SSM_GROUPS // S5_GROUP_BLOCK
    return pl.pallas_call(
        _s5_body,
        name="s5_scan",
        grid=(nb, seq // chunk),
        in_specs=[
            pl.BlockSpec((n_b, chunk, S5_CH), lambda j, c: (0, c, j)),
            pl.BlockSpec((None, S5_CH, 2 * S5_ST), lambda j, c: (j, 0, 0)),
            pl.BlockSpec((None, 2 * S5_ST, S5_CH), lambda j, c: (j, 0, 0)),
            pl.BlockSpec((None, SUBLANES, 2 * S5_ST), lambda j, c: (j, 0, 0)),
            pl.BlockSpec((1, S5_CH), lambda j, c: (0, j)),
        ],
        out_specs=pl.BlockSpec((n_b, chunk, S5_CH), lambda j, c: (0, c, j)),
        out_shape=jax.ShapeDtypeStruct((n_b, seq, SSM_WIDTH), BF16),
        scratch_shapes=[
            pltpu.VMEM((n_b * S5_PITCH, S5_CH), F32),
            pltpu.VMEM((chunk * n_b, S5_CH), F32),
            pltpu.VMEM((S5_SEG * n_b, 2 * S5_ST), F32),
            pltpu.VMEM((S5_SEG * n_b, 2 * S5_ST), F32),
            pltpu.VMEM((S5_SEG * n_b, 2 * S5_ST), BF16),
            pltpu.VMEM((S5_SEG * n_b, 2 * S5_ST), BF16),
            pltpu.VMEM((n_b, 2 * S5_ST), F32),
            pltpu.VMEM((chunk * n_b, S5_CH), F32),
        ],
        compiler_params=_cparams(("parallel", "arbitrary")),
    )(u3, b_mat, c_mat, ab, d_flat)


def _attn_body(sinks_ref, q_ref, kp_ref, kc_ref, vp_ref, vc_ref, bias_ref, half_ref, o_ref):
    lo_mask = half_ref[0:1, :]
    hi_mask = half_ref[1:2, :]
    lo_sel = lax.broadcasted_iota(jnp.int32, (BLOCK, LANES), 1) < HEAD_DIM
    for kv in range(N_KV_HEADS):
        grp = slice(kv * LANES, (kv + 1) * LANES)
        kw = jnp.concatenate([kp_ref[:, grp], kc_ref[:, grp]], axis=0)
        vw = jnp.concatenate([vp_ref[:, grp], vc_ref[:, grp]], axis=0)
        qs = []
        for pair in range(Q_PER_KV // 2):
            lanes = slice((2 * kv + pair) * LANES, (2 * kv + pair + 1) * LANES)
            qp = q_ref[:, lanes]
            qs += [qp * lo_mask, qp * hi_mask]
        q_st = jnp.concatenate(qs, axis=0)
        s = lax.dot_general(q_st, kw, (((1,), (1,)), ((), ())), preferred_element_type=F32)
        s = s + bias_ref[kv]
        ps = []
        for g in range(Q_PER_KV):
            sg = s[g * BLOCK:(g + 1) * BLOCK]
            sink = sinks_ref[kv * Q_PER_KV + g]
            m = jnp.maximum(jnp.max(sg, axis=-1, keepdims=True), sink)
            p = jnp.exp(sg - m)
            den = jnp.sum(p, axis=-1, keepdims=True) + jnp.exp(sink - m)
            ps.append((p / den).astype(BF16))
        p_st = jnp.concatenate(ps, axis=0)
        o = jnp.dot(p_st, vw, preferred_element_type=F32)
        for pair in range(Q_PER_KV // 2):
            even = o[(2 * pair) * BLOCK:(2 * pair + 1) * BLOCK]
            odd = o[(2 * pair + 1) * BLOCK:(2 * pair + 2) * BLOCK]
            lanes = slice((2 * kv + pair) * LANES, (2 * kv + pair + 1) * LANES)
            o_ref[:, lanes] = jnp.where(lo_sel, even, odd).astype(o_ref.dtype)


def _attn_bias():
    h = jnp.arange(1, N_Q_HEADS + 1, dtype=F32)
    slopes = jnp.exp2(-8.0 * h / N_Q_HEADS)
    qi = jnp.arange(BLOCK)[:, None]
    kj = jnp.arange(2 * BLOCK)[None, :]
    dist = qi - kj + BLOCK
    band = (dist >= 0) & (dist < WINDOW)
    bias = -slopes[:, None, None] * dist.astype(F32)
    later = jnp.where(band[None], bias, -jnp.inf)
    first = jnp.where((band & (kj >= BLOCK))[None], bias, -jnp.inf)
    both = jnp.stack([first, later])
    return both.reshape(2, N_KV_HEADS, Q_PER_KV * BLOCK, 2 * BLOCK)


def _attention(q3, kd3, vd3, sinks):
    n_b, seq, _ = q3.shape
    n_blk = seq // BLOCK
    kv_dup = N_KV_HEADS * LANES
    prev = lambda b, i: (b, jnp.maximum(i - 1, 0), 0)
    cur = lambda b, i: (b, i, 0)
    lane = jnp.arange(LANES)
    half = jnp.stack([lane < HEAD_DIM, lane >= HEAD_DIM]).astype(BF16)
    return pl.pallas_call(
        _attn_body,
        name="swa_attention",
        grid=(n_b, n_blk),
        in_specs=[
            pl.BlockSpec(memory_space=pltpu.SMEM),
            pl.BlockSpec((None, BLOCK, ATTN_WIDTH), cur),
            pl.BlockSpec((None, BLOCK, kv_dup), prev),
            pl.BlockSpec((None, BLOCK, kv_dup), cur),
            pl.BlockSpec((None, BLOCK, kv_dup), prev),
            pl.BlockSpec((None, BLOCK, kv_dup), cur),
            pl.BlockSpec((None, N_KV_HEADS, Q_PER_KV * BLOCK, 2 * BLOCK),
                         lambda b, i: (jnp.minimum(i, 1), 0, 0, 0)),
            pl.BlockSpec((2, LANES), lambda b, i: (0, 0)),
        ],
        out_specs=pl.BlockSpec((None, BLOCK, ATTN_WIDTH), cur),
        out_shape=jax.ShapeDtypeStruct((n_b, seq, ATTN_WIDTH), BF16),
        compiler_params=_cparams(("parallel", "arbitrary")),
    )(sinks, q3, kd3, kd3, vd3, vd3, _attn_bias(), half)


ROW_TILE = D_MODEL // LANES
assert ROW_TILE == SUBLANES


def _store_row_tiles(ref, x):
    for i in range(x.shape[0] // SUBLANES):
        for j in range(ROW_TILE):
            ref[pl.ds(i * SUBLANES * ROW_TILE + j, SUBLANES, stride=ROW_TILE), :] = (
                x[i * SUBLANES:(i + 1) * SUBLANES, j * LANES:(j + 1) * LANES])


def _load_row_tiles(ref, n):
    cols = []
    for j in range(ROW_TILE):
        cols.append(jnp.concatenate(
            [ref[pl.ds(i * SUBLANES * ROW_TILE + j, SUBLANES, stride=ROW_TILE), :]
             for i in range(n // SUBLANES)], axis=0))
    return jnp.concatenate(cols, axis=1)


def _mix_body(y_ref, a_ref, gs_ref, ga_ref, x_ref, wglu_ref, wsp_ref, wap_ref, wo_ref,
              g2_ref, wr_ref, br_ref, x1_ref, lg_ref):
    z = jnp.dot(y_ref[...], wglu_ref[...], preferred_element_type=F32)
    glu = z[:, :SSM_WIDTH] * jax.nn.sigmoid(z[:, SSM_WIDTH:])
    y_ssm = jnp.dot(glu.astype(BF16), wsp_ref[...], preferred_element_type=F32)
    y_attn = jnp.dot(a_ref[...], wap_ref[...], preferred_element_type=F32)
    mixed = jax.nn.sigmoid(gs_ref[...]) * y_ssm + jax.nn.sigmoid(ga_ref[...]) * y_attn
    x1 = x_ref[...] + jnp.dot(mixed.astype(BF16), wo_ref[...], preferred_element_type=F32)
    _store_row_tiles(x1_ref, x1)
    h2 = _rms(x1, g2_ref[...]).astype(BF16)
    lg_ref[...] = lax.dot_general(wr_ref[...], h2, (((1,), (1,)), ((), ())),
                                  preferred_element_type=F32) + br_ref[...]


def _mix(y2, a2, gs, ga, x2, w_glu, w_sp, w_ap, w_o, norm2_g, w_r_t, b_r):
    t = x2.shape[0]
    tm = min(TM_MIX, t)
    b_r_wide = jnp.broadcast_to(b_r.reshape(N_EXPERTS, 1), (N_EXPERTS, tm))
    tok = lambda w: pl.BlockSpec((tm, w), lambda i: (i, 0))
    full = lambda a: pl.BlockSpec(a.shape, lambda i: (0, 0))
    return pl.pallas_call(
        _mix_body,
        name="mix",
        grid=(t // tm,),
        in_specs=[tok(SSM_WIDTH), tok(ATTN_WIDTH), tok(D_MODEL), tok(D_MODEL), tok(D_MODEL),
                  full(w_glu), full(w_sp), full(w_ap), full(w_o), full(norm2_g),
                  full(w_r_t), full(b_r_wide)],
        out_specs=[pl.BlockSpec((tm * ROW_TILE, LANES), lambda i: (i, 0)),
                   pl.BlockSpec((N_EXPERTS, tm), lambda i: (0, i))],
        out_shape=[jax.ShapeDtypeStruct((t * ROW_TILE, LANES), F32),
                   jax.ShapeDtypeStruct((N_EXPERTS, t), F32)],
        compiler_params=_cparams(("parallel",)),
    )(y2, a2, gs, ga, x2, w_glu, w_sp, w_ap, w_o, norm2_g, w_r_t, b_r_wide)


ROUTE_ROWS = 4 * TOP_K


def _route_body(lg_ref, tri_ref, out_ref, cnt_ref, start_ref, base):
    @pl.when(pl.program_id(0) == 0)
    def _():
        base[...] = jnp.zeros_like(base)

    start_ref[...] = base[:, :LANES]

    tt = lg_ref.shape[1]
    row = lax.broadcasted_iota(jnp.int32, (N_EXPERTS, tt), 0)
    l = lg_ref[...]
    tops, hots = [], []
    for _ in range(TOP_K):
        m = jnp.max(l, axis=0, keepdims=True)
        idx = jnp.min(jnp.where(l == m, row, N_EXPERTS), axis=0, keepdims=True)
        hot = row == idx
        tops.append((m, idx))
        hots.append(hot)
        l = jnp.where(hot, -jnp.inf, l)
    es = [jnp.exp(m - tops[0][0]) for m, _ in tops]
    den = es[0] + es[1] + es[2] + es[3]
    member = jnp.zeros((N_EXPERTS, tt), F32)
    for hot in hots:
        member = member + jnp.where(hot, 1.0, 0.0)
    before = jnp.dot(member.astype(BF16), tri_ref[...], preferred_element_type=F32) + base[:, :tt]
    rows = [idx.astype(F32) for _, idx in tops]
    rows += [e / den for e in es]
    rows += [jnp.sum(jnp.where(hot, before, 0.0), axis=0, keepdims=True) for hot in hots]
    rows += [jnp.zeros((1, tt), F32)] * (ROUTE_ROWS - len(rows))
    out_ref[...] = jnp.concatenate(rows, axis=0)
    base[...] = base[...] + jnp.sum(member, axis=1, keepdims=True)
    cnt_ref[...] = base[:, :LANES]


def _route(logits_t):
    t = logits_t.shape[1]
    tt = min(TT_ROUTE, t)
    tri = jnp.triu(jnp.ones((tt, tt), F32), 1).astype(BF16)
    return pl.pallas_call(
        _route_body,
        name="route",
        grid=(t // tt,),
        in_specs=[pl.BlockSpec((N_EXPERTS, tt), lambda i: (0, i)),
                  pl.BlockSpec((tt, tt), lambda i: (0, 0))],
        out_specs=[pl.BlockSpec((ROUTE_ROWS, tt), lambda i: (0, i)),
                   pl.BlockSpec((N_EXPERTS, LANES), lambda i: (0, 0)),
                   pl.BlockSpec((None, N_EXPERTS, LANES), lambda i: (i, 0, 0))],
        out_shape=[jax.ShapeDtypeStruct((ROUTE_ROWS, t), F32),
                   jax.ShapeDtypeStruct((N_EXPERTS, LANES), F32),
                   jax.ShapeDtypeStruct((t // tt, N_EXPERTS, LANES), F32)],
        scratch_shapes=[pltpu.VMEM((N_EXPERTS, tt), F32)],
        compiler_params=_cparams(("arbitrary",)),
    )(logits_t, tri)


def _row_copies(n_tok, issue_one):
    def body(t8, carry):
        for i in range(SUBLANES):
            issue_one(t8 * SUBLANES + i)
        return carry
    lax.fori_loop(0, n_tok // SUBLANES, body, 0)


SEG_BITS = 10
SEG_CNT, SEG_SRC, SEG_DST = 0, N_EXPERTS, 2 * N_EXPERTS


def _runs(seg_ref, staged, rows, sem, act):
    for e in range(N_EXPERTS):
        n = seg_ref[0, 0, SEG_CNT + e]
        src = seg_ref[0, 0, SEG_SRC + e]
        dst = seg_ref[0, 0, SEG_DST + e]
        for b in reversed(range(SEG_BITS)):
            size = (1 << b) * ROW_TILE
            done = (n >> (b + 1)) << (b + 1)

            @pl.when((n & (1 << b)) != 0)
            def _():
                act(staged.at[pl.ds(pl.multiple_of((src + done) * ROW_TILE, ROW_TILE), size)],
                    rows.at[pl.ds(pl.multiple_of((dst + done) * ROW_TILE, ROW_TILE), size)], sem)


def _dispatch_body(pe_ref, nu_ref, q_ref, seg_ref, x_ref, xs_ref, zbuf, stage, sems, zsem):
    tt = x_ref.shape[0] // ROW_TILE
    blk_rows = MOE_BLOCK * ROW_TILE
    n_blocks = xs_ref.shape[0] // blk_rows

    @pl.when(pl.program_id(0) == 0)
    def _():
        zbuf[...] = jnp.zeros_like(zbuf)

        def zero_block(row):
            return pltpu.make_async_copy(
                zbuf, xs_ref.at[pl.ds(pl.multiple_of(row * ROW_TILE, blk_rows), blk_rows)], zsem)

        def each(act):
            for e in range(N_EXPERTS):
                @pl.when(pe_ref[e] >= MOE_BLOCK)
                def _():
                    act(zero_block(pe_ref[e] - MOE_BLOCK))

            def tail(i, carry):
                act(zero_block(i * MOE_BLOCK))
                return carry
            lax.fori_loop(nu_ref[0], n_blocks, tail, 0)

        each(lambda c: c.start())
        each(lambda c: c.wait())

    i = pl.program_id(0)

    def wait_tile(s):
        for _ in range(TOP_K):
            pltpu.make_async_copy(x_ref, xs_ref.at[pl.ds(0, tt * ROW_TILE)], sems.at[s]).wait()

    for slot in range(2):
        @pl.when(i % 2 == slot)
        def _():
            def place(t):
                row = x_ref[pl.ds(pl.multiple_of(t * ROW_TILE, ROW_TILE), ROW_TILE), :]
                for k in range(TOP_K):
                    q = pl.multiple_of(q_ref[0, 0, k * tt + t], ROW_TILE)
                    stage[slot, pl.ds(q, ROW_TILE), :] = row

            _row_copies(tt, place)

            _runs(seg_ref, stage.at[slot], xs_ref, sems.at[slot],
                  lambda src, dst, s: pltpu.make_async_copy(src, dst, s).start())

            @pl.when(i > 0)
            def _():
                wait_tile(1 - slot)

            @pl.when(i == pl.num_programs(0) - 1)
            def _():
                wait_tile(slot)


def _dispatch(pad_ends, n_used, x1r, q3, seg3, n_rows):
    t = x1r.shape[0] // ROW_TILE
    tt = q3.shape[2] // TOP_K
    smem = lambda a: pl.BlockSpec((1, 1, a.shape[2]), lambda i, pe, nu: (i, 0, 0), memory_space=pltpu.SMEM)
    grid_spec = pltpu.PrefetchScalarGridSpec(
        num_scalar_prefetch=2,
        grid=(t // tt,),
        in_specs=[smem(q3), smem(seg3),
                  pl.BlockSpec((tt * ROW_TILE, LANES), lambda i, pe, nu: (i, 0))],
        out_specs=pl.BlockSpec(memory_space=pl.ANY),
        scratch_shapes=[pltpu.VMEM((MOE_BLOCK * ROW_TILE, LANES), F32),
                        pltpu.VMEM((2, TOP_K * tt * ROW_TILE, LANES), F32),
                        pltpu.SemaphoreType.DMA((2,)), pltpu.SemaphoreType.DMA(())],
    )
    return pl.pallas_call(
        _dispatch_body,
        name="dispatch",
        grid_spec=grid_spec,
        out_shape=jax.ShapeDtypeStruct((n_rows * ROW_TILE, LANES), F32),
        compiler_params=_cparams(("arbitrary",)),
    )(pad_ends, n_used, q3, seg3, x1r)


def _ffn_body(b0_ref, nb_ref, nu_ref, g_ref, wgu_ref, bgu_ref, wd_ref, bd_ref, xs_ref, ys_ref,
              xbuf, ybuf, in_sems, out_sems, wgu_bf, wd_bf):
    e = pl.program_id(0)
    blk_rows = MOE_BLOCK * ROW_TILE
    n_blocks = xs_ref.shape[0] // blk_rows
    first = b0_ref[e]
    n_used = nu_ref[0]

    def rows(b):
        return pl.ds(pl.multiple_of(b * blk_rows, blk_rows), blk_rows)

    def in_copy(b):
        return pltpu.make_async_copy(xs_ref.at[rows(b)], xbuf.at[b % 2], in_sems.at[b % 2])

    def out_copy(b):
        return pltpu.make_async_copy(ybuf.at[b % 2], ys_ref.at[rows(b)], out_sems.at[b % 2])

    @pl.when(e == 0)
    def _():
        in_copy(0).start(priority=1)

    wgu_bf[...] = wgu_ref[...].astype(BF16)
    wd_bf[...] = wd_ref[...].astype(BF16)

    def block(j, carry):
        b = first + j
        in_copy(b).wait()

        @pl.when(b + 1 < n_used)
        def _():
            in_copy(b + 1).start(priority=1)

        h = _rms(_load_row_tiles(xbuf.at[b % 2], MOE_BLOCK), g_ref[...]).astype(BF16)
        gu = jnp.dot(h, wgu_bf[...], preferred_element_type=F32) + bgu_ref[...]
        gate = jnp.minimum(gu[:, :D_FF], SWIGLU_LIMIT)
        up = jnp.clip(gu[:, D_FF:], -SWIGLU_LIMIT, SWIGLU_LIMIT)
        act = (up + 1.0) * gate * jax.nn.sigmoid(SWIGLU_ALPHA * gate)
        y = jnp.dot(act.astype(BF16), wd_bf[...], preferred_element_type=F32) + bd_ref[...]

        @pl.when(b >= 2)
        def _():
            out_copy(b - 2).wait()

        _store_row_tiles(ybuf.at[b % 2], y)
        out_copy(b).start(priority=1)
        return carry

    lax.fori_loop(0, nb_ref[e], block, 0)

    @pl.when(e == pl.num_programs(0) - 1)
    def _():
        @pl.when(n_used >= 2)
        def _():
            out_copy(n_used - 2).wait()
        out_copy(n_used - 1).wait()

        ybuf[0] = jnp.zeros((blk_rows, LANES), F32)

        def tail(act):
            def body(i, carry):
                act(pltpu.make_async_copy(
                    ybuf.at[0], ys_ref.at[pl.ds(pl.multiple_of(i * blk_rows, blk_rows), blk_rows)],
                    out_sems.at[0]))
                return carry
            lax.fori_loop(nu_ref[0], n_blocks, body, 0)

        tail(lambda c: c.start())
        tail(lambda c: c.wait())


def _moe_ffn(blk_first, blk_count, n_used, xs, norm2_g, w_gu, b_gu, w_d, b_d):
    blk_rows = MOE_BLOCK * ROW_TILE
    by_e = lambda e, b0, nb, nu: (e, 0, 0)
    grid_spec = pltpu.PrefetchScalarGridSpec(
        num_scalar_prefetch=3,
        grid=(N_EXPERTS,),
        in_specs=[
            pl.BlockSpec((1, D_MODEL), lambda e, b0, nb, nu: (0, 0)),
            pl.BlockSpec((None, D_MODEL, 2 * D_FF), by_e),
            pl.BlockSpec((None, 1, 2 * D_FF), by_e),
            pl.BlockSpec((None, D_FF, D_MODEL), by_e),
            pl.BlockSpec((None, 1, D_MODEL), by_e),
            pl.BlockSpec(memory_space=pl.ANY),
        ],
        out_specs=pl.BlockSpec(memory_space=pl.ANY),
        scratch_shapes=[pltpu.VMEM((2, blk_rows, LANES), F32), pltpu.VMEM((2, blk_rows, LANES), F32),
                        pltpu.SemaphoreType.DMA((2,)), pltpu.SemaphoreType.DMA((2,)),
                        pltpu.VMEM((D_MODEL, 2 * D_FF), BF16), pltpu.VMEM((D_FF, D_MODEL), BF16)],
    )
    return pl.pallas_call(
        _ffn_body,
        name="moe_ffn",
        grid_spec=grid_spec,
        out_shape=jax.ShapeDtypeStruct(xs.shape, F32),
        compiler_params=_cparams(("arbitrary",)),
    )(blk_first, blk_count, n_used, norm2_g, w_gu, b_gu, w_d, b_d, xs)


def _combine_body(seg_ref, q_ref, gate_ref, x_ref, ys_ref, o_ref, stage, acc_rt, sems):
    i = pl.program_id(0)
    n_tiles = pl.num_programs(0) - 1
    tt = o_ref.shape[0]

    def fetch(slot):
        _runs(seg_ref, stage.at[slot], ys_ref, sems.at[slot],
              lambda staged, rows, s: pltpu.make_async_copy(rows, staged, s).start())

    def finish(slot):
        for _ in range(TOP_K):
            pltpu.make_async_copy(ys_ref.at[pl.ds(0, tt * ROW_TILE)], acc_rt, sems.at[slot]).wait()

        def token(t):
            row = pl.ds(pl.multiple_of(t * ROW_TILE, ROW_TILE), ROW_TILE)
            acc = x_ref[row, :]
            for k in range(TOP_K):
                q = pl.multiple_of(q_ref[0, 0, k * tt + t], ROW_TILE)
                acc = acc + gate_ref[0, 0, k * tt + t] * stage[slot, pl.ds(q, ROW_TILE), :]
            acc_rt[row, :] = acc

        _row_copies(tt, token)
        o_ref[...] = _load_row_tiles(acc_rt, tt)

    for slot in range(2):
        @pl.when((i < n_tiles) & (i % 2 == slot))
        def _():
            fetch(slot)

        @pl.when((i > 0) & ((i - 1) % 2 == slot))
        def _():
            finish(slot)


def _combine(x1r, gates3, q3, seg3, ys):
    t = x1r.shape[0] // ROW_TILE
    tt = q3.shape[2] // TOP_K
    n_tiles = t // tt
    ahead = lambda i: (jnp.minimum(i, n_tiles - 1), 0, 0)
    behind3 = lambda i: (jnp.maximum(i - 1, 0), 0, 0)
    behind = lambda i: (jnp.maximum(i - 1, 0), 0)
    smem = lambda a, idx: pl.BlockSpec((1, 1, a.shape[2]), idx, memory_space=pltpu.SMEM)
    return pl.pallas_call(
        _combine_body,
        name="combine",
        grid=(n_tiles + 1,),
        in_specs=[smem(seg3, ahead), smem(q3, behind3), smem(gates3, behind3),
                  pl.BlockSpec((tt * ROW_TILE, LANES), behind),
                  pl.BlockSpec(memory_space=pl.ANY)],
        out_specs=pl.BlockSpec((tt, D_MODEL), behind),
        out_shape=jax.ShapeDtypeStruct((t, D_MODEL), F32),
        scratch_shapes=[pltpu.VMEM((2, TOP_K * tt * ROW_TILE, LANES), F32),
                        pltpu.VMEM((tt * ROW_TILE, LANES), F32),
                        pltpu.SemaphoreType.DMA((2,))],
        compiler_params=_cparams(("arbitrary",)),
    )(seg3, q3, gates3, x1r, ys)


def _moe(x1, logits, norm2_g, w_gate_up, b_gate_up, w_down, b_down):
    t = x1.shape[0] // ROW_TILE
    n_assign = t * TOP_K
    n_blocks = -(-n_assign // MOE_BLOCK) + N_EXPERTS
    n_rows = n_blocks * MOE_BLOCK

    routed, cnt, tile_start = _route(logits)
    ids = routed[:TOP_K].astype(jnp.int32)
    pos = routed[2 * TOP_K:3 * TOP_K].astype(jnp.int32)
    gates = routed[TOP_K:2 * TOP_K]
    counts = cnt[:, 0].astype(jnp.int32)
    padded = (counts + MOE_BLOCK - 1) // MOE_BLOCK * MOE_BLOCK
    pad_ends = jnp.cumsum(padded)
    pad_starts = pad_ends - padded
    n_used = (pad_ends[-1] // MOE_BLOCK).astype(jnp.int32).reshape(1)

    tt = min(TT_MOVE, t)
    n_tiles = t // tt
    by_tile = lambda a: a.reshape(TOP_K, n_tiles, tt).transpose(1, 0, 2).reshape(n_tiles, 1, TOP_K * tt)

    tile_start = tile_start[:, :, 0].astype(jnp.int32)
    tile_cnt = jnp.concatenate([tile_start[1:], counts[None]], axis=0) - tile_start
    tile_off = jnp.cumsum(tile_cnt, axis=1) - tile_cnt
    seg_dst = pad_starts[None, :] + tile_start
    seg3 = jnp.concatenate([tile_cnt, tile_off, seg_dst, jnp.zeros_like(tile_cnt)], axis=1)
    seg3 = seg3.reshape(n_tiles, 1, 4 * N_EXPERTS)
    experts = jnp.arange(N_EXPERTS, dtype=jnp.int32)[:, None, None]
    shift = jnp.repeat((tile_off - tile_start).T, tt, axis=1)
    q = jnp.sum(jnp.where(ids[None] == experts, shift[:, None, :], 0), axis=0) + pos
    q3 = by_tile(q * ROW_TILE)
    xs = _dispatch(pad_ends.astype(jnp.int32), n_used, x1, q3, seg3, n_rows)
    ys = _moe_ffn((pad_starts // MOE_BLOCK).astype(jnp.int32), (padded // MOE_BLOCK).astype(jnp.int32),
                  n_used, xs, norm2_g,
                  w_gate_up, b_gate_up.reshape(N_EXPERTS, 1, 2 * D_FF),
                  w_down, b_down.reshape(N_EXPERTS, 1, D_MODEL))
    return _combine(x1, by_tile(gates), q3, seg3, ys)


def _layer(x, norm1_g, w_in, ssm_a_re, ssm_a_im, ssm_log_dt, ssm_b_re, ssm_b_im, ssm_c_re,
           ssm_c_im, ssm_d, w_glu, w_ssm_proj, q_norm_g, k_norm_g, attn_sinks, w_attn_proj,
           w_out, norm2_g, w_router, b_router, w_gate_up, b_gate_up, w_down, b_down):
    n_b, seq, d = x.shape
    t = n_b * seq
    x2 = x.reshape(t, d)
    u, q, kd, vd, gs, ga = _in_proj(x2, norm1_g.reshape(1, d), w_in, q_norm_g, k_norm_g)

    abar_re, abar_im, bbt_re, bbt_im = _s5_prep(ssm_a_re, ssm_a_im, ssm_log_dt, ssm_b_re, ssm_b_im)
    b_mat, c_mat, ab = _s5_matrices(abar_re, abar_im, bbt_re, bbt_im, ssm_c_re, ssm_c_im)
    y = _s5(u.reshape(n_b, seq, SSM_WIDTH), b_mat, c_mat, ab, ssm_d.reshape(1, SSM_WIDTH))

    a = _attention(q.reshape(n_b, seq, ATTN_WIDTH), kd.reshape(n_b, seq, KV_DUP),
                   vd.reshape(n_b, seq, KV_DUP), attn_sinks)

    norm2 = norm2_g.reshape(1, d)
    x1, logits = _mix(y.reshape(t, SSM_WIDTH), a.reshape(t, ATTN_WIDTH), gs, ga, x2,
                      w_glu.astype(BF16), w_ssm_proj.astype(BF16), w_attn_proj.astype(BF16),
                      w_out.astype(BF16), norm2, w_router.T.astype(BF16), b_router)
    out = _moe(x1, logits, norm2, w_gate_up, b_gate_up, w_down, b_down)
    return out.reshape(n_b, seq, d)


def kernel(x, norm1_g, w_in, ssm_a_re, ssm_a_im, ssm_log_dt, ssm_b_re, ssm_b_im, ssm_c_re, ssm_c_im, ssm_d, w_glu, w_ssm_proj, q_norm_g, k_norm_g, attn_sinks, w_attn_proj, w_out, norm2_g, w_router, b_router, w_gate_up, b_gate_up, w_down, b_down):
    for layer in range(norm1_g.shape[0]):
        x = _layer(
            x, norm1_g[layer], w_in[layer], ssm_a_re[layer], ssm_a_im[layer], ssm_log_dt[layer],
            ssm_b_re[layer], ssm_b_im[layer], ssm_c_re[layer], ssm_c_im[layer], ssm_d[layer],
            w_glu[layer], w_ssm_proj[layer], q_norm_g[layer], k_norm_g[layer], attn_sinks[layer],
            w_attn_proj[layer], w_out[layer], norm2_g[layer], w_router[layer], b_router[layer],
            w_gate_up[layer], b_gate_up[layer], w_down[layer], b_down[layer])
    return x
```

```python
import functools

import jax
import jax.numpy as jnp
from jax import lax
from jax.experimental import pallas as pl
from jax.experimental.pallas import tpu as pltpu

D_MODEL = 1024
NORM_EPS = 1e-5
SSM_WIDTH = 1024
SSM_GROUP_CH = 16
SSM_GROUPS = SSM_WIDTH // SSM_GROUP_CH
SSM_STATE = 64
HEAD_DIM = 64
N_Q_HEADS = 16
N_KV_HEADS = 4
Q_PER_KV = N_Q_HEADS // N_KV_HEADS
ATTN_WIDTH = N_Q_HEADS * HEAD_DIM
KV_WIDTH = N_KV_HEADS * HEAD_DIM
WINDOW = 128
BLOCK = 128
N_EXPERTS = 32
TOP_K = 4
D_FF = 1024
SWIGLU_LIMIT = 7.0
SWIGLU_ALPHA = 1.702
MOE_BLOCK = 512

LANES = 128
SUBLANES = 8
VMEM_LIMIT = 56 * 1024 * 1024

S5_GROUP_BLOCK = 8
S5_CH = S5_GROUP_BLOCK * SSM_GROUP_CH
S5_ST = S5_GROUP_BLOCK * SSM_STATE
S5_CHUNK = 512
S5_SEG = 64
assert S5_CHUNK // S5_SEG >= 4 and (S5_CHUNK // S5_SEG) % 2 == 0
S5_PITCH = S5_CHUNK + SUBLANES

TM_IN = 512
ATTN_QB = 8
TM_MIX = 512
TT_ROUTE = 512
TT_MOVE = 512
assert TT_MOVE == TT_ROUTE

F32 = jnp.float32
BF16 = jnp.bfloat16


def _cparams(sem):
    return pltpu.CompilerParams(dimension_semantics=sem, vmem_limit_bytes=VMEM_LIMIT)


def _rms(x, g):
    return x * lax.rsqrt(jnp.mean(x * x, axis=-1, keepdims=True) + NORM_EPS) * g


KV_DUP = N_KV_HEADS * LANES


def _in_proj_body(x_ref, g_ref, w_ref, qg_ref, kg_ref, u_ref, q_ref, kd_ref, vd_ref, gs_ref, ga_ref):
    h = _rms(x_ref[...], g_ref[...]).astype(BF16)
    off = [0]

    def proj(width):
        lo = off[0]
        off[0] = lo + width
        return jnp.dot(h, w_ref[:, lo:lo + width], preferred_element_type=F32)

    u_ref[...] = proj(SSM_WIDTH)

    q = proj(ATTN_WIDTH)
    lo_sel = lax.broadcasted_iota(jnp.int32, (q.shape[0], LANES), 1) < HEAD_DIM
    for j in range(ATTN_WIDTH // LANES):
        qq = q[:, j * LANES:(j + 1) * LANES]
        sq = qq * qq
        ms_lo = jnp.sum(jnp.where(lo_sel, sq, 0.0), axis=-1, keepdims=True) / HEAD_DIM
        ms_hi = jnp.sum(jnp.where(lo_sel, 0.0, sq), axis=-1, keepdims=True) / HEAD_DIM
        r = jnp.where(lo_sel, lax.rsqrt(ms_lo + NORM_EPS), lax.rsqrt(ms_hi + NORM_EPS))
        q_ref[:, j * LANES:(j + 1) * LANES] = (qq * r * qg_ref[...]).astype(q_ref.dtype)

    kd = proj(KV_DUP)
    for kv in range(N_KV_HEADS):
        grp = slice(kv * LANES, (kv + 1) * LANES)
        kd_ref[:, grp] = _rms(kd[:, grp], kg_ref[...]).astype(kd_ref.dtype)
    vd_ref[...] = proj(KV_DUP).astype(vd_ref.dtype)
    gs_ref[...] = proj(D_MODEL)
    ga_ref[...] = proj(D_MODEL)


def _in_proj_weight(w_in):
    d = w_in.shape[0]
    cuts = (SSM_WIDTH, SSM_WIDTH + ATTN_WIDTH, SSM_WIDTH + ATTN_WIDTH + KV_WIDTH,
            SSM_WIDTH + ATTN_WIDTH + 2 * KV_WIDTH)
    w_uq, w_k, w_v, w_g = (w_in[:, :cuts[1]], w_in[:, cuts[1]:cuts[2]], w_in[:, cuts[2]:cuts[3]],
                           w_in[:, cuts[3]:])

    def twice(w):
        w = w.reshape(d, N_KV_HEADS, 1, HEAD_DIM)
        return jnp.concatenate([w, w], axis=2).reshape(d, KV_DUP)

    return jnp.concatenate([w_uq, twice(w_k), twice(w_v), w_g], axis=1).astype(BF16)


def _in_proj(x2, norm_g, w_in, q_norm_g, k_norm_g):
    t = x2.shape[0]
    w = _in_proj_weight(w_in)
    qg = (jnp.tile(q_norm_g, LANES // HEAD_DIM) * HEAD_DIM ** -0.5).reshape(1, LANES)
    kg = jnp.tile(k_norm_g, LANES // HEAD_DIM).reshape(1, LANES)
    outs = ((SSM_WIDTH, F32), (ATTN_WIDTH, BF16), (KV_DUP, BF16), (KV_DUP, BF16),
            (D_MODEL, F32), (D_MODEL, F32))
    const = lambda a: pl.BlockSpec(a.shape, lambda i: (0, 0))
    return pl.pallas_call(
        _in_proj_body,
        name="in_proj",
        grid=(t // TM_IN,),
        in_specs=[pl.BlockSpec((TM_IN, D_MODEL), lambda i: (i, 0)), const(norm_g), const(w),
                  const(qg), const(kg)],
        out_specs=[pl.BlockSpec((TM_IN, n), lambda i: (i, 0)) for n, _ in outs],
        out_shape=[jax.ShapeDtypeStruct((t, n), dt) for n, dt in outs],
        compiler_params=_cparams(("parallel",)),
    )(x2, norm_g, w, qg, kg)


def _s5_prep_body(are_ref, aim_ref, ldt_ref, bre_ref, bim_ref,
                  abr_ref, abi_ref, bbr_ref, bbi_ref):
    a_re = are_ref[...]
    a_im = aim_ref[...]
    dt = jnp.exp(ldt_ref[...])
    mag = jnp.exp(a_re * dt)
    abar_re = mag * jnp.cos(a_im * dt)
    abar_im = mag * jnp.sin(a_im * dt)
    den = a_re * a_re + a_im * a_im
    q_re = ((abar_re - 1.0) * a_re + abar_im * a_im) / den
    q_im = (abar_im * a_re - (abar_re - 1.0) * a_im) / den
    abr_ref[...] = abar_re
    abi_ref[...] = abar_im
    for h in range(SSM_GROUP_CH):
        bbr_ref[h] = q_re * bre_ref[h] - q_im * bim_ref[h]
        bbi_ref[h] = q_re * bim_ref[h] + q_im * bre_ref[h]


def _s5_prep(a_re, a_im, log_dt, b_re, b_im):
    g, p, hc = SSM_GROUPS, SSM_STATE, SSM_GROUP_CH
    bt_re = jnp.transpose(b_re, (2, 0, 1))
    bt_im = jnp.transpose(b_im, (2, 0, 1))
    return pl.pallas_call(
        _s5_prep_body,
        name="s5_prep",
        out_shape=[
            jax.ShapeDtypeStruct((g, p), F32),
            jax.ShapeDtypeStruct((g, p), F32),
            jax.ShapeDtypeStruct((hc, g, p), F32),
            jax.ShapeDtypeStruct((hc, g, p), F32),
        ],
    )(a_re, a_im, log_dt.reshape(g, 1), bt_re, bt_im)


def _s5_matrices(abar_re, abar_im, bbt_re, bbt_im, c_re, c_im):
    nb, gb, hc, p = SSM_GROUPS // S5_GROUP_BLOCK, S5_GROUP_BLOCK, SSM_GROUP_CH, SSM_STATE
    eye = jnp.eye(gb, dtype=F32)

    def b_block(bbt):
        t = jnp.transpose(bbt, (1, 0, 2)).reshape(nb, gb, hc, p)
        return jnp.einsum('jghp,gk->jghkp', t, eye).reshape(nb, gb * hc, gb * p)

    def c_block(c):
        t = c.reshape(nb, gb, hc, p)
        return jnp.einsum('jghp,gk->jgpkh', t, eye).reshape(nb, gb * p, gb * hc)

    b_mat = jnp.concatenate([b_block(bbt_re), b_block(bbt_im)], axis=2).astype(BF16)
    c_mat = jnp.concatenate([c_block(c_re), c_block(-c_im)], axis=1).astype(BF16)
    ab = jnp.concatenate([abar_re.reshape(nb, 1, gb * p), abar_im.reshape(nb, 1, gb * p)], axis=2)
    ab = jnp.broadcast_to(ab, (nb, SUBLANES, 2 * gb * p))
    return b_mat, c_mat, ab


def _s5_body(u_ref, b_ref, c_ref, ab_ref, d_ref, o_ref, ubuf, utm, bu0, bu1, st0, st1, state, ytm):
    n_b = u_ref.shape[0]
    chunk = u_ref.shape[1]

    @pl.when(pl.program_id(1) == 0)
    def _():
        state[...] = jnp.zeros_like(state)

    for b in range(n_b):
        ubuf[b * S5_PITCH:b * S5_PITCH + chunk, :] = u_ref[b]

    def to_time_major(l8, carry):
        for i in range(SUBLANES):
            l = l8 * SUBLANES + i
            row = pl.multiple_of(l * n_b, SUBLANES)
            utm[pl.ds(row, n_b), :] = ubuf[pl.ds(l, n_b, stride=S5_PITCH), :]
        return carry

    lax.fori_loop(0, chunk // SUBLANES, to_time_major, 0)

    seg_rows = S5_SEG * n_b
    n_seg = chunk // S5_SEG
    a_re = ab_ref[:, :S5_ST]
    a_im = ab_ref[:, S5_ST:]

    bu = (bu0, bu1)
    st = (st0, st1)

    def seg(s):
        return pl.ds(pl.multiple_of(s * seg_rows, seg_rows), seg_rows)

    def b_proj(s, par):
        bu[par][...] = jnp.dot(utm[seg(s), :].astype(BF16), b_ref[...], preferred_element_type=F32)

    def c_proj(s, par):
        y = jnp.dot(st[par][...], c_ref[...], preferred_element_type=F32)
        ytm[seg(s), :] = jax.nn.gelu(y + d_ref[...] * utm[seg(s), :])

    def recurrence(par, carry):
        s_re, s_im = carry
        for i in range(0, S5_SEG, 2):
            both = []
            for ii in (i, i + 1):
                rows = slice(ii * n_b, (ii + 1) * n_b)
                n_re = a_re * s_re - a_im * s_im + bu[par][rows, :S5_ST]
                n_im = a_re * s_im + a_im * s_re + bu[par][rows, S5_ST:]
                both.append(jnp.concatenate([n_re, n_im], axis=1))
                s_re, s_im = n_re, n_im
            st[par][i * n_b:(i + 2) * n_b, :] = jnp.concatenate(both, axis=0).astype(BF16)
        return s_re, s_im

    def stage(s, par, carry):
        b_proj(s + 1, 1 - par)
        c_proj(s - 1, 1 - par)
        return recurrence(par, carry)

    def stage_pair(p, carry):
        s = 2 * p + 1
        carry = stage(s, 1, carry)
        return stage(s + 1, 0, carry)

    b_proj(0, 0)
    b_proj(1, 1)
    carry = recurrence(0, (state[:, :S5_ST], state[:, S5_ST:]))
    carry = lax.fori_loop(0, (n_seg - 2) // 2, stage_pair, carry)
    c_proj(n_seg - 2, 0)
    s_re, s_im = recurrence(1, carry)
    c_proj(n_seg - 1, 1)
    state[:, :S5_ST] = s_re
    state[:, S5_ST:] = s_im

    for b in range(n_b):
        o_ref[b] = ytm[pl.ds(b, chunk, stride=n_b), :].astype(o_ref.dtype)


def _s5(u3, b_mat, c_mat, ab, d_flat):
    n_b, seq, _ = u3.shape
    assert n_b == SUBLANES, "the scan keeps one sequence per sublane"
    chunk = min(S5_CHUNK, seq)
    nb = SSM_GROUPS // S5_GROUP_BLOCK
    return pl.pallas_call(
        _s5_body,
        name="s5_scan",
        grid=(nb, seq // chunk),
        in_specs=[
            pl.BlockSpec((n_b, chunk, S5_CH), lambda j, c: (0, c, j)),
            pl.BlockSpec((None, S5_CH, 2 * S5_ST), lambda j, c: (j, 0, 0)),
            pl.BlockSpec((None, 2 * S5_ST, S5_CH), lambda j, c: (j, 0, 0)),
            pl.BlockSpec((None, SUBLANES, 2 * S5_ST), lambda j, c: (j, 0, 0)),
            pl.BlockSpec((1, S5_CH), lambda j, c: (0, j)),
        ],
        out_specs=pl.BlockSpec((n_b, chunk, S5_CH), lambda j, c: (0, c, j)),
        out_shape=jax.ShapeDtypeStruct((n_b, seq, SSM_WIDTH), BF16),
        scratch_shapes=[
            pltpu.VMEM((n_b * S5_PITCH, S5_CH), F32),
            pltpu.VMEM((chunk * n_b, S5_CH), F32),
            pltpu.VMEM((S5_SEG * n_b, 2 * S5_ST), F32),
            pltpu.VMEM((S5_SEG * n_b, 2 * S5_ST), F32),
            pltpu.VMEM((S5_SEG * n_b, 2 * S5_ST), BF16),
            pltpu.VMEM((S5_SEG * n_b, 2 * S5_ST), BF16),
            pltpu.VMEM((n_b, 2 * S5_ST), F32),
            pltpu.VMEM((chunk * n_b, S5_CH), F32),
        ],
        compiler_params=_cparams(("parallel", "arbitrary")),
    )(u3, b_mat, c_mat, ab, d_flat)


def _attn_body(sinks_ref, q_ref, kp_ref, kc_ref, vp_ref, vc_ref, bias_ref, half_ref, o_ref):
    lo_mask = half_ref[0:1, :]
    hi_mask = half_ref[1:2, :]
    lo_sel = lax.broadcasted_iota(jnp.int32, (BLOCK, LANES), 1) < HEAD_DIM
    first_step = jnp.minimum(pl.program_id(1), 1)
    for kv in range(N_KV_HEADS):
        grp = slice(kv * LANES, (kv + 1) * LANES)
        k_all = jnp.concatenate([kp_ref[:, grp], kc_ref[:, grp]], axis=0)
        v_all = jnp.concatenate([vp_ref[:, grp], vc_ref[:, grp]], axis=0)
        for qb in range(ATTN_QB):
            rows = slice(qb * BLOCK, (qb + 1) * BLOCK)
            kw = k_all[qb * BLOCK:(qb + 2) * BLOCK]
            vw = v_all[qb * BLOCK:(qb + 2) * BLOCK]
            qs = []
            for pair in range(Q_PER_KV // 2):
                lanes = slice((2 * kv + pair) * LANES, (2 * kv + pair + 1) * LANES)
                qp = q_ref[rows, lanes]
                qs += [qp * lo_mask, qp * hi_mask]
            q_st = jnp.concatenate(qs, axis=0)
            s = lax.dot_general(q_st, kw, (((1,), (1,)), ((), ())), preferred_element_type=F32)
            s = s + (bias_ref[first_step, kv] if qb == 0 else bias_ref[1, kv])
            ps = []
            for g in range(Q_PER_KV):
                sg = s[g * BLOCK:(g + 1) * BLOCK]
                sink = sinks_ref[kv * Q_PER_KV + g]
                m = jnp.maximum(jnp.max(sg, axis=-1, keepdims=True), sink)
                p = jnp.exp(sg - m)
                den = jnp.sum(p, axis=-1, keepdims=True) + jnp.exp(sink - m)
                ps.append((p / den).astype(BF16))
            p_st = jnp.concatenate(ps, axis=0)
            o = jnp.dot(p_st, vw, preferred_element_type=F32)
            for pair in range(Q_PER_KV // 2):
                even = o[(2 * pair) * BLOCK:(2 * pair + 1) * BLOCK]
                odd = o[(2 * pair + 1) * BLOCK:(2 * pair + 2) * BLOCK]
                lanes = slice((2 * kv + pair) * LANES, (2 * kv + pair + 1) * LANES)
                o_ref[rows, lanes] = jnp.where(lo_sel, even, odd).astype(o_ref.dtype)


def _attn_bias():
    h = jnp.arange(1, N_Q_HEADS + 1, dtype=F32)
    slopes = jnp.exp2(-8.0 * h / N_Q_HEADS)
    qi = jnp.arange(BLOCK)[:, None]
    kj = jnp.arange(2 * BLOCK)[None, :]
    dist = qi - kj + BLOCK
    band = (dist >= 0) & (dist < WINDOW)
    bias = -slopes[:, None, None] * dist.astype(F32)
    later = jnp.where(band[None], bias, -jnp.inf)
    first = jnp.where((band & (kj >= BLOCK))[None], bias, -jnp.inf)
    both = jnp.stack([first, later])
    return both.reshape(2, N_KV_HEADS, Q_PER_KV * BLOCK, 2 * BLOCK)


def _attention(q3, kd3, vd3, sinks):
    n_b, seq, _ = q3.shape
    tq = ATTN_QB * BLOCK
    kv_dup = N_KV_HEADS * LANES
    prev = lambda b, i: (b, jnp.maximum(i * ATTN_QB - 1, 0), 0)
    cur = lambda b, i: (b, i, 0)
    lane = jnp.arange(LANES)
    half = jnp.stack([lane < HEAD_DIM, lane >= HEAD_DIM]).astype(BF16)
    bias = _attn_bias()
    return pl.pallas_call(
        _attn_body,
        name="swa_attention",
        grid=(n_b, seq // tq),
        in_specs=[
            pl.BlockSpec(memory_space=pltpu.SMEM),
            pl.BlockSpec((None, tq, ATTN_WIDTH), cur),
            pl.BlockSpec((None, BLOCK, kv_dup), prev),
            pl.BlockSpec((None, tq, kv_dup), cur),
            pl.BlockSpec((None, BLOCK, kv_dup), prev),
            pl.BlockSpec((None, tq, kv_dup), cur),
            pl.BlockSpec(bias.shape, lambda b, i: (0, 0, 0, 0)),
            pl.BlockSpec((2, LANES), lambda b, i: (0, 0)),
        ],
        out_specs=pl.BlockSpec((None, tq, ATTN_WIDTH), cur),
        out_shape=jax.ShapeDtypeStruct((n_b, seq, ATTN_WIDTH), BF16),
        compiler_params=_cparams(("parallel", "arbitrary")),
    )(sinks, q3, kd3, kd3, vd3, vd3, bias, half)


ROW_TILE = D_MODEL // LANES
assert ROW_TILE == SUBLANES


def _store_row_tiles(ref, x):
    for i in range(x.shape[0] // SUBLANES):
        for j in range(ROW_TILE):
            ref[pl.ds(i * SUBLANES * ROW_TILE + j, SUBLANES, stride=ROW_TILE), :] = (
                x[i * SUBLANES:(i + 1) * SUBLANES, j * LANES:(j + 1) * LANES])


def _load_row_tiles(ref, n):
    cols = []
    for j in range(ROW_TILE):
        cols.append(jnp.concatenate(
            [ref[pl.ds(i * SUBLANES * ROW_TILE + j, SUBLANES, stride=ROW_TILE), :]
             for i in range(n // SUBLANES)], axis=0))
    return jnp.concatenate(cols, axis=1)


def _mix_body(y_ref, a_ref, gs_ref, ga_ref, x_ref, wglu_ref, wsp_ref, wap_ref, wo_ref,
              g2_ref, wr_ref, br_ref, x1_ref, lg_ref):
    z = jnp.dot(y_ref[...], wglu_ref[...], preferred_element_type=F32)
    glu = z[:, :SSM_WIDTH] * jax.nn.sigmoid(z[:, SSM_WIDTH:])
    y_ssm = jnp.dot(glu.astype(BF16), wsp_ref[...], preferred_element_type=F32)
    y_attn = jnp.dot(a_ref[...], wap_ref[...], preferred_element_type=F32)
    mixed = jax.nn.sigmoid(gs_ref[...]) * y_ssm + jax.nn.sigmoid(ga_ref[...]) * y_attn
    x1 = x_ref[...] + jnp.dot(mixed.astype(BF16), wo_ref[...], preferred_element_type=F32)
    _store_row_tiles(x1_ref, x1)
    h2 = _rms(x1, g2_ref[...]).astype(BF16)
    lg_ref[...] = lax.dot_general(wr_ref[...], h2, (((1,), (1,)), ((), ())),
                                  preferred_element_type=F32) + br_ref[...]


def _mix(y2, a2, gs, ga, x2, w_glu, w_sp, w_ap, w_o, norm2_g, w_r_t, b_r):
    t = x2.shape[0]
    tm = min(TM_MIX, t)
    b_r_wide = jnp.broadcast_to(b_r.reshape(N_EXPERTS, 1), (N_EXPERTS, tm))
    tok = lambda w: pl.BlockSpec((tm, w), lambda i: (i, 0))
    full = lambda a: pl.BlockSpec(a.shape, lambda i: (0, 0))
    return pl.pallas_call(
        _mix_body,
        name="mix",
        grid=(t // tm,),
        in_specs=[tok(SSM_WIDTH), tok(ATTN_WIDTH), tok(D_MODEL), tok(D_MODEL), tok(D_MODEL),
                  full(w_glu), full(w_sp), full(w_ap), full(w_o), full(norm2_g),
                  full(w_r_t), full(b_r_wide)],
        out_specs=[pl.BlockSpec((tm * ROW_TILE, LANES), lambda i: (i, 0)),
                   pl.BlockSpec((N_EXPERTS, tm), lambda i: (0, i))],
        out_shape=[jax.ShapeDtypeStruct((t * ROW_TILE, LANES), F32),
                   jax.ShapeDtypeStruct((N_EXPERTS, t), F32)],
        compiler_params=_cparams(("parallel",)),
    )(y2, a2, gs, ga, x2, w_glu, w_sp, w_ap, w_o, norm2_g, w_r_t, b_r_wide)


ROUTE_ROWS = 4 * TOP_K


def _route_body(lg_ref, tri_ref, out_ref, cnt_ref, start_ref, base):
    @pl.when(pl.program_id(0) == 0)
    def _():
        base[...] = jnp.zeros_like(base)

    start_ref[...] = base[:, :LANES]

    tt = lg_ref.shape[1]
    row = lax.broadcasted_iota(jnp.int32, (N_EXPERTS, tt), 0)
    l = lg_ref[...]
    tops, hots = [], []
    for _ in range(TOP_K):
        m = jnp.max(l, axis=0, keepdims=True)
        idx = jnp.min(jnp.where(l == m, row, N_EXPERTS), axis=0, keepdims=True)
        hot = row == idx
        tops.append((m, idx))
        hots.append(hot)
        l = jnp.where(hot, -jnp.inf, l)
    es = [jnp.exp(m - tops[0][0]) for m, _ in tops]
    den = es[0] + es[1] + es[2] + es[3]
    member = jnp.zeros((N_EXPERTS, tt), F32)
    for hot in hots:
        member = member + jnp.where(hot, 1.0, 0.0)
    before = jnp.dot(member.astype(BF16), tri_ref[...], preferred_element_type=F32) + base[:, :tt]
    rows = [idx.astype(F32) for _, idx in tops]
    rows += [e / den for e in es]
    rows += [jnp.sum(jnp.where(hot, before, 0.0), axis=0, keepdims=True) for hot in hots]
    rows += [jnp.zeros((1, tt), F32)] * (ROUTE_ROWS - len(rows))
    out_ref[...] = jnp.concatenate(rows, axis=0)
    base[...] = base[...] + jnp.sum(member, axis=1, keepdims=True)
    cnt_ref[...] = base[:, :LANES]


def _route(logits_t):
    t = logits_t.shape[1]
    tt = min(TT_ROUTE, t)
    tri = jnp.triu(jnp.ones((tt, tt), F32), 1).astype(BF16)
    return pl.pallas_call(
        _route_body,
        name="route",
        grid=(t // tt,),
        in_specs=[pl.BlockSpec((N_EXPERTS, tt), lambda i: (0, i)),
                  pl.BlockSpec((tt, tt), lambda i: (0, 0))],
        out_specs=[pl.BlockSpec((ROUTE_ROWS, tt), lambda i: (0, i)),
                   pl.BlockSpec((N_EXPERTS, LANES), lambda i: (0, 0)),
                   pl.BlockSpec((None, N_EXPERTS, LANES), lambda i: (i, 0, 0))],
        out_shape=[jax.ShapeDtypeStruct((ROUTE_ROWS, t), F32),
                   jax.ShapeDtypeStruct((N_EXPERTS, LANES), F32),
                   jax.ShapeDtypeStruct((t // tt, N_EXPERTS, LANES), F32)],
        scratch_shapes=[pltpu.VMEM((N_EXPERTS, tt), F32)],
        compiler_params=_cparams(("arbitrary",)),
    )(logits_t, tri)


def _row_copies(n_tok, issue_one):
    def body(t8, carry):
        for i in range(SUBLANES):
            issue_one(t8 * SUBLANES + i)
        return carry
    lax.fori_loop(0, n_tok // SUBLANES, body, 0)


SEG_BITS = 10
SEG_CNT, SEG_SRC, SEG_DST = 0, N_EXPERTS, 2 * N_EXPERTS


def _runs(seg_ref, staged, rows, sem, act):
    for e in range(N_EXPERTS):
        n = seg_ref[0, 0, SEG_CNT + e]
        src = seg_ref[0, 0, SEG_SRC + e]
        dst = seg_ref[0, 0, SEG_DST + e]
        for b in reversed(range(SEG_BITS)):
            size = (1 << b) * ROW_TILE
            done = (n >> (b + 1)) << (b + 1)

            @pl.when((n & (1 << b)) != 0)
            def _():
                act(staged.at[pl.ds(pl.multiple_of((src + done) * ROW_TILE, ROW_TILE), size)],
                    rows.at[pl.ds(pl.multiple_of((dst + done) * ROW_TILE, ROW_TILE), size)], sem)


def _dispatch_body(pe_ref, nu_ref, q_ref, seg_ref, x_ref, xs_ref, zbuf, stage, sems, zsem):
    tt = x_ref.shape[0] // ROW_TILE
    blk_rows = MOE_BLOCK * ROW_TILE
    n_blocks = xs_ref.shape[0] // blk_rows

    @pl.when(pl.program_id(0) == 0)
    def _():
        zbuf[...] = jnp.zeros_like(zbuf)

        def zero_block(row):
            return pltpu.make_async_copy(
                zbuf, xs_ref.at[pl.ds(pl.multiple_of(row * ROW_TILE, blk_rows), blk_rows)], zsem)

        def each(act):
            for e in range(N_EXPERTS):
                @pl.when(pe_ref[e] >= MOE_BLOCK)
                def _():
                    act(zero_block(pe_ref[e] - MOE_BLOCK))

            def tail(i, carry):
                act(zero_block(i * MOE_BLOCK))
                return carry
            lax.fori_loop(nu_ref[0], n_blocks, tail, 0)

        each(lambda c: c.start())
        each(lambda c: c.wait())

    i = pl.program_id(0)

    def wait_tile(s):
        for _ in range(TOP_K):
            pltpu.make_async_copy(x_ref, xs_ref.at[pl.ds(0, tt * ROW_TILE)], sems.at[s]).wait()

    for slot in range(2):
        @pl.when(i % 2 == slot)
        def _():
            def place(t):
                row = x_ref[pl.ds(pl.multiple_of(t * ROW_TILE, ROW_TILE), ROW_TILE), :]
                for k in range(TOP_K):
                    q = pl.multiple_of(q_ref[0, 0, k * tt + t], ROW_TILE)
                    stage[slot, pl.ds(q, ROW_TILE), :] = row

            _row_copies(tt, place)

            _runs(seg_ref, stage.at[slot], xs_ref, sems.at[slot],
                  lambda src, dst, s: pltpu.make_async_copy(src, dst, s).start())

            @pl.when(i > 0)
            def _():
                wait_tile(1 - slot)

            @pl.when(i == pl.num_programs(0) - 1)
            def _():
                wait_tile(slot)


def _dispatch(pad_ends, n_used, x1r, q3, seg3, n_rows):
    t = x1r.shape[0] // ROW_TILE
    tt = q3.shape[2] // TOP_K
    smem = lambda a: pl.BlockSpec((1, 1, a.shape[2]), lambda i, pe, nu: (i, 0, 0), memory_space=pltpu.SMEM)
    grid_spec = pltpu.PrefetchScalarGridSpec(
        num_scalar_prefetch=2,
        grid=(t // tt,),
        in_specs=[smem(q3), smem(seg3),
                  pl.BlockSpec((tt * ROW_TILE, LANES), lambda i, pe, nu: (i, 0))],
        out_specs=pl.BlockSpec(memory_space=pl.ANY),
        scratch_shapes=[pltpu.VMEM((MOE_BLOCK * ROW_TILE, LANES), F32),
                        pltpu.VMEM((2, TOP_K * tt * ROW_TILE, LANES), F32),
                        pltpu.SemaphoreType.DMA((2,)), pltpu.SemaphoreType.DMA(())],
    )
    return pl.pallas_call(
        _dispatch_body,
        name="dispatch",
        grid_spec=grid_spec,
        out_shape=jax.ShapeDtypeStruct((n_rows * ROW_TILE, LANES), F32),
        compiler_params=_cparams(("arbitrary",)),
    )(pad_ends, n_used, q3, seg3, x1r)


def _ffn_body(b0_ref, nb_ref, nu_ref, g_ref, wgu_ref, bgu_ref, wd_ref, bd_ref, xs_ref, ys_ref,
              xbuf, ybuf, in_sems, out_sems, wgu_bf, wd_bf):
    e = pl.program_id(0)
    blk_rows = MOE_BLOCK * ROW_TILE
    n_blocks = xs_ref.shape[0] // blk_rows
    first = b0_ref[e]
    n_used = nu_ref[0]

    def rows(b):
        return pl.ds(pl.multiple_of(b * blk_rows, blk_rows), blk_rows)

    def in_copy(b):
        return pltpu.make_async_copy(xs_ref.at[rows(b)], xbuf.at[b % 2], in_sems.at[b % 2])

    def out_copy(b):
        return pltpu.make_async_copy(ybuf.at[b % 2], ys_ref.at[rows(b)], out_sems.at[b % 2])

    @pl.when(e == 0)
    def _():
        in_copy(0).start(priority=1)

    wgu_bf[...] = wgu_ref[...].astype(BF16)
    wd_bf[...] = wd_ref[...].astype(BF16)

    def block(j, carry):
        b = first + j
        in_copy(b).wait()

        @pl.when(b + 1 < n_used)
        def _():
            in_copy(b + 1).start(priority=1)

        h = _rms(_load_row_tiles(xbuf.at[b % 2], MOE_BLOCK), g_ref[...]).astype(BF16)
        gu = jnp.dot(h, wgu_bf[...], preferred_element_type=F32) + bgu_ref[...]
        gate = jnp.minimum(gu[:, :D_FF], SWIGLU_LIMIT)
        up = jnp.clip(gu[:, D_FF:], -SWIGLU_LIMIT, SWIGLU_LIMIT)
        act = (up + 1.0) * gate * jax.nn.sigmoid(SWIGLU_ALPHA * gate)
        y = jnp.dot(act.astype(BF16), wd_bf[...], preferred_element_type=F32) + bd_ref[...]

        @pl.when(b >= 2)
        def _():
            out_copy(b - 2).wait()

        _store_row_tiles(ybuf.at[b % 2], y)
        out_copy(b).start(priority=1)
        return carry

    lax.fori_loop(0, nb_ref[e], block, 0)

    @pl.when(e == pl.num_programs(0) - 1)
    def _():
        @pl.when(n_used >= 2)
        def _():
            out_copy(n_used - 2).wait()
        out_copy(n_used - 1).wait()

        ybuf[0] = jnp.zeros((blk_rows, LANES), F32)

        def tail(act):
            def body(i, carry):
                act(pltpu.make_async_copy(
                    ybuf.at[0], ys_ref.at[pl.ds(pl.multiple_of(i * blk_rows, blk_rows), blk_rows)],
                    out_sems.at[0]))
                return carry
            lax.fori_loop(nu_ref[0], n_blocks, body, 0)

        tail(lambda c: c.start())
        tail(lambda c: c.wait())


def _moe_ffn(blk_first, blk_count, n_used, xs, norm2_g, w_gu, b_gu, w_d, b_d):
    blk_rows = MOE_BLOCK * ROW_TILE
    by_e = lambda e, b0, nb, nu: (e, 0, 0)
    grid_spec = pltpu.PrefetchScalarGridSpec(
        num_scalar_prefetch=3,
        grid=(N_EXPERTS,),
        in_specs=[
            pl.BlockSpec((1, D_MODEL), lambda e, b0, nb, nu: (0, 0)),
            pl.BlockSpec((None, D_MODEL, 2 * D_FF), by_e),
            pl.BlockSpec((None, 1, 2 * D_FF), by_e),
            pl.BlockSpec((None, D_FF, D_MODEL), by_e),
            pl.BlockSpec((None, 1, D_MODEL), by_e),
            pl.BlockSpec(memory_space=pl.ANY),
        ],
        out_specs=pl.BlockSpec(memory_space=pl.ANY),
        scratch_shapes=[pltpu.VMEM((2, blk_rows, LANES), F32), pltpu.VMEM((2, blk_rows, LANES), F32),
                        pltpu.SemaphoreType.DMA((2,)), pltpu.SemaphoreType.DMA((2,)),
                        pltpu.VMEM((D_MODEL, 2 * D_FF), BF16), pltpu.VMEM((D_FF, D_MODEL), BF16)],
    )
    return pl.pallas_call(
        _ffn_body,
        name="moe_ffn",
        grid_spec=grid_spec,
        out_shape=jax.ShapeDtypeStruct(xs.shape, F32),
        compiler_params=_cparams(("arbitrary",)),
    )(blk_first, blk_count, n_used, norm2_g, w_gu, b_gu, w_d, b_d, xs)


def _combine_body(seg_ref, q_ref, gate_ref, x_ref, ys_ref, o_ref, stage, acc_rt, sems):
    i = pl.program_id(0)
    n_tiles = pl.num_programs(0) - 1
    tt = o_ref.shape[0]

    def fetch(slot):
        _runs(seg_ref, stage.at[slot], ys_ref, sems.at[slot],
              lambda staged, rows, s: pltpu.make_async_copy(rows, staged, s).start())

    def finish(slot):
        for _ in range(TOP_K):
            pltpu.make_async_copy(ys_ref.at[pl.ds(0, tt * ROW_TILE)], acc_rt, sems.at[slot]).wait()

        def token(t):
            row = pl.ds(pl.multiple_of(t * ROW_TILE, ROW_TILE), ROW_TILE)
            acc = x_ref[row, :]
            for k in range(TOP_K):
                q = pl.multiple_of(q_ref[0, 0, k * tt + t], ROW_TILE)
                acc = acc + gate_ref[0, 0, k * tt + t] * stage[slot, pl.ds(q, ROW_TILE), :]
            acc_rt[row, :] = acc

        _row_copies(tt, token)
        o_ref[...] = _load_row_tiles(acc_rt, tt)

    for slot in range(2):
        @pl.when((i < n_tiles) & (i % 2 == slot))
        def _():
            fetch(slot)

        @pl.when((i > 0) & ((i - 1) % 2 == slot))
        def _():
            finish(slot)


def _combine(x1r, gates3, q3, seg3, ys):
    t = x1r.shape[0] // ROW_TILE
    tt = q3.shape[2] // TOP_K
    n_tiles = t // tt
    ahead = lambda i: (jnp.minimum(i, n_tiles - 1), 0, 0)
    behind3 = lambda i: (jnp.maximum(i - 1, 0), 0, 0)
    behind = lambda i: (jnp.maximum(i - 1, 0), 0)
    smem = lambda a, idx: pl.BlockSpec((1, 1, a.shape[2]), idx, memory_space=pltpu.SMEM)
    return pl.pallas_call(
        _combine_body,
        name="combine",
        grid=(n_tiles + 1,),
        in_specs=[smem(seg3, ahead), smem(q3, behind3), smem(gates3, behind3),
                  pl.BlockSpec((tt * ROW_TILE, LANES), behind),
                  pl.BlockSpec(memory_space=pl.ANY)],
        out_specs=pl.BlockSpec((tt, D_MODEL), behind),
        out_shape=jax.ShapeDtypeStruct((t, D_MODEL), F32),
        scratch_shapes=[pltpu.VMEM((2, TOP_K * tt * ROW_TILE, LANES), F32),
                        pltpu.VMEM((tt * ROW_TILE, LANES), F32),
                        pltpu.SemaphoreType.DMA((2,))],
        compiler_params=_cparams(("arbitrary",)),
    )(seg3, q3, gates3, x1r, ys)


def _moe(x1, logits, norm2_g, w_gate_up, b_gate_up, w_down, b_down):
    t = x1.shape[0] // ROW_TILE
    n_assign = t * TOP_K
    n_blocks = -(-n_assign // MOE_BLOCK) + N_EXPERTS
    n_rows = n_blocks * MOE_BLOCK

    routed, cnt, tile_start = _route(logits)
    ids = routed[:TOP_K].astype(jnp.int32)
    pos = routed[2 * TOP_K:3 * TOP_K].astype(jnp.int32)
    gates = routed[TOP_K:2 * TOP_K]
    counts = cnt[:, 0].astype(jnp.int32)
    padded = (counts + MOE_BLOCK - 1) // MOE_BLOCK * MOE_BLOCK
    pad_ends = jnp.cumsum(padded)
    pad_starts = pad_ends - padded
    n_used = (pad_ends[-1] // MOE_BLOCK).astype(jnp.int32).reshape(1)

    tt = min(TT_MOVE, t)
    n_tiles = t // tt
    by_tile = lambda a: a.reshape(TOP_K, n_tiles, tt).transpose(1, 0, 2).reshape(n_tiles, 1, TOP_K * tt)

    tile_start = tile_start[:, :, 0].astype(jnp.int32)
    tile_cnt = jnp.concatenate([tile_start[1:], counts[None]], axis=0) - tile_start
    tile_off = jnp.cumsum(tile_cnt, axis=1) - tile_cnt
    seg_dst = pad_starts[None, :] + tile_start
    seg3 = jnp.concatenate([tile_cnt, tile_off, seg_dst, jnp.zeros_like(tile_cnt)], axis=1)
    seg3 = seg3.reshape(n_tiles, 1, 4 * N_EXPERTS)
    experts = jnp.arange(N_EXPERTS, dtype=jnp.int32)[:, None, None]
    shift = jnp.repeat((tile_off - tile_start).T, tt, axis=1)
    q = jnp.sum(jnp.where(ids[None] == experts, shift[:, None, :], 0), axis=0) + pos
    q3 = by_tile(q * ROW_TILE)
    xs = _dispatch(pad_ends.astype(jnp.int32), n_used, x1, q3, seg3, n_rows)
    ys = _moe_ffn((pad_starts // MOE_BLOCK).astype(jnp.int32), (padded // MOE_BLOCK).astype(jnp.int32),
                  n_used, xs, norm2_g,
                  w_gate_up, b_gate_up.reshape(N_EXPERTS, 1, 2 * D_FF),
                  w_down, b_down.reshape(N_EXPERTS, 1, D_MODEL))
    return _combine(x1, by_tile(gates), q3, seg3, ys)


def _layer(x, norm1_g, w_in, ssm_a_re, ssm_a_im, ssm_log_dt, ssm_b_re, ssm_b_im, ssm_c_re,
           ssm_c_im, ssm_d, w_glu, w_ssm_proj, q_norm_g, k_norm_g, attn_sinks, w_attn_proj,
           w_out, norm2_g, w_router, b_router, w_gate_up, b_gate_up, w_down, b_down):
    n_b, seq, d = x.shape
    t = n_b * seq
    x2 = x.reshape(t, d)
    u, q, kd, vd, gs, ga = _in_proj(x2, norm1_g.reshape(1, d), w_in, q_norm_g, k_norm_g)

    abar_re, abar_im, bbt_re, bbt_im = _s5_prep(ssm_a_re, ssm_a_im, ssm_log_dt, ssm_b_re, ssm_b_im)
    b_mat, c_mat, ab = _s5_matrices(abar_re, abar_im, bbt_re, bbt_im, ssm_c_re, ssm_c_im)
    y = _s5(u.reshape(n_b, seq, SSM_WIDTH), b_mat, c_mat, ab, ssm_d.reshape(1, SSM_WIDTH))

    a = _attention(q.reshape(n_b, seq, ATTN_WIDTH), kd.reshape(n_b, seq, KV_DUP),
                   vd.reshape(n_b, seq, KV_DUP), attn_sinks)

    norm2 = norm2_g.reshape(1, d)
    x1, logits = _mix(y.reshape(t, SSM_WIDTH), a.reshape(t, ATTN_WIDTH), gs, ga, x2,
                      w_glu.astype(BF16), w_ssm_proj.astype(BF16), w_attn_proj.astype(BF16),
                      w_out.astype(BF16), norm2, w_router.T.astype(BF16), b_router)
    out = _moe(x1, logits, norm2, w_gate_up, b_gate_up, w_down, b_down)
    return out.reshape(n_b, seq, d)


def kernel(x, norm1_g, w_in, ssm_a_re, ssm_a_im, ssm_log_dt, ssm_b_re, ssm_b_im, ssm_c_re, ssm_c_im, ssm_d, w_glu, w_ssm_proj, q_norm_g, k_norm_g, attn_sinks, w_attn_proj, w_out, norm2_g, w_router, b_router, w_gate_up, b_gate_up, w_down, b_down):
    for layer in range(norm1_g.shape[0]):
        x = _layer(
            x, norm1_g[layer], w_in[layer], ssm_a_re[layer], ssm_a_im[layer], ssm_log_dt[layer],
            ssm_b_re[layer], ssm_b_im[layer], ssm_c_re[layer], ssm_c_im[layer], ssm_d[layer],
            w_glu[layer], w_ssm_proj[layer], q_norm_g[layer], k_norm_g[layer], attn_sinks[layer],
            w_attn_proj[layer], w_out[layer], norm2_g[layer], w_router[layer], b_router[layer],
            w_gate_up[layer], b_gate_up[layer], w_down[layer], b_down[layer])
    return x
```
